```python
import math
import jax, jax.numpy as jnp
from jax import lax
import numpy as np

D_MODEL = 4096
BATCH = 2
SEQ = 8192
DEPTH = 1

PLE_DIM = 256
HEAD_DIM = 128
N_HEADS = D_MODEL // 256
V_HEAD_DIM = 2 * HEAD_DIM
D_QK = N_HEADS * 2 * HEAD_DIM
D_ATTN = N_HEADS * V_HEAD_DIM
Q_BLOCK = 128
D_GMLP = D_MODEL
GMLP_GROUPS = 16
GROUP_DIM = D_GMLP // GMLP_GROUPS
CHUNK = 128
D_FF = -(-8 * D_MODEL // (3 * 256)) * 256
IN_SIZES = (D_GMLP, D_GMLP, D_QK, D_QK, D_ATTN, D_MODEL, D_MODEL)
IN_TOTAL = 2 * D_GMLP + 2 * D_QK + D_ATTN + 2 * D_MODEL
RMS_EPS = 1e-6
LN_EPS = 1e-5

kernel_name = "hybrid_gmlp_diffattn_gated_block"


def _rmsnorm(x, g, eps=RMS_EPS):
    xf = x.astype(jnp.float32)
    y = xf * lax.rsqrt(jnp.mean(xf * xf, axis=-1, keepdims=True) + eps)
    return (y * g.astype(jnp.float32)).astype(x.dtype)


def _layernorm(x, g, b, eps=LN_EPS):
    xf = x.astype(jnp.float32)
    mu = jnp.mean(xf, axis=-1, keepdims=True)
    xc = xf - mu
    var = jnp.mean(xc * xc, axis=-1, keepdims=True)
    y = xc * lax.rsqrt(var + eps) * g.astype(jnp.float32) + b.astype(jnp.float32)
    return y.astype(x.dtype)


def _alibi_slopes(n_heads):
    return jnp.exp2(-8.0 * jnp.arange(1, n_heads + 1, dtype=jnp.float32) / n_heads)


def _spatial_gating(u, v, w_s, b_s, ln_g, ln_b):
    bsz, s, _ = v.shape
    n_chunks = s // CHUNK
    vn = _layernorm(v, ln_g, ln_b).reshape(bsz, n_chunks, CHUNK, GMLP_GROUPS, GROUP_DIM)
    causal = jnp.tril(jnp.ones((CHUNK, CHUNK), dtype=w_s.dtype))
    w = w_s * causal
    mixed = jnp.einsum('gts,bcsgd->bctgd', w, vn) + b_s.T[:, :, None]
    return u * mixed.reshape(bsz, s, D_GMLP)


def _diff_attention(q, k, v, lam, slopes):
    bsz, s = q.shape[0], q.shape[1]
    n_blocks = s // Q_BLOCK
    scale = HEAD_DIM ** -0.5
    k_pos = jnp.arange(s)

    def block(i):
        start = i * Q_BLOCK
        qb = lax.dynamic_slice_in_dim(q, start, Q_BLOCK, axis=1)
        scores = jnp.einsum('bqhmd,bkhmd->bhmqk', qb, k,
                            preferred_element_type=jnp.float32) * scale
        q_pos = start + jnp.arange(Q_BLOCK)
        dist = (q_pos[:, None] - k_pos[None, :]).astype(jnp.float32)
        bias = -slopes[:, None, None, None] * dist
        scores = jnp.where(dist >= 0, scores + bias, -jnp.inf)
        probs = jax.nn.softmax(scores, axis=-1).astype(v.dtype)
        o = jnp.einsum('bhmqk,bkhe->bqhme', probs, v)
        return o[:, :, :, 0] - lam.astype(o.dtype) * o[:, :, :, 1]

    out = lax.map(block, jnp.arange(n_blocks))
    return out.transpose(1, 0, 2, 3, 4).reshape(bsz, s, N_HEADS, V_HEAD_DIM)


def setup_inputs(seed: int = 0) -> dict:
    key = jax.random.key(seed)
    ks = jax.random.split(key, 26)
    f32 = jnp.float32

    def nrm(k, shape, scale):
        return jax.random.normal(k, shape, f32) * scale

    L = DEPTH
    return {
        'x': nrm(ks[0], (BATCH, SEQ, D_MODEL), 1.0),
        'p': nrm(ks[1], (DEPTH, BATCH, SEQ, PLE_DIM), 1.0),
        'g_mix': 1.0 + nrm(ks[2], (L, D_MODEL), 0.02),
        'w_in': nrm(ks[3], (L, D_MODEL, IN_TOTAL), D_MODEL ** -0.5),
        'b_gate': nrm(ks[4], (L, 2, D_MODEL), 0.02),
        'ln_v_g': 1.0 + nrm(ks[5], (L, D_GMLP), 0.02),
        'ln_v_b': nrm(ks[6], (L, D_GMLP), 0.02),
        'w_s': nrm(ks[7], (L, GMLP_GROUPS, CHUNK, CHUNK), CHUNK ** -0.5),
        'b_s': nrm(ks[8], (L, GMLP_GROUPS, CHUNK), 0.02),
        'lambda_q1': nrm(ks[9], (L, HEAD_DIM), 0.1),
        'lambda_k1': nrm(ks[10], (L, HEAD_DIM), 0.1),
        'lambda_q2': nrm(ks[11], (L, HEAD_DIM), 0.1),
        'lambda_k2': nrm(ks[12], (L, HEAD_DIM), 0.1),
        'subln_g': 1.0 + nrm(ks[13], (L, V_HEAD_DIM), 0.02),
        'w_br_a': nrm(ks[14], (L, D_GMLP, D_MODEL), D_GMLP ** -0.5),
        'w_br_b': nrm(ks[15], (L, D_ATTN, D_MODEL), D_ATTN ** -0.5),
        'w_o': nrm(ks[16], (L, D_MODEL, D_MODEL), D_MODEL ** -0.5),
        'g_ffn': 1.0 + nrm(ks[17], (L, D_MODEL), 0.02),
        'w_gu': nrm(ks[18], (L, D_MODEL, 2 * D_FF), D_MODEL ** -0.5),
        'w_down': nrm(ks[19], (L, D_FF, D_MODEL), D_FF ** -0.5),
        'g_ple': 1.0 + nrm(ks[20], (L, D_MODEL), 0.02),
        'w_ple_gate': nrm(ks[21], (L, D_MODEL, D_MODEL), D_MODEL ** -0.5),
        'w_ple_proj': nrm(ks[22], (L, PLE_DIM, D_MODEL), PLE_DIM ** -0.5),
        'g_final': 1.0 + nrm(ks[23], (D_MODEL,), 0.02),
    }


def reference(x, p, g_mix, w_in, b_gate, ln_v_g, ln_v_b, w_s, b_s,
              lambda_q1, lambda_k1, lambda_q2, lambda_k2, subln_g,
              w_br_a, w_br_b, w_o, g_ffn, w_gu, w_down,
              g_ple, w_ple_gate, w_ple_proj, g_final):
    bsz, s, _ = x.shape
    split_points = np.cumsum(IN_SIZES)[:-1].tolist()
    slopes = _alibi_slopes(N_HEADS)
    for i in range(DEPTH):
        h = _rmsnorm(x, g_mix[i])
        proj = h @ w_in[i]
        u_a, v_a, q, k, v_b, gate_a, gate_b = jnp.split(proj, split_points, axis=-1)

        y_a = _spatial_gating(jax.nn.gelu(u_a, approximate=False),
                              jax.nn.gelu(v_a, approximate=False),
                              w_s[i], b_s[i], ln_v_g[i], ln_v_b[i])

        lam_init = 0.8 - 0.6 * math.exp(-0.3 * i)
        lam = (jnp.exp(jnp.sum(lambda_q1[i].astype(jnp.float32) * lambda_k1[i].astype(jnp.float32)))
               - jnp.exp(jnp.sum(lambda_q2[i].astype(jnp.float32) * lambda_k2[i].astype(jnp.float32)))
               + lam_init)
        qh = q.reshape(bsz, s, N_HEADS, 2, HEAD_DIM)
        kh = k.reshape(bsz, s, N_HEADS, 2, HEAD_DIM)
        vh = v_b.reshape(bsz, s, N_HEADS, V_HEAD_DIM)
        o = _diff_attention(qh, kh, vh, lam, slopes)
        y_b = (_rmsnorm(o, subln_g[i], eps=LN_EPS) * (1.0 - lam_init)).reshape(bsz, s, D_ATTN)

        merged = (jax.nn.sigmoid(gate_a + b_gate[i, 0]) * (y_a @ w_br_a[i])
                  + jax.nn.sigmoid(gate_b + b_gate[i, 1]) * (y_b @ w_br_b[i]))
        x = x + merged @ w_o[i]

        h = _rmsnorm(x, g_ffn[i])
        g_ff, u_ff = jnp.split(h @ w_gu[i], 2, axis=-1)
        x = x + (jax.nn.silu(g_ff) * u_ff) @ w_down[i]

        x = x + jax.nn.sigmoid(_rmsnorm(x, g_ple[i]) @ w_ple_gate[i]) * (p[i] @ w_ple_proj[i])
    return _rmsnorm(x, g_final)
```

```python
import functools
import math

import jax
import jax.numpy as jnp
from jax import lax
from jax.experimental import pallas as pl
from jax.experimental.pallas import tpu as pltpu

F32 = jnp.float32
BF16 = jnp.bfloat16

V7X_VMEM_LIMIT_BYTES = 56 * 1024 * 1024

HEAD_DIM = 128
V_HEAD_DIM = 2 * HEAD_DIM
CHUNK = 128
RMS_EPS = 1e-6
LN_EPS = 1e-5
MASK_VALUE = -1e30


def _params(*semantics):
    return pltpu.CompilerParams(dimension_semantics=semantics,
                                vmem_limit_bytes=V7X_VMEM_LIMIT_BYTES)


def _rmsnorm_kernel(x_ref, g_ref, o_ref, *, eps):
    x = x_ref[...]
    ms = jnp.mean(x * x, axis=-1, keepdims=True)
    o_ref[...] = (x * lax.rsqrt(ms + eps) * g_ref[...]).astype(o_ref.dtype)


def _rmsnorm(x, g, out_dtype, *, rows=256):
    t, d = x.shape
    return pl.pallas_call(
        functools.partial(_rmsnorm_kernel, eps=RMS_EPS),
        out_shape=jax.ShapeDtypeStruct((t, d), out_dtype),
        grid=(t // rows,),
        in_specs=[pl.BlockSpec((rows, d), lambda i: (i, 0)),
                  pl.BlockSpec((1, d), lambda i: (0, 0))],
        out_specs=pl.BlockSpec((rows, d), lambda i: (i, 0)),
        compiler_params=_params("parallel"),
        name="rmsnorm",
    )(x, g.reshape(1, d))


def _mm_kernel(*refs, n_x, n_w, dot_pairs, epilogue):
    x_refs = refs[:n_x]
    w_refs = refs[n_x:n_x + n_w]
    extra_refs = refs[n_x + n_w:-1]
    o_ref = refs[-1]
    accs = [jnp.dot(x_refs[a][...].astype(BF16), w_refs[b][...],
                    preferred_element_type=F32) for a, b in dot_pairs]
    o_ref[...] = epilogue(accs, extra_refs).astype(o_ref.dtype)


def _matmul(name, xs, ws, extras, epilogue, *, dot_pairs, n_out, out_dtype, bm, bn):
    t = xs[0][0].shape[0]
    in_specs, operands = [], []
    for arr, k, cb in xs:
        in_specs.append(pl.BlockSpec((bm, k), lambda i, j, cb=cb: (i, cb)))
        operands.append(arr)
    for arr, cb0 in ws:
        k = arr.shape[0]
        in_specs.append(pl.BlockSpec((k, bn), lambda i, j, cb0=cb0: (0, cb0 + j)))
        operands.append(arr)
    for kind, arr, cb0 in extras:
        if kind == "tile":
            in_specs.append(pl.BlockSpec((bm, bn), lambda i, j, cb0=cb0: (i, cb0 + j)))
        else:
            in_specs.append(pl.BlockSpec((1, bn), lambda i, j, cb0=cb0: (0, cb0 + j)))
        operands.append(arr)
    return pl.pallas_call(
        functools.partial(_mm_kernel, n_x=len(xs), n_w=len(ws),
                          dot_pairs=dot_pairs, epilogue=epilogue),
        out_shape=jax.ShapeDtypeStruct((t, n_out), out_dtype),
        grid=(t // bm, n_out // bn),
        in_specs=in_specs,
        out_specs=pl.BlockSpec((bm, bn), lambda i, j: (i, j)),
        compiler_params=_params("parallel", "parallel"),
        name=name,
    )(*operands)


def _gelu_exact(x):
    return 0.5 * x * (1.0 + lax.erf(x * math.sqrt(0.5)))


def _ep_gelu(accs, extras):
    return _gelu_exact(accs[0])


def _ep_colscale(accs, extras):
    return accs[0] * extras[0][...]


def _ep_sigmoid_bias(accs, extras):
    return jax.nn.sigmoid(accs[0] + extras[0][...])


def _ep_gated_merge(accs, extras):
    return (extras[0][...].astype(F32) * accs[0]
            + extras[1][...].astype(F32) * accs[1])


def _ep_residual(accs, extras):
    return extras[0][...] + accs[0]


def _ep_swiglu(accs, extras):
    return jax.nn.silu(accs[0]) * accs[1]


def _ep_ple(accs, extras):
    return extras[0][...] + jax.nn.sigmoid(accs[0]) * accs[1]


def _mm_ksplit_kernel(x_ref, w_ref, r_ref, o_ref, acc_ref):
    k = pl.program_id(2)
    part = jnp.dot(x_ref[...], w_ref[...], preferred_element_type=F32)

    @pl.when(k == 0)
    def _():
        acc_ref[...] = part

    @pl.when(k != 0)
    def _():
        acc_ref[...] += part

    @pl.when(k == pl.num_programs(2) - 1)
    def _():
        o_ref[...] = r_ref[...] + acc_ref[...]


def _matmul_ksplit_residual(x, w, resid, *, bm, bn, bk):
    t, kdim = x.shape
    n = w.shape[1]
    return pl.pallas_call(
        _mm_ksplit_kernel,
        out_shape=jax.ShapeDtypeStruct((t, n), F32),
        grid=(t // bm, n // bn, kdim // bk),
        in_specs=[pl.BlockSpec((bm, bk), lambda i, j, k: (i, k)),
                  pl.BlockSpec((bk, bn), lambda i, j, k: (k, j)),
                  pl.BlockSpec((bm, bn), lambda i, j, k: (i, j))],
        out_specs=pl.BlockSpec((bm, bn), lambda i, j, k: (i, j)),
        scratch_shapes=[pltpu.VMEM((bm, bn), F32)],
        compiler_params=_params("parallel", "parallel", "arbitrary"),
        name="ffn_down",
    )(x, w, resid)


def _gmlp_kernel(gu_ref, gv_ref, lng_ref, lnb_ref, ws_ref, bs_ref, o_ref, wm_ref,
                 *, rows, n_groups, group_dim):
    @pl.when(pl.program_id(0) == 0)
    def _():
        r = lax.broadcasted_iota(jnp.int32, (CHUNK, CHUNK), 0)
        c = lax.broadcasted_iota(jnp.int32, (CHUNK, CHUNK), 1)
        for g in range(n_groups):
            wm_ref[g] = jnp.where(c <= r, ws_ref[g], 0.0).astype(BF16)

    for ch in range(rows // CHUNK):
        rs = slice(ch * CHUNK, (ch + 1) * CHUNK)
        v = gv_ref[rs, :].astype(F32)
        mu = jnp.mean(v, axis=-1, keepdims=True)
        vc = v - mu
        var = jnp.mean(vc * vc, axis=-1, keepdims=True)
        vn = (vc * lax.rsqrt(var + LN_EPS) * lng_ref[...] + lnb_ref[...]).astype(BF16)
        for g in range(n_groups):
            cs = slice(g * group_dim, (g + 1) * group_dim)
            mixed = jnp.dot(wm_ref[g], vn[:, cs], preferred_element_type=F32) + bs_ref[g]
            o_ref[rs, cs] = (gu_ref[rs, cs].astype(F32) * mixed).astype(o_ref.dtype)


def _gmlp(guv, ln_g, ln_b, w_s, b_s, *, rows=256):
    t = guv.shape[0]
    d = guv.shape[1] // 2
    n_groups = w_s.shape[0]
    return pl.pallas_call(
        functools.partial(_gmlp_kernel, rows=rows, n_groups=n_groups, group_dim=d // n_groups),
        out_shape=jax.ShapeDtypeStruct((t, d), BF16),
        grid=(t // rows,),
        in_specs=[pl.BlockSpec((rows, d), lambda i: (i, 0)),
                  pl.BlockSpec((rows, d), lambda i: (i, 1)),
                  pl.BlockSpec((1, d), lambda i: (0, 0)),
                  pl.BlockSpec((1, d), lambda i: (0, 0)),
                  pl.BlockSpec((n_groups, CHUNK, CHUNK), lambda i: (0, 0, 0)),
                  pl.BlockSpec((n_groups, CHUNK, 1), lambda i: (0, 0, 0))],
        out_specs=pl.BlockSpec((rows, d), lambda i: (i, 0)),
        scratch_shapes=[pltpu.VMEM((n_groups, CHUNK, CHUNK), BF16)],
        compiler_params=_params("arbitrary"),
        name="gmlp_gating",
    )(guv, guv, ln_g.reshape(1, d), ln_b.reshape(1, d), w_s, b_s.reshape(n_groups, CHUNK, 1))


def _attn_kernel(q_ref, k_ref, v_ref, lq1_ref, lk1_ref, lq2_ref, lk2_ref, sg_ref, o_ref,
                 m_ref, l_ref, acc_ref, *, blk, n_heads, lam_init):
    h = pl.program_id(1)
    i = pl.program_id(2)
    slope = jnp.exp2(-8.0 * jnp.full((1, blk), h + 1, F32) / n_heads)
    k_idx = lax.broadcasted_iota(jnp.int32, (1, blk), 1).astype(F32)

    m_ref[...] = jnp.full(m_ref.shape, MASK_VALUE, F32)
    l_ref[...] = jnp.zeros(l_ref.shape, F32)
    acc_ref[...] = jnp.zeros(acc_ref.shape, F32)

    def step(j, masked):
        off = pl.multiple_of(j * blk, blk)
        bias = slope * (k_idx + ((j - i) * blk).astype(F32))
        v_blk = v_ref[pl.ds(off, blk), :]
        for mp in range(2):
            cs = slice(mp * HEAD_DIM, (mp + 1) * HEAD_DIM)
            s = lax.dot_general(q_ref[:, cs], k_ref[pl.ds(off, blk), cs],
                                (((1,), (1,)), ((), ())),
                                preferred_element_type=F32) + bias
            if masked:
                row = lax.broadcasted_iota(jnp.int32, (blk, blk), 0)
                col = lax.broadcasted_iota(jnp.int32, (blk, blk), 1)
                s = jnp.where(col <= row, s, MASK_VALUE)
            m_old = m_ref[mp]
            m_new = jnp.maximum(m_old, jnp.max(s, axis=-1, keepdims=True))
            alpha = jnp.exp(m_old - m_new)
            p = jnp.exp(s - m_new)
            l_ref[mp] = alpha * l_ref[mp] + jnp.sum(p, axis=-1, keepdims=True)
            acc_ref[mp] = alpha * acc_ref[mp] + jnp.dot(
                p.astype(BF16), v_blk, preferred_element_type=F32)
            m_ref[mp] = m_new

    def body(j, carry):
        step(j, masked=False)
        return carry

    lax.fori_loop(0, i, body, 0)
    step(i, masked=True)

    lam = (jnp.exp(jnp.sum(lq1_ref[...] * lk1_ref[...], axis=-1, keepdims=True))
           - jnp.exp(jnp.sum(lq2_ref[...] * lk2_ref[...], axis=-1, keepdims=True))
           + lam_init)
    o = acc_ref[0] / l_ref[0] - lam * (acc_ref[1] / l_ref[1])
    ms = jnp.mean(o * o, axis=-1, keepdims=True)
    y = (o * lax.rsqrt(ms + LN_EPS) * sg_ref[...]) * (1.0 - lam_init)
    o_ref[...] = y.astype(o_ref.dtype)


def _diff_attention(qkv, lq1, lk1, lq2, lk2, subln_g, *, batch, seq, n_heads, lam_init, blk=512):
    t = qkv.shape[0]
    nq = seq // blk
    vec = lambda a: a.reshape(1, -1)
    vec_spec = lambda n: pl.BlockSpec((1, n), lambda b, h, i: (0, 0))
    return pl.pallas_call(
        functools.partial(_attn_kernel, blk=blk, n_heads=n_heads, lam_init=lam_init),
        out_shape=jax.ShapeDtypeStruct((t, n_heads * V_HEAD_DIM), BF16),
        grid=(batch, n_heads, nq),
        in_specs=[pl.BlockSpec((blk, 2 * HEAD_DIM), lambda b, h, i: (b * nq + i, h)),
                  pl.BlockSpec((seq, 2 * HEAD_DIM), lambda b, h, i: (b, n_heads + h)),
                  pl.BlockSpec((seq, V_HEAD_DIM), lambda b, h, i: (b, 2 * n_heads + h)),
                  vec_spec(HEAD_DIM), vec_spec(HEAD_DIM), vec_spec(HEAD_DIM), vec_spec(HEAD_DIM),
                  vec_spec(V_HEAD_DIM)],
        out_specs=pl.BlockSpec((blk, V_HEAD_DIM), lambda b, h, i: (b * nq + i, h)),
        scratch_shapes=[pltpu.VMEM((2, blk, 1), F32),
                        pltpu.VMEM((2, blk, 1), F32),
                        pltpu.VMEM((2, blk, V_HEAD_DIM), F32)],
        compiler_params=_params("parallel", "parallel", "arbitrary"),
        name="diff_attention",
    )(qkv, qkv, qkv, vec(lq1), vec(lk1), vec(lq2), vec(lk2), vec(subln_g))


def kernel(x, p, g_mix, w_in, b_gate, ln_v_g, ln_v_b, w_s, b_s, lambda_q1, lambda_k1, lambda_q2, lambda_k2, subln_g, w_br_a, w_br_b, w_o, g_ffn, w_gu, w_down, g_ple, w_ple_gate, w_ple_proj, g_final):
    batch, seq, d_model = x.shape
    depth = w_in.shape[0]
    t = batch * seq
    d_gmlp = ln_v_g.shape[1]
    d_attn = w_br_b.shape[1]
    n_heads = d_attn // V_HEAD_DIM
    d_qk = n_heads * 2 * HEAD_DIM
    d_ff = w_down.shape[1]
    ple_dim = p.shape[-1]

    xf = x.reshape(t, d_model)
    for layer in range(depth):
        lam_init = 0.8 - 0.6 * math.exp(-0.3 * layer)
        w_in_b = w_in[layer].astype(BF16)

        h = _rmsnorm(xf, g_mix[layer], BF16)
        hx = [(h, d_model, 0)]
        bn = 1024
        guv = _matmul("in_proj_gelu", hx, [(w_in_b, 0)], [], _ep_gelu, dot_pairs=((0, 0),),
                      n_out=2 * d_gmlp, out_dtype=BF16, bm=1024, bn=bn)
        qkv_scale = jnp.concatenate([jnp.full((d_qk,), HEAD_DIM ** -0.5, F32),
                                     jnp.ones((d_qk + d_attn,), F32)]).reshape(1, -1)
        qkv = _matmul("in_proj_qkv", hx, [(w_in_b, 2 * d_gmlp // bn)], [("row", qkv_scale, 0)], _ep_colscale,
                      dot_pairs=((0, 0),), n_out=2 * d_qk + d_attn, out_dtype=BF16, bm=1024, bn=bn)
        gates = _matmul("in_proj_gates", hx, [(w_in_b, (2 * d_gmlp + 2 * d_qk + d_attn) // bn)],
                        [("row", b_gate[layer].reshape(1, -1), 0)], _ep_sigmoid_bias,
                        dot_pairs=((0, 0),), n_out=2 * d_model, out_dtype=BF16, bm=1024, bn=bn)

        y_a = _gmlp(guv, ln_v_g[layer], ln_v_b[layer], w_s[layer], b_s[layer])
        y_b = _diff_attention(qkv, lambda_q1[layer], lambda_k1[layer], lambda_q2[layer],
                              lambda_k2[layer], subln_g[layer], batch=batch, seq=seq,
                              n_heads=n_heads, lam_init=lam_init)

        bn = 512
        merged = _matmul("gated_merge", [(y_a, d_gmlp, 0), (y_b, d_attn, 0)],
                         [(w_br_a[layer].astype(BF16), 0), (w_br_b[layer].astype(BF16), 0)],
                         [("tile", gates, 0), ("tile", gates, d_model // bn)], _ep_gated_merge,
                         dot_pairs=((0, 0), (1, 1)), n_out=d_model, out_dtype=BF16, bm=512, bn=bn)
        xf = _matmul("out_proj", [(merged, d_model, 0)], [(w_o[layer].astype(BF16), 0)], [("tile", xf, 0)],
                     _ep_residual, dot_pairs=((0, 0),), n_out=d_model, out_dtype=F32,
                     bm=1024, bn=512)

        h = _rmsnorm(xf, g_ffn[layer], BF16)
        bn = 256
        w_gu_b = w_gu[layer].astype(BF16)
        act = _matmul("ffn_up", [(h, d_model, 0)], [(w_gu_b, 0), (w_gu_b, d_ff // bn)], [], _ep_swiglu,
                      dot_pairs=((0, 0), (0, 1)), n_out=d_ff, out_dtype=BF16, bm=1024, bn=bn)
        xf = _matmul_ksplit_residual(act, w_down[layer].astype(BF16), xf,
                                     bm=1024, bn=512, bk=d_ff // 2)

        h = _rmsnorm(xf, g_ple[layer], BF16)
        xf = _matmul("ple", [(h, d_model, 0), (p[layer].reshape(t, ple_dim), ple_dim, 0)],
                     [(w_ple_gate[layer].astype(BF16), 0), (w_ple_proj[layer].astype(BF16), 0)],
                     [("tile", xf, 0)], _ep_ple, dot_pairs=((0, 0), (1, 1)),
                     n_out=d_model, out_dtype=F32, bm=1024, bn=512)

    return _rmsnorm(xf, g_final, F32).reshape(batch, seq, d_model)
```

```python
import functools
import math

import jax
import jax.numpy as jnp
from jax import lax
from jax.experimental import pallas as pl
from jax.experimental.pallas import tpu as pltpu

F32 = jnp.float32
BF16 = jnp.bfloat16

V7X_VMEM_LIMIT_BYTES = 56 * 1024 * 1024

LANES = 128
LOG2_E = math.log2(math.e)

HEAD_DIM = 128
V_HEAD_DIM = 2 * HEAD_DIM
CHUNK = 128
RMS_EPS = 1e-6
LN_EPS = 1e-5
MASK_VALUE = -1e30


def _params(*semantics):
    return pltpu.CompilerParams(dimension_semantics=semantics,
                                vmem_limit_bytes=V7X_VMEM_LIMIT_BYTES)


def _rmsnorm_kernel(x_ref, g_ref, o_ref, *, eps):
    x = x_ref[...]
    ms = jnp.mean(x * x, axis=-1, keepdims=True)
    o_ref[...] = (x * lax.rsqrt(ms + eps) * g_ref[...]).astype(o_ref.dtype)


def _rmsnorm(x, g, out_dtype, *, rows=256):
    t, d = x.shape
    return pl.pallas_call(
        functools.partial(_rmsnorm_kernel, eps=RMS_EPS),
        out_shape=jax.ShapeDtypeStruct((t, d), out_dtype),
        grid=(t // rows,),
        in_specs=[pl.BlockSpec((rows, d), lambda i: (i, 0)),
                  pl.BlockSpec((1, d), lambda i: (0, 0))],
        out_specs=pl.BlockSpec((rows, d), lambda i: (i, 0)),
        compiler_params=_params("parallel"),
        name="rmsnorm",
    )(x, g.reshape(1, d))


def _mm_kernel(*refs, n_x, n_w, dot_pairs, epilogue):
    x_refs = refs[:n_x]
    w_refs = refs[n_x:n_x + n_w]
    extra_refs = refs[n_x + n_w:-1]
    o_ref = refs[-1]
    accs = [jnp.dot(x_refs[a][...].astype(BF16), w_refs[b][...],
                    preferred_element_type=F32) for a, b in dot_pairs]
    o_ref[...] = epilogue(accs, extra_refs).astype(o_ref.dtype)


def _matmul(name, xs, ws, extras, epilogue, *, dot_pairs, n_out, out_dtype, bm, bn):
    t = xs[0][0].shape[0]
    in_specs, operands = [], []
    for arr, k, cb in xs:
        in_specs.append(pl.BlockSpec((bm, k), lambda i, j, cb=cb: (i, cb)))
        operands.append(arr)
    for arr, cb0 in ws:
        k = arr.shape[0]
        in_specs.append(pl.BlockSpec((k, bn), lambda i, j, cb0=cb0: (0, cb0 + j)))
        operands.append(arr)
    for kind, arr, cb0 in extras:
        if kind == "tile":
            in_specs.append(pl.BlockSpec((bm, bn), lambda i, j, cb0=cb0: (i, cb0 + j)))
        else:
            in_specs.append(pl.BlockSpec((1, bn), lambda i, j, cb0=cb0: (0, cb0 + j)))
        operands.append(arr)
    return pl.pallas_call(
        functools.partial(_mm_kernel, n_x=len(xs), n_w=len(ws),
                          dot_pairs=dot_pairs, epilogue=epilogue),
        out_shape=jax.ShapeDtypeStruct((t, n_out), out_dtype),
        grid=(t // bm, n_out // bn),
        in_specs=in_specs,
        out_specs=pl.BlockSpec((bm, bn), lambda i, j: (i, j)),
        compiler_params=_params("parallel", "parallel"),
        name=name,
    )(*operands)


def _gelu_exact(x):
    return 0.5 * x * (1.0 + lax.erf(x * math.sqrt(0.5)))


def _ep_gelu(accs, extras):
    return _gelu_exact(accs[0])


def _ep_colscale(accs, extras):
    return accs[0] * extras[0][...]


def _ep_sigmoid_bias(accs, extras):
    return jax.nn.sigmoid(accs[0] + extras[0][...])


def _ep_gated_merge(accs, extras):
    return (extras[0][...].astype(F32) * accs[0]
            + extras[1][...].astype(F32) * accs[1])


def _ep_residual(accs, extras):
    return extras[0][...] + accs[0]


def _ep_swiglu(accs, extras):
    return jax.nn.silu(accs[0]) * accs[1]


def _ep_ple(accs, extras):
    return extras[0][...] + jax.nn.sigmoid(accs[0]) * accs[1]


def _mm_ksplit_kernel(x_ref, w_ref, r_ref, o_ref, acc_ref):
    k = pl.program_id(2)
    part = jnp.dot(x_ref[...], w_ref[...], preferred_element_type=F32)

    @pl.when(k == 0)
    def _():
        acc_ref[...] = part

    @pl.when(k != 0)
    def _():
        acc_ref[...] += part

    @pl.when(k == pl.num_programs(2) - 1)
    def _():
        o_ref[...] = r_ref[...] + acc_ref[...]


def _matmul_ksplit_residual(x, w, resid, *, bm, bn, bk):
    t, kdim = x.shape
    n = w.shape[1]
    return pl.pallas_call(
        _mm_ksplit_kernel,
        out_shape=jax.ShapeDtypeStruct((t, n), F32),
        grid=(t // bm, n // bn, kdim // bk),
        in_specs=[pl.BlockSpec((bm, bk), lambda i, j, k: (i, k)),
                  pl.BlockSpec((bk, bn), lambda i, j, k: (k, j)),
                  pl.BlockSpec((bm, bn), lambda i, j, k: (i, j))],
        out_specs=pl.BlockSpec((bm, bn), lambda i, j, k: (i, j)),
        scratch_shapes=[pltpu.VMEM((bm, bn), F32)],
        compiler_params=_params("parallel", "parallel", "arbitrary"),
        name="ffn_down",
    )(x, w, resid)


def _gmlp_kernel(gu_ref, gv_ref, lng_ref, lnb_ref, ws_ref, bs_ref, o_ref, wm_ref,
                 *, rows, n_groups, group_dim):
    @pl.when(pl.program_id(0) == 0)
    def _():
        r = lax.broadcasted_iota(jnp.int32, (CHUNK, CHUNK), 0)
        c = lax.broadcasted_iota(jnp.int32, (CHUNK, CHUNK), 1)
        for g in range(n_groups):
            wm_ref[g] = jnp.where(c <= r, ws_ref[g], 0.0).astype(BF16)

    for ch in range(rows // CHUNK):
        rs = slice(ch * CHUNK, (ch + 1) * CHUNK)
        v = gv_ref[rs, :].astype(F32)
        mu = jnp.mean(v, axis=-1, keepdims=True)
        vc = v - mu
        var = jnp.mean(vc * vc, axis=-1, keepdims=True)
        vn = (vc * lax.rsqrt(var + LN_EPS) * lng_ref[...] + lnb_ref[...]).astype(BF16)
        for g in range(n_groups):
            cs = slice(g * group_dim, (g + 1) * group_dim)
            mixed = jnp.dot(wm_ref[g], vn[:, cs], preferred_element_type=F32) + bs_ref[g]
            o_ref[rs, cs] = (gu_ref[rs, cs].astype(F32) * mixed).astype(o_ref.dtype)


def _gmlp(guv, ln_g, ln_b, w_s, b_s, *, rows=256):
    t = guv.shape[0]
    d = guv.shape[1] // 2
    n_groups = w_s.shape[0]
    return pl.pallas_call(
        functools.partial(_gmlp_kernel, rows=rows, n_groups=n_groups, group_dim=d // n_groups),
        out_shape=jax.ShapeDtypeStruct((t, d), BF16),
        grid=(t // rows,),
        in_specs=[pl.BlockSpec((rows, d), lambda i: (i, 0)),
                  pl.BlockSpec((rows, d), lambda i: (i, 1)),
                  pl.BlockSpec((1, d), lambda i: (0, 0)),
                  pl.BlockSpec((1, d), lambda i: (0, 0)),
                  pl.BlockSpec((n_groups, CHUNK, CHUNK), lambda i: (0, 0, 0)),
                  pl.BlockSpec((n_groups, CHUNK, 1), lambda i: (0, 0, 0))],
        out_specs=pl.BlockSpec((rows, d), lambda i: (i, 0)),
        scratch_shapes=[pltpu.VMEM((n_groups, CHUNK, CHUNK), BF16)],
        compiler_params=_params("arbitrary"),
        name="gmlp_gating",
    )(guv, guv, ln_g.reshape(1, d), ln_b.reshape(1, d), w_s, b_s.reshape(n_groups, CHUNK, 1))


def _lanes(x, n):
    return jnp.concatenate([x] * (n // LANES), axis=1)


def _attn_kernel(q_ref, k_ref, v_ref, lq1_ref, lk1_ref, lq2_ref, lk2_ref, sg_ref, o_ref,
                 s_ref, p_ref, m_ref, l_ref, a_ref, acc_ref, *, blk, strip, n_heads, lam_init):
    h = pl.program_id(1)
    i = pl.program_id(2)
    slope = jnp.exp2(-8.0 * jnp.full((1, blk), h + 1, F32) / n_heads) * LOG2_E
    k_idx = lax.broadcasted_iota(jnp.int32, (1, blk), 1).astype(F32)

    m_ref[...] = jnp.full(m_ref.shape, MASK_VALUE, F32)
    l_ref[...] = jnp.zeros(l_ref.shape, F32)
    acc_ref[...] = jnp.zeros(acc_ref.shape, F32)

    def softmax_strips(mp, bias, masked):
        for r in range(blk // strip):
            rows = slice(r * strip, (r + 1) * strip)
            s = s_ref[mp, rows, :] + bias
            if masked:
                row = r * strip + lax.broadcasted_iota(jnp.int32, (strip, blk), 0)
                col = lax.broadcasted_iota(jnp.int32, (strip, blk), 1)
                s = jnp.where(col <= row, s, MASK_VALUE)
            m_old = m_ref[mp, rows, :]
            m_new = jnp.maximum(m_old, jnp.max(s, axis=-1, keepdims=True))
            alpha = jnp.exp2(m_old - m_new)
            p = jnp.exp2(s - _lanes(m_new, blk))
            p_sum = p[:, :LANES]
            for c in range(1, blk // LANES):
                p_sum = p_sum + p[:, c * LANES:(c + 1) * LANES]
            l_ref[mp, rows, :] = alpha * l_ref[mp, rows, :] + p_sum
            m_ref[mp, rows, :] = m_new
            a_ref[mp, rows, :] = alpha
            p_ref[mp, rows, :] = p.astype(BF16)

    def step(j, masked):
        off = pl.multiple_of(j * blk, blk)
        bias = slope * (k_idx + ((j - i) * blk).astype(F32))
        for mp in range(2):
            cs = slice(mp * HEAD_DIM, (mp + 1) * HEAD_DIM)
            s_ref[mp] = lax.dot_general(q_ref[:, cs], k_ref[pl.ds(off, blk), cs],
                                        (((1,), (1,)), ((), ())),
                                        preferred_element_type=F32)
        v_blk = v_ref[pl.ds(off, blk), :]
        for mp in range(2):
            softmax_strips(mp, bias, masked)
            acc_ref[mp] = (acc_ref[mp] * _lanes(a_ref[mp], V_HEAD_DIM)
                           + jnp.dot(p_ref[mp], v_blk, preferred_element_type=F32))

    def body(j, carry):
        step(j, masked=False)
        return carry

    lax.fori_loop(0, i, body, 0)
    step(i, masked=True)

    lam = (jnp.exp(jnp.sum(lq1_ref[...] * lk1_ref[...], axis=-1, keepdims=True))
           - jnp.exp(jnp.sum(lq2_ref[...] * lk2_ref[...], axis=-1, keepdims=True))
           + lam_init)
    l1 = jnp.sum(l_ref[0], axis=-1, keepdims=True)
    l2 = jnp.sum(l_ref[1], axis=-1, keepdims=True)
    o = acc_ref[0] / l1 - lam * (acc_ref[1] / l2)
    ms = jnp.mean(o * o, axis=-1, keepdims=True)
    y = (o * lax.rsqrt(ms + LN_EPS) * sg_ref[...]) * (1.0 - lam_init)
    o_ref[...] = y.astype(o_ref.dtype)


def _diff_attention(qkv, lq1, lk1, lq2, lk2, subln_g, *, batch, seq, n_heads, lam_init,
                    blk=512, strip=32):
    t = qkv.shape[0]
    nq = seq // blk
    vec = lambda a: a.reshape(1, -1)
    vec_spec = lambda n: pl.BlockSpec((1, n), lambda b, h, i: (0, 0))
    return pl.pallas_call(
        functools.partial(_attn_kernel, blk=blk, strip=strip, n_heads=n_heads,
                          lam_init=lam_init),
        out_shape=jax.ShapeDtypeStruct((t, n_heads * V_HEAD_DIM), BF16),
        grid=(batch, n_heads, nq),
        in_specs=[pl.BlockSpec((blk, 2 * HEAD_DIM), lambda b, h, i: (b * nq + i, h)),
                  pl.BlockSpec((seq, 2 * HEAD_DIM), lambda b, h, i: (b, n_heads + h)),
                  pl.BlockSpec((seq, V_HEAD_DIM), lambda b, h, i: (b, 2 * n_heads + h)),
                  vec_spec(HEAD_DIM), vec_spec(HEAD_DIM), vec_spec(HEAD_DIM), vec_spec(HEAD_DIM),
                  vec_spec(V_HEAD_DIM)],
        out_specs=pl.BlockSpec((blk, V_HEAD_DIM), lambda b, h, i: (b * nq + i, h)),
        scratch_shapes=[pltpu.VMEM((2, blk, blk), F32),
                        pltpu.VMEM((2, blk, blk), BF16),
                        pltpu.VMEM((2, blk, LANES), F32),
                        pltpu.VMEM((2, blk, LANES), F32),
                        pltpu.VMEM((2, blk, LANES), F32),
                        pltpu.VMEM((2, blk, V_HEAD_DIM), F32)],
        compiler_params=_params("parallel", "parallel", "arbitrary"),
        name="diff_attention",
    )(qkv, qkv, qkv, vec(lq1), vec(lk1), vec(lq2), vec(lk2), vec(subln_g))


def kernel(x, p, g_mix, w_in, b_gate, ln_v_g, ln_v_b, w_s, b_s, lambda_q1, lambda_k1, lambda_q2, lambda_k2, subln_g, w_br_a, w_br_b, w_o, g_ffn, w_gu, w_down, g_ple, w_ple_gate, w_ple_proj, g_final):
    batch, seq, d_model = x.shape
    depth = w_in.shape[0]
    t = batch * seq
    d_gmlp = ln_v_g.shape[1]
    d_attn = w_br_b.shape[1]
    n_heads = d_attn // V_HEAD_DIM
    d_qk = n_heads * 2 * HEAD_DIM
    d_ff = w_down.shape[1]
    ple_dim = p.shape[-1]

    xf = x.reshape(t, d_model)
    for layer in range(depth):
        lam_init = 0.8 - 0.6 * math.exp(-0.3 * layer)
        w_in_b = w_in[layer].astype(BF16)

        h = _rmsnorm(xf, g_mix[layer], BF16)
        hx = [(h, d_model, 0)]
        bn = 1024
        guv = _matmul("in_proj_gelu", hx, [(w_in_b, 0)], [], _ep_gelu, dot_pairs=((0, 0),),
                      n_out=2 * d_gmlp, out_dtype=BF16, bm=1024, bn=bn)
        qkv_scale = jnp.concatenate([jnp.full((d_qk,), HEAD_DIM ** -0.5 * LOG2_E, F32),
                                     jnp.ones((d_qk + d_attn,), F32)]).reshape(1, -1)
        qkv = _matmul("in_proj_qkv", hx, [(w_in_b, 2 * d_gmlp // bn)], [("row", qkv_scale, 0)], _ep_colscale,
                      dot_pairs=((0, 0),), n_out=2 * d_qk + d_attn, out_dtype=BF16, bm=1024, bn=bn)
        gates = _matmul("in_proj_gates", hx, [(w_in_b, (2 * d_gmlp + 2 * d_qk + d_attn) // bn)],
                        [("row", b_gate[layer].reshape(1, -1), 0)], _ep_sigmoid_bias,
                        dot_pairs=((0, 0),), n_out=2 * d_model, out_dtype=BF16, bm=1024, bn=bn)

        y_a = _gmlp(guv, ln_v_g[layer], ln_v_b[layer], w_s[layer], b_s[layer])
        y_b = _diff_attention(qkv, lambda_q1[layer], lambda_k1[layer], lambda_q2[layer],
                              lambda_k2[layer], subln_g[layer], batch=batch, seq=seq,
                              n_heads=n_heads, lam_init=lam_init)

        bn = 512
        merged = _matmul("gated_merge", [(y_a, d_gmlp, 0), (y_b, d_attn, 0)],
                         [(w_br_a[layer].astype(BF16), 0), (w_br_b[layer].astype(BF16), 0)],
                         [("tile", gates, 0), ("tile", gates, d_model // bn)], _ep_gated_merge,
                         dot_pairs=((0, 0), (1, 1)), n_out=d_model, out_dtype=BF16, bm=512, bn=bn)
        xf = _matmul("out_proj", [(merged, d_model, 0)], [(w_o[layer].astype(BF16), 0)], [("tile", xf, 0)],
                     _ep_residual, dot_pairs=((0, 0),), n_out=d_model, out_dtype=F32,
                     bm=1024, bn=512)

        h = _rmsnorm(xf, g_ffn[layer], BF16)
        bn = 256
        w_gu_b = w_gu[layer].astype(BF16)
        act = _matmul("ffn_up", [(h, d_model, 0)], [(w_gu_b, 0), (w_gu_b, d_ff // bn)], [], _ep_swiglu,
                      dot_pairs=((0, 0), (0, 1)), n_out=d_ff, out_dtype=BF16, bm=1024, bn=bn)
        xf = _matmul_ksplit_residual(act, w_down[layer].astype(BF16), xf,
                                     bm=1024, bn=512, bk=d_ff // 2)

        h = _rmsnorm(xf, g_ple[layer], BF16)
        xf = _matmul("ple", [(h, d_model, 0), (p[layer].reshape(t, ple_dim), ple_dim, 0)],
                     [(w_ple_gate[layer].astype(BF16), 0), (w_ple_proj[layer].astype(BF16), 0)],
                     [("tile", xf, 0)], _ep_ple, dot_pairs=((0, 0), (1, 1)),
                     n_out=d_model, out_dtype=F32, bm=1024, bn=512)

    return _rmsnorm(xf, g_final, F32).reshape(batch, seq, d_model)
```

```python
import functools
import math

import jax
import jax.numpy as jnp
from jax import lax
from jax.experimental import pallas as pl
from jax.experimental.pallas import tpu as pltpu

F32 = jnp.float32
BF16 = jnp.bfloat16

V7X_VMEM_LIMIT_BYTES = 56 * 1024 * 1024

LANES = 128
LOG2_E = math.log2(math.e)

HEAD_DIM = 128
V_HEAD_DIM = 2 * HEAD_DIM
CHUNK = 128
RMS_EPS = 1e-6
LN_EPS = 1e-5
MASK_VALUE = -1e30


def _params(*semantics):
    return pltpu.CompilerParams(dimension_semantics=semantics,
                                vmem_limit_bytes=V7X_VMEM_LIMIT_BYTES)


def _rmsnorm_kernel(x_ref, g_ref, o_ref, *, eps):
    x = x_ref[...]
    ms = jnp.mean(x * x, axis=-1, keepdims=True)
    o_ref[...] = (x * lax.rsqrt(ms + eps) * g_ref[...]).astype(o_ref.dtype)


def _rmsnorm(x, g, out_dtype, *, rows=256):
    t, d = x.shape
    return pl.pallas_call(
        functools.partial(_rmsnorm_kernel, eps=RMS_EPS),
        out_shape=jax.ShapeDtypeStruct((t, d), out_dtype),
        grid=(t // rows,),
        in_specs=[pl.BlockSpec((rows, d), lambda i: (i, 0)),
                  pl.BlockSpec((1, d), lambda i: (0, 0))],
        out_specs=pl.BlockSpec((rows, d), lambda i: (i, 0)),
        compiler_params=_params("parallel"),
        name="rmsnorm",
    )(x, g.reshape(1, d))


def _mm_kernel(*refs, n_x, n_w, dot_pairs, epilogue):
    x_refs = refs[:n_x]
    w_refs = refs[n_x:n_x + n_w]
    extra_refs = refs[n_x + n_w:-1]
    o_ref = refs[-1]
    accs = [jnp.dot(x_refs[a][...].astype(BF16), w_refs[b][...],
                    preferred_element_type=F32) for a, b in dot_pairs]
    o_ref[...] = epilogue(accs, extra_refs).astype(o_ref.dtype)


def _matmul(name, xs, ws, extras, epilogue, *, dot_pairs, n_out, out_dtype, bm, bn):
    t = xs[0][0].shape[0]
    in_specs, operands = [], []
    for arr, k, cb in xs:
        in_specs.append(pl.BlockSpec((bm, k), lambda i, j, cb=cb: (i, cb)))
        operands.append(arr)
    for arr, cb0 in ws:
        k = arr.shape[0]
        in_specs.append(pl.BlockSpec((k, bn), lambda i, j, cb0=cb0: (0, cb0 + j)))
        operands.append(arr)
    for kind, arr, cb0 in extras:
        if kind == "tile":
            in_specs.append(pl.BlockSpec((bm, bn), lambda i, j, cb0=cb0: (i, cb0 + j)))
        else:
            in_specs.append(pl.BlockSpec((1, bn), lambda i, j, cb0=cb0: (0, cb0 + j)))
        operands.append(arr)
    return pl.pallas_call(
        functools.partial(_mm_kernel, n_x=len(xs), n_w=len(ws),
                          dot_pairs=dot_pairs, epilogue=epilogue),
        out_shape=jax.ShapeDtypeStruct((t, n_out), out_dtype),
        grid=(t // bm, n_out // bn),
        in_specs=in_specs,
        out_specs=pl.BlockSpec((bm, bn), lambda i, j: (i, j)),
        compiler_params=_params("parallel", "parallel"),
        name=name,
    )(*operands)


def _gelu_exact(x):
    return 0.5 * x * (1.0 + lax.erf(x * math.sqrt(0.5)))


def _ep_gelu(accs, extras):
    return _gelu_exact(accs[0])


def _ep_colscale(accs, extras):
    return accs[0] * extras[0][...]


def _ep_sigmoid_bias(accs, extras):
    return jax.nn.sigmoid(accs[0] + extras[0][...])


def _ep_gated_merge(accs, extras):
    return (extras[0][...].astype(F32) * accs[0]
            + extras[1][...].astype(F32) * accs[1])


def _ep_residual(accs, extras):
    return extras[0][...] + accs[0]


def _ep_swiglu(accs, extras):
    return jax.nn.silu(accs[0]) * accs[1]


def _ep_ple(accs, extras):
    return extras[0][...] + jax.nn.sigmoid(accs[0]) * accs[1]


def _mm_ksplit_kernel(x_ref, w_ref, r_ref, o_ref, acc_ref):
    k = pl.program_id(2)
    part = jnp.dot(x_ref[...], w_ref[...], preferred_element_type=F32)

    @pl.when(k == 0)
    def _():
        acc_ref[...] = part

    @pl.when(k != 0)
    def _():
        acc_ref[...] += part

    @pl.when(k == pl.num_programs(2) - 1)
    def _():
        o_ref[...] = r_ref[...] + acc_ref[...]


def _matmul_ksplit_residual(x, w, resid, *, bm, bn, bk):
    t, kdim = x.shape
    n = w.shape[1]
    return pl.pallas_call(
        _mm_ksplit_kernel,
        out_shape=jax.ShapeDtypeStruct((t, n), F32),
        grid=(t // bm, n // bn, kdim // bk),
        in_specs=[pl.BlockSpec((bm, bk), lambda i, j, k: (i, k)),
                  pl.BlockSpec((bk, bn), lambda i, j, k: (k, j)),
                  pl.BlockSpec((bm, bn), lambda i, j, k: (i, j))],
        out_specs=pl.BlockSpec((bm, bn), lambda i, j, k: (i, j)),
        scratch_shapes=[pltpu.VMEM((bm, bn), F32)],
        compiler_params=_params("parallel", "parallel", "arbitrary"),
        name="ffn_down",
    )(x, w, resid)


def _gmlp_kernel(gu_ref, gv_ref, lng_ref, lnb_ref, ws_ref, bs_ref, o_ref, wm_ref,
                 *, rows, n_groups, group_dim):
    @pl.when(pl.program_id(0) == 0)
    def _():
        r = lax.broadcasted_iota(jnp.int32, (CHUNK, CHUNK), 0)
        c = lax.broadcasted_iota(jnp.int32, (CHUNK, CHUNK), 1)
        for g in range(n_groups):
            wm_ref[g] = jnp.where(c <= r, ws_ref[g], 0.0).astype(BF16)

    for ch in range(rows // CHUNK):
        rs = slice(ch * CHUNK, (ch + 1) * CHUNK)
        v = gv_ref[rs, :].astype(F32)
        mu = jnp.mean(v, axis=-1, keepdims=True)
        vc = v - mu
        var = jnp.mean(vc * vc, axis=-1, keepdims=True)
        vn = (vc * lax.rsqrt(var + LN_EPS) * lng_ref[...] + lnb_ref[...]).astype(BF16)
        for g in range(n_groups):
            cs = slice(g * group_dim, (g + 1) * group_dim)
            mixed = jnp.dot(wm_ref[g], vn[:, cs], preferred_element_type=F32) + bs_ref[g]
            o_ref[rs, cs] = (gu_ref[rs, cs].astype(F32) * mixed).astype(o_ref.dtype)


def _gmlp(guv, ln_g, ln_b, w_s, b_s, *, rows=256):
    t = guv.shape[0]
    d = guv.shape[1] // 2
    n_groups = w_s.shape[0]
    return pl.pallas_call(
        functools.partial(_gmlp_kernel, rows=rows, n_groups=n_groups, group_dim=d // n_groups),
        out_shape=jax.ShapeDtypeStruct((t, d), BF16),
        grid=(t // rows,),
        in_specs=[pl.BlockSpec((rows, d), lambda i: (i, 0)),
                  pl.BlockSpec((rows, d), lambda i: (i, 1)),
                  pl.BlockSpec((1, d), lambda i: (0, 0)),
                  pl.BlockSpec((1, d), lambda i: (0, 0)),
                  pl.BlockSpec((n_groups, CHUNK, CHUNK), lambda i: (0, 0, 0)),
                  pl.BlockSpec((n_groups, CHUNK, 1), lambda i: (0, 0, 0))],
        out_specs=pl.BlockSpec((rows, d), lambda i: (i, 0)),
        scratch_shapes=[pltpu.VMEM((n_groups, CHUNK, CHUNK), BF16)],
        compiler_params=_params("arbitrary"),
        name="gmlp_gating",
    )(guv, guv, ln_g.reshape(1, d), ln_b.reshape(1, d), w_s, b_s.reshape(n_groups, CHUNK, 1))


def _lanes(x, n):
    return jnp.concatenate([x] * (n // LANES), axis=1)


def _attn_kernel(q_ref, k_ref, v_ref, lq1_ref, lk1_ref, lq2_ref, lk2_ref, sg_ref, o_ref,
                 s_ref, p_ref, m_ref, l_ref, a_ref, acc_ref, *, blk, strip, n_heads, lam_init):
    h = pl.program_id(1)
    i = pl.program_id(2)
    slope = jnp.exp2(-8.0 * jnp.full((1, blk), h + 1, F32) / n_heads) * LOG2_E
    k_idx = lax.broadcasted_iota(jnp.int32, (1, blk), 1).astype(F32)

    m_ref[...] = jnp.full(m_ref.shape, MASK_VALUE, F32)
    l_ref[...] = jnp.zeros(l_ref.shape, F32)
    acc_ref[...] = jnp.zeros(acc_ref.shape, F32)
    p_ref[1] = jnp.zeros(p_ref.shape[1:], BF16)
    a_ref[1] = jnp.ones(a_ref.shape[1:], F32)

    def qk(mp, j):
        off = pl.multiple_of(j * blk, blk)
        cs = slice(mp * HEAD_DIM, (mp + 1) * HEAD_DIM)
        s_ref[mp] = lax.dot_general(q_ref[:, cs], k_ref[pl.ds(off, blk), cs],
                                    (((1,), (1,)), ((), ())),
                                    preferred_element_type=F32)

    def pv(mp, j):
        off = pl.multiple_of(j * blk, blk)
        acc_ref[mp] = (acc_ref[mp] * _lanes(a_ref[mp], V_HEAD_DIM)
                       + jnp.dot(p_ref[mp], v_ref[pl.ds(off, blk), :],
                                 preferred_element_type=F32))

    def softmax_strips(mp, bias, masked):
        for r in range(blk // strip):
            rows = slice(r * strip, (r + 1) * strip)
            s = s_ref[mp, rows, :] + bias
            if masked:
                row = r * strip + lax.broadcasted_iota(jnp.int32, (strip, blk), 0)
                col = lax.broadcasted_iota(jnp.int32, (strip, blk), 1)
                s = jnp.where(col <= row, s, MASK_VALUE)
            m_old = m_ref[mp, rows, :]
            m_new = jnp.maximum(m_old, jnp.max(s, axis=-1, keepdims=True))
            alpha = jnp.exp2(m_old - m_new)
            p = jnp.exp2(s - _lanes(m_new, blk))
            p_sum = p[:, :LANES]
            for c in range(1, blk // LANES):
                p_sum = p_sum + p[:, c * LANES:(c + 1) * LANES]
            l_ref[mp, rows, :] = alpha * l_ref[mp, rows, :] + p_sum
            m_ref[mp, rows, :] = m_new
            a_ref[mp, rows, :] = alpha
            p_ref[mp, rows, :] = p.astype(BF16)

    def step(j, masked):
        bias = slope * (k_idx + ((j - i) * blk).astype(F32))
        pv(1, jnp.maximum(j - 1, 0))
        qk(1, j)
        softmax_strips(0, bias, masked)
        pv(0, j)
        if not masked:
            qk(0, j + 1)
        softmax_strips(1, bias, masked)

    def body(j, carry):
        step(j, masked=False)
        return carry

    qk(0, 0)
    lax.fori_loop(0, i, body, 0)
    step(i, masked=True)
    pv(1, i)

    lam = (jnp.exp(jnp.sum(lq1_ref[...] * lk1_ref[...], axis=-1, keepdims=True))
           - jnp.exp(jnp.sum(lq2_ref[...] * lk2_ref[...], axis=-1, keepdims=True))
           + lam_init)
    l1 = jnp.sum(l_ref[0], axis=-1, keepdims=True)
    l2 = jnp.sum(l_ref[1], axis=-1, keepdims=True)
    o = acc_ref[0] / l1 - lam * (acc_ref[1] / l2)
    ms = jnp.mean(o * o, axis=-1, keepdims=True)
    y = (o * lax.rsqrt(ms + LN_EPS) * sg_ref[...]) * (1.0 - lam_init)
    o_ref[...] = y.astype(o_ref.dtype)


def _diff_attention(qkv, lq1, lk1, lq2, lk2, subln_g, *, batch, seq, n_heads, lam_init,
                    blk=512, strip=32):
    t = qkv.shape[0]
    nq = seq // blk
    vec = lambda a: a.reshape(1, -1)
    vec_spec = lambda n: pl.BlockSpec((1, n), lambda b, h, i: (0, 0))
    return pl.pallas_call(
        functools.partial(_attn_kernel, blk=blk, strip=strip, n_heads=n_heads,
                          lam_init=lam_init),
        out_shape=jax.ShapeDtypeStruct((t, n_heads * V_HEAD_DIM), BF16),
        grid=(batch, n_heads, nq),
        in_specs=[pl.BlockSpec((blk, 2 * HEAD_DIM), lambda b, h, i: (b * nq + i, h)),
                  pl.BlockSpec((seq, 2 * HEAD_DIM), lambda b, h, i: (b, n_heads + h)),
                  pl.BlockSpec((seq, V_HEAD_DIM), lambda b, h, i: (b, 2 * n_heads + h)),
                  vec_spec(HEAD_DIM), vec_spec(HEAD_DIM), vec_spec(HEAD_DIM), vec_spec(HEAD_DIM),
                  vec_spec(V_HEAD_DIM)],
        out_specs=pl.BlockSpec((blk, V_HEAD_DIM), lambda b, h, i: (b * nq + i, h)),
        scratch_shapes=[pltpu.VMEM((2, blk, blk), F32),
                        pltpu.VMEM((2, blk, blk), BF16),
                        pltpu.VMEM((2, blk, LANES), F32),
                        pltpu.VMEM((2, blk, LANES), F32),
                        pltpu.VMEM((2, blk, LANES), F32),
                        pltpu.VMEM((2, blk, V_HEAD_DIM), F32)],
        compiler_params=_params("parallel", "parallel", "arbitrary"),
        name="diff_attention",
    )(qkv, qkv, qkv, vec(lq1), vec(lk1), vec(lq2), vec(lk2), vec(subln_g))


def kernel(x, p, g_mix, w_in, b_gate, ln_v_g, ln_v_b, w_s, b_s, lambda_q1, lambda_k1, lambda_q2, lambda_k2, subln_g, w_br_a, w_br_b, w_o, g_ffn, w_gu, w_down, g_ple, w_ple_gate, w_ple_proj, g_final):
    batch, seq, d_model = x.shape
    depth = w_in.shape[0]
    t = batch * seq
    d_gmlp = ln_v_g.shape[1]
    d_attn = w_br_b.shape[1]
    n_heads = d_attn // V_HEAD_DIM
    d_qk = n_heads * 2 * HEAD_DIM
    d_ff = w_down.shape[1]
    ple_dim = p.shape[-1]

    xf = x.reshape(t, d_model)
    for layer in range(depth):
        lam_init = 0.8 - 0.6 * math.exp(-0.3 * layer)
        w_in_b = w_in[layer].astype(BF16)

        h = _rmsnorm(xf, g_mix[layer], BF16)
        hx = [(h, d_model, 0)]
        bn = 1024
        guv = _matmul("in_proj_gelu", hx, [(w_in_b, 0)], [], _ep_gelu, dot_pairs=((0, 0),),
                      n_out=2 * d_gmlp, out_dtype=BF16, bm=1024, bn=bn)
        qkv_scale = jnp.concatenate([jnp.full((d_qk,), HEAD_DIM ** -0.5 * LOG2_E, F32),
                                     jnp.ones((d_qk + d_attn,), F32)]).reshape(1, -1)
        qkv = _matmul("in_proj_qkv", hx, [(w_in_b, 2 * d_gmlp // bn)], [("row", qkv_scale, 0)], _ep_colscale,
                      dot_pairs=((0, 0),), n_out=2 * d_qk + d_attn, out_dtype=BF16, bm=1024, bn=bn)
        gates = _matmul("in_proj_gates", hx, [(w_in_b, (2 * d_gmlp + 2 * d_qk + d_attn) // bn)],
                        [("row", b_gate[layer].reshape(1, -1), 0)], _ep_sigmoid_bias,
                        dot_pairs=((0, 0),), n_out=2 * d_model, out_dtype=BF16, bm=1024, bn=bn)

        y_a = _gmlp(guv, ln_v_g[layer], ln_v_b[layer], w_s[layer], b_s[layer])
        y_b = _diff_attention(qkv, lambda_q1[layer], lambda_k1[layer], lambda_q2[layer],
                              lambda_k2[layer], subln_g[layer], batch=batch, seq=seq,
                              n_heads=n_heads, lam_init=lam_init)

        bn = 512
        merged = _matmul("gated_merge", [(y_a, d_gmlp, 0), (y_b, d_attn, 0)],
                         [(w_br_a[layer].astype(BF16), 0), (w_br_b[layer].astype(BF16), 0)],
                         [("tile", gates, 0), ("tile", gates, d_model // bn)], _ep_gated_merge,
                         dot_pairs=((0, 0), (1, 1)), n_out=d_model, out_dtype=BF16, bm=512, bn=bn)
        xf = _matmul("out_proj", [(merged, d_model, 0)], [(w_o[layer].astype(BF16), 0)], [("tile", xf, 0)],
                     _ep_residual, dot_pairs=((0, 0),), n_out=d_model, out_dtype=F32,
                     bm=1024, bn=512)

        h = _rmsnorm(xf, g_ffn[layer], BF16)
        bn = 256
        w_gu_b = w_gu[layer].astype(BF16)
        act = _matmul("ffn_up", [(h, d_model, 0)], [(w_gu_b, 0), (w_gu_b, d_ff // bn)], [], _ep_swiglu,
                      dot_pairs=((0, 0), (0, 1)), n_out=d_ff, out_dtype=BF16, bm=1024, bn=bn)
        xf = _matmul_ksplit_residual(act, w_down[layer].astype(BF16), xf,
                                     bm=1024, bn=512, bk=d_ff // 2)

        h = _rmsnorm(xf, g_ple[layer], BF16)
        xf = _matmul("ple", [(h, d_model, 0), (p[layer].reshape(t, ple_dim), ple_dim, 0)],
                     [(w_ple_gate[layer].astype(BF16), 0), (w_ple_proj[layer].astype(BF16), 0)],
                     [("tile", xf, 0)], _ep_ple, dot_pairs=((0, 0), (1, 1)),
                     n_out=d_model, out_dtype=F32, bm=1024, bn=512)

    return _rmsnorm(xf, g_final, F32).reshape(batch, seq, d_model)
```

```python
import functools
import math

import jax
import jax.numpy as jnp
from jax import lax
from jax.experimental import pallas as pl
from jax.experimental.pallas import tpu as pltpu

F32 = jnp.float32
BF16 = jnp.bfloat16

V7X_VMEM_LIMIT_BYTES = 56 * 1024 * 1024

LANES = 128
LOG2_E = math.log2(math.e)

HEAD_DIM = 128
V_HEAD_DIM = 2 * HEAD_DIM
CHUNK = 128
RMS_EPS = 1e-6
LN_EPS = 1e-5
MASK_VALUE = -1e30


def _params(*semantics):
    return pltpu.CompilerParams(dimension_semantics=semantics,
                                vmem_limit_bytes=V7X_VMEM_LIMIT_BYTES)


def _rmsnorm_kernel(x_ref, g_ref, o_ref, *, eps):
    x = x_ref[...]
    ms = jnp.mean(x * x, axis=-1, keepdims=True)
    o_ref[...] = (x * lax.rsqrt(ms + eps) * g_ref[...]).astype(o_ref.dtype)


def _rmsnorm(x, g, out_dtype, *, rows=256):
    t, d = x.shape
    return pl.pallas_call(
        functools.partial(_rmsnorm_kernel, eps=RMS_EPS),
        out_shape=jax.ShapeDtypeStruct((t, d), out_dtype),
        grid=(t // rows,),
        in_specs=[pl.BlockSpec((rows, d), lambda i: (i, 0)),
                  pl.BlockSpec((1, d), lambda i: (0, 0))],
        out_specs=pl.BlockSpec((rows, d), lambda i: (i, 0)),
        compiler_params=_params("parallel"),
        name="rmsnorm",
    )(x, g.reshape(1, d))


def _mm_kernel(*refs, n_x, n_w, dot_pairs, epilogue):
    x_refs = refs[:n_x]
    w_refs = refs[n_x:n_x + n_w]
    extra_refs = refs[n_x + n_w:-1]
    o_ref = refs[-1]
    accs = [jnp.dot(x_refs[a][...].astype(BF16), w_refs[b][...],
                    preferred_element_type=F32) for a, b in dot_pairs]
    o_ref[...] = epilogue(accs, extra_refs).astype(o_ref.dtype)


def _matmul(name, xs, ws, extras, epilogue, *, dot_pairs, n_out, out_dtype, bm, bn):
    t = xs[0][0].shape[0]
    in_specs, operands = [], []
    for arr, k, cb in xs:
        in_specs.append(pl.BlockSpec((bm, k), lambda i, j, cb=cb: (i, cb)))
        operands.append(arr)
    for arr, cb0 in ws:
        k = arr.shape[0]
        in_specs.append(pl.BlockSpec((k, bn), lambda i, j, cb0=cb0: (0, cb0 + j)))
        operands.append(arr)
    for kind, arr, cb0 in extras:
        if kind == "tile":
            in_specs.append(pl.BlockSpec((bm, bn), lambda i, j, cb0=cb0: (i, cb0 + j)))
        else:
            in_specs.append(pl.BlockSpec((1, bn), lambda i, j, cb0=cb0: (0, cb0 + j)))
        operands.append(arr)
    return pl.pallas_call(
        functools.partial(_mm_kernel, n_x=len(xs), n_w=len(ws),
                          dot_pairs=dot_pairs, epilogue=epilogue),
        out_shape=jax.ShapeDtypeStruct((t, n_out), out_dtype),
        grid=(t // bm, n_out // bn),
        in_specs=in_specs,
        out_specs=pl.BlockSpec((bm, bn), lambda i, j: (i, j)),
        compiler_params=_params("parallel", "parallel"),
        name=name,
    )(*operands)


def _gelu_exact(x):
    return 0.5 * x * (1.0 + lax.erf(x * math.sqrt(0.5)))


def _ep_gelu(accs, extras):
    return _gelu_exact(accs[0])


def _ep_colscale(accs, extras):
    return accs[0] * extras[0][...]


def _ep_sigmoid_bias(accs, extras):
    return jax.nn.sigmoid(accs[0] + extras[0][...])


def _ep_gated(accs, extras):
    return extras[0][...].astype(F32) * accs[0]


def _ep_gated_add(accs, extras):
    return extras[1][...] + extras[0][...].astype(F32) * accs[0]


def _ep_residual(accs, extras):
    return extras[0][...] + accs[0]


def _ep_swiglu(accs, extras):
    return jax.nn.silu(accs[0]) * accs[1]


def _ep_ple(accs, extras):
    return extras[0][...] + jax.nn.sigmoid(accs[0]) * accs[1]


def _gmlp_kernel(gu_ref, gv_ref, lng_ref, lnb_ref, ws_ref, bs_ref, o_ref, wm_ref,
                 *, rows, n_groups, group_dim):
    @pl.when(pl.program_id(0) == 0)
    def _():
        r = lax.broadcasted_iota(jnp.int32, (CHUNK, CHUNK), 0)
        c = lax.broadcasted_iota(jnp.int32, (CHUNK, CHUNK), 1)
        for g in range(n_groups):
            wm_ref[g] = jnp.where(c <= r, ws_ref[g], 0.0).astype(BF16)

    for ch in range(rows // CHUNK):
        rs = slice(ch * CHUNK, (ch + 1) * CHUNK)
        v = gv_ref[rs, :].astype(F32)
        mu = jnp.mean(v, axis=-1, keepdims=True)
        vc = v - mu
        var = jnp.mean(vc * vc, axis=-1, keepdims=True)
        vn = (vc * lax.rsqrt(var + LN_EPS) * lng_ref[...] + lnb_ref[...]).astype(BF16)
        for g in range(n_groups):
            cs = slice(g * group_dim, (g + 1) * group_dim)
            mixed = jnp.dot(wm_ref[g], vn[:, cs], preferred_element_type=F32) + bs_ref[g]
            o_ref[rs, cs] = (gu_ref[rs, cs].astype(F32) * mixed).astype(o_ref.dtype)


def _gmlp(guv, ln_g, ln_b, w_s, b_s, *, rows=256):
    t = guv.shape[0]
    d = guv.shape[1] // 2
    n_groups = w_s.shape[0]
    return pl.pallas_call(
        functools.partial(_gmlp_kernel, rows=rows, n_groups=n_groups, group_dim=d // n_groups),
        out_shape=jax.ShapeDtypeStruct((t, d), BF16),
        grid=(t // rows,),
        in_specs=[pl.BlockSpec((rows, d), lambda i: (i, 0)),
                  pl.BlockSpec((rows, d), lambda i: (i, 1)),
                  pl.BlockSpec((1, d), lambda i: (0, 0)),
                  pl.BlockSpec((1, d), lambda i: (0, 0)),
                  pl.BlockSpec((n_groups, CHUNK, CHUNK), lambda i: (0, 0, 0)),
                  pl.BlockSpec((n_groups, CHUNK, 1), lambda i: (0, 0, 0))],
        out_specs=pl.BlockSpec((rows, d), lambda i: (i, 0)),
        scratch_shapes=[pltpu.VMEM((n_groups, CHUNK, CHUNK), BF16)],
        compiler_params=_params("arbitrary"),
        name="gmlp_gating",
    )(guv, guv, ln_g.reshape(1, d), ln_b.reshape(1, d), w_s, b_s.reshape(n_groups, CHUNK, 1))


def _lanes(x, n):
    return jnp.concatenate([x] * (n // LANES), axis=1)


def _attn_kernel(q_ref, k_ref, v_ref, lq1_ref, lk1_ref, lq2_ref, lk2_ref, sg_ref, o_ref,
                 s_ref, p_ref, m_ref, l_ref, a_ref, acc_ref, *, blk, strip, n_heads, lam_init):
    h = pl.program_id(1)
    i = pl.program_id(2)
    slope = jnp.exp2(-8.0 * jnp.full((1, blk), h + 1, F32) / n_heads) * LOG2_E
    k_idx = lax.broadcasted_iota(jnp.int32, (1, blk), 1).astype(F32)

    m_ref[...] = jnp.full(m_ref.shape, MASK_VALUE, F32)
    l_ref[...] = jnp.zeros(l_ref.shape, F32)
    acc_ref[...] = jnp.zeros(acc_ref.shape, F32)
    p_ref[1] = jnp.zeros(p_ref.shape[1:], BF16)
    a_ref[1] = jnp.ones(a_ref.shape[1:], F32)

    def qk(mp, j):
        off = pl.multiple_of(j * blk, blk)
        cs = slice(mp * HEAD_DIM, (mp + 1) * HEAD_DIM)
        s_ref[mp] = lax.dot_general(q_ref[:, cs], k_ref[pl.ds(off, blk), cs],
                                    (((1,), (1,)), ((), ())),
                                    preferred_element_type=F32)

    def pv(mp, j):
        off = pl.multiple_of(j * blk, blk)
        acc_ref[mp] = (acc_ref[mp] * _lanes(a_ref[mp], V_HEAD_DIM)
                       + jnp.dot(p_ref[mp], v_ref[pl.ds(off, blk), :],
                                 preferred_element_type=F32))

    def softmax_strips(mp, bias, masked):
        for r in range(blk // strip):
            rows = slice(r * strip, (r + 1) * strip)
            s = s_ref[mp, rows, :] + bias
            if masked:
                row = r * strip + lax.broadcasted_iota(jnp.int32, (strip, blk), 0)
                col = lax.broadcasted_iota(jnp.int32, (strip, blk), 1)
                s = jnp.where(col <= row, s, MASK_VALUE)
            m_old = m_ref[mp, rows, :]
            m_new = jnp.maximum(m_old, jnp.max(s, axis=-1, keepdims=True))
            alpha = jnp.exp2(m_old - m_new)
            p = jnp.exp2(s - _lanes(m_new, blk))
            p_sum = p[:, :LANES]
            for c in range(1, blk // LANES):
                p_sum = p_sum + p[:, c * LANES:(c + 1) * LANES]
            l_ref[mp, rows, :] = alpha * l_ref[mp, rows, :] + p_sum
            m_ref[mp, rows, :] = m_new
            a_ref[mp, rows, :] = alpha
            p_ref[mp, rows, :] = p.astype(BF16)

    def step(j, masked):
        bias = slope * (k_idx + ((j - i) * blk).astype(F32))
        pv(1, jnp.maximum(j - 1, 0))
        qk(1, j)
        softmax_strips(0, bias, masked)
        pv(0, j)
        if not masked:
            qk(0, j + 1)
        softmax_strips(1, bias, masked)

    def two_steps(t, carry):
        step(2 * t, masked=False)
        step(2 * t + 1, masked=False)
        return carry

    qk(0, 0)
    lax.fori_loop(0, lax.shift_right_logical(i, 1), two_steps, 0)

    @pl.when(lax.bitwise_and(i, 1) == 1)
    def _():
        step(i - 1, masked=False)

    step(i, masked=True)
    pv(1, i)

    lam = (jnp.exp(jnp.sum(lq1_ref[...] * lk1_ref[...], axis=-1, keepdims=True))
           - jnp.exp(jnp.sum(lq2_ref[...] * lk2_ref[...], axis=-1, keepdims=True))
           + lam_init)
    l1 = jnp.sum(l_ref[0], axis=-1, keepdims=True)
    l2 = jnp.sum(l_ref[1], axis=-1, keepdims=True)
    o = acc_ref[0] / l1 - lam * (acc_ref[1] / l2)
    ms = jnp.mean(o * o, axis=-1, keepdims=True)
    y = (o * lax.rsqrt(ms + LN_EPS) * sg_ref[...]) * (1.0 - lam_init)
    o_ref[...] = y.astype(o_ref.dtype)


def _diff_attention(qkv, lq1, lk1, lq2, lk2, subln_g, *, batch, seq, n_heads, lam_init,
                    blk=512, strip=32):
    t = qkv.shape[0]
    nq = seq // blk
    vec = lambda a: a.reshape(1, -1)
    vec_spec = lambda n: pl.BlockSpec((1, n), lambda b, h, i: (0, 0))
    return pl.pallas_call(
        functools.partial(_attn_kernel, blk=blk, strip=strip, n_heads=n_heads,
                          lam_init=lam_init),
        out_shape=jax.ShapeDtypeStruct((t, n_heads * V_HEAD_DIM), BF16),
        grid=(batch, n_heads, nq),
        in_specs=[pl.BlockSpec((blk, 2 * HEAD_DIM), lambda b, h, i: (b * nq + i, h)),
                  pl.BlockSpec((seq, 2 * HEAD_DIM), lambda b, h, i: (b, n_heads + h)),
                  pl.BlockSpec((seq, V_HEAD_DIM), lambda b, h, i: (b, 2 * n_heads + h)),
                  vec_spec(HEAD_DIM), vec_spec(HEAD_DIM), vec_spec(HEAD_DIM), vec_spec(HEAD_DIM),
                  vec_spec(V_HEAD_DIM)],
        out_specs=pl.BlockSpec((blk, V_HEAD_DIM), lambda b, h, i: (b * nq + i, h)),
        scratch_shapes=[pltpu.VMEM((2, blk, blk), F32),
                        pltpu.VMEM((2, blk, blk), BF16),
                        pltpu.VMEM((2, blk, LANES), F32),
                        pltpu.VMEM((2, blk, LANES), F32),
                        pltpu.VMEM((2, blk, LANES), F32),
                        pltpu.VMEM((2, blk, V_HEAD_DIM), F32)],
        compiler_params=_params("parallel", "parallel", "arbitrary"),
        name="diff_attention",
    )(qkv, qkv, qkv, vec(lq1), vec(lk1), vec(lq2), vec(lk2), vec(subln_g))


def kernel(x, p, g_mix, w_in, b_gate, ln_v_g, ln_v_b, w_s, b_s, lambda_q1, lambda_k1, lambda_q2, lambda_k2, subln_g, w_br_a, w_br_b, w_o, g_ffn, w_gu, w_down, g_ple, w_ple_gate, w_ple_proj, g_final):
    batch, seq, d_model = x.shape
    depth = w_in.shape[0]
    t = batch * seq
    d_gmlp = ln_v_g.shape[1]
    d_attn = w_br_b.shape[1]
    n_heads = d_attn // V_HEAD_DIM
    d_qk = n_heads * 2 * HEAD_DIM
    d_ff = w_down.shape[1]
    ple_dim = p.shape[-1]

    xf = x.reshape(t, d_model)
    for layer in range(depth):
        lam_init = 0.8 - 0.6 * math.exp(-0.3 * layer)
        w_in_b = w_in[layer].astype(BF16)

        h = _rmsnorm(xf, g_mix[layer], BF16)
        hx = [(h, d_model, 0)]
        bn = 1024
        guv = _matmul("in_proj_gelu", hx, [(w_in_b, 0)], [], _ep_gelu, dot_pairs=((0, 0),),
                      n_out=2 * d_gmlp, out_dtype=BF16, bm=1024, bn=bn)
        qkv_scale = jnp.concatenate([jnp.full((d_qk,), HEAD_DIM ** -0.5 * LOG2_E, F32),
                                     jnp.ones((d_qk + d_attn,), F32)]).reshape(1, -1)
        qkv = _matmul("in_proj_qkv", hx, [(w_in_b, 2 * d_gmlp // bn)], [("row", qkv_scale, 0)], _ep_colscale,
                      dot_pairs=((0, 0),), n_out=2 * d_qk + d_attn, out_dtype=BF16, bm=1024, bn=bn)
        gates = _matmul("in_proj_gates", hx, [(w_in_b, (2 * d_gmlp + 2 * d_qk + d_attn) // bn)],
                        [("row", b_gate[layer].reshape(1, -1), 0)], _ep_sigmoid_bias,
                        dot_pairs=((0, 0),), n_out=2 * d_model, out_dtype=BF16, bm=1024, bn=bn)

        y_a = _gmlp(guv, ln_v_g[layer], ln_v_b[layer], w_s[layer], b_s[layer])
        y_b = _diff_attention(qkv, lambda_q1[layer], lambda_k1[layer], lambda_q2[layer],
                              lambda_k2[layer], subln_g[layer], batch=batch, seq=seq,
                              n_heads=n_heads, lam_init=lam_init)

        bn = 1024
        merged_a = _matmul("merge_a", [(y_a, d_gmlp, 0)], [(w_br_a[layer].astype(BF16), 0)],
                           [("tile", gates, 0)], _ep_gated, dot_pairs=((0, 0),),
                           n_out=d_model, out_dtype=F32, bm=1024, bn=bn)
        merged = _matmul("merge_b", [(y_b, d_attn, 0)], [(w_br_b[layer].astype(BF16), 0)],
                         [("tile", gates, d_model // bn), ("tile", merged_a, 0)], _ep_gated_add,
                         dot_pairs=((0, 0),), n_out=d_model, out_dtype=BF16, bm=1024, bn=bn)
        xf = _matmul("out_proj", [(merged, d_model, 0)], [(w_o[layer].astype(BF16), 0)], [("tile", xf, 0)],
                     _ep_residual, dot_pairs=((0, 0),), n_out=d_model, out_dtype=F32,
                     bm=1024, bn=512)

        h = _rmsnorm(xf, g_ffn[layer], BF16)
        bn = 256
        w_gu_b = w_gu[layer].astype(BF16)
        act = _matmul("ffn_up", [(h, d_model, 0)], [(w_gu_b, 0), (w_gu_b, d_ff // bn)], [], _ep_swiglu,
                      dot_pairs=((0, 0), (0, 1)), n_out=d_ff, out_dtype=BF16, bm=1024, bn=bn)
        xf = _matmul("ffn_down", [(act, d_ff, 0)], [(w_down[layer].astype(BF16), 0)],
                     [("tile", xf, 0)], _ep_residual, dot_pairs=((0, 0),), n_out=d_model,
                     out_dtype=F32, bm=512, bn=512)

        h = _rmsnorm(xf, g_ple[layer], BF16)
        xf = _matmul("ple", [(h, d_model, 0), (p[layer].reshape(t, ple_dim), ple_dim, 0)],
                     [(w_ple_gate[layer].astype(BF16), 0), (w_ple_proj[layer].astype(BF16), 0)],
                     [("tile", xf, 0)], _ep_ple, dot_pairs=((0, 0), (1, 1)),
                     n_out=d_model, out_dtype=F32, bm=1024, bn=512)

    return _rmsnorm(xf, g_final, F32).reshape(batch, seq, d_model)
```

```python
import functools
import math

import jax
import jax.numpy as jnp
from jax import lax
from jax.experimental import pallas as pl
from jax.experimental.pallas import tpu as pltpu

F32 = jnp.float32
BF16 = jnp.bfloat16

V7X_VMEM_LIMIT_BYTES = 56 * 1024 * 1024

LANES = 128
LOG2_E = math.log2(math.e)

HEAD_DIM = 128
V_HEAD_DIM = 2 * HEAD_DIM
CHUNK = 128
RMS_EPS = 1e-6
LN_EPS = 1e-5
MASK_VALUE = -1e30


def _params(*semantics):
    return pltpu.CompilerParams(dimension_semantics=semantics,
                                vmem_limit_bytes=V7X_VMEM_LIMIT_BYTES)


def _rmsnorm_kernel(x_ref, g_ref, o_ref, *, eps):
    x = x_ref[...]
    ms = jnp.mean(x * x, axis=-1, keepdims=True)
    o_ref[...] = (x * lax.rsqrt(ms + eps) * g_ref[...]).astype(o_ref.dtype)


def _rmsnorm(x, g, out_dtype, *, rows=256):
    t, d = x.shape
    return pl.pallas_call(
        functools.partial(_rmsnorm_kernel, eps=RMS_EPS),
        out_shape=jax.ShapeDtypeStruct((t, d), out_dtype),
        grid=(t // rows,),
        in_specs=[pl.BlockSpec((rows, d), lambda i: (i, 0)),
                  pl.BlockSpec((1, d), lambda i: (0, 0))],
        out_specs=pl.BlockSpec((rows, d), lambda i: (i, 0)),
        compiler_params=_params("parallel"),
        name="rmsnorm",
    )(x, g.reshape(1, d))


def _mm_kernel(*refs, n_x, n_w, n_extra, dot_pairs, epilogue, emit_norm):
    x_refs = refs[:n_x]
    w_refs = refs[n_x:n_x + n_w]
    extra_refs = refs[n_x + n_w:n_x + n_w + n_extra]
    out_refs = refs[n_x + n_w + n_extra:]
    accs = [jnp.dot(x_refs[a][...].astype(BF16), w_refs[b][...],
                    preferred_element_type=F32) for a, b in dot_pairs]
    y = epilogue(accs, extra_refs)
    out_refs[0][...] = y.astype(out_refs[0].dtype)
    if emit_norm:
        gain_ref = extra_refs[-1]
        xg_ref, ssq_ref = out_refs[1:]
        xg_ref[...] = (y * gain_ref[...]).astype(xg_ref.dtype)
        part = jnp.broadcast_to(jnp.sum(y * y, axis=-1, keepdims=True), ssq_ref.shape)
        j = pl.program_id(1)

        @pl.when(j == 0)
        def _():
            ssq_ref[...] = part

        @pl.when(j != 0)
        def _():
            ssq_ref[...] += part


def _matmul(name, xs, ws, extras, epilogue, *, dot_pairs, n_out, out_dtype, bm, bn,
            norm_gain=None):
    t = xs[0][0].shape[0]
    emit_norm = norm_gain is not None
    if emit_norm:
        extras = list(extras) + [("row", norm_gain, 0)]
    in_specs, operands = [], []
    for arr, k, cb in xs:
        in_specs.append(pl.BlockSpec((bm, k), lambda i, j, cb=cb: (i, cb)))
        operands.append(arr)
    for arr, cb0 in ws:
        k = arr.shape[0]
        in_specs.append(pl.BlockSpec((k, bn), lambda i, j, cb0=cb0: (0, cb0 + j)))
        operands.append(arr)
    for kind, arr, cb0 in extras:
        if kind == "tile":
            in_specs.append(pl.BlockSpec((bm, bn), lambda i, j, cb0=cb0: (i, cb0 + j)))
        elif kind == "row":
            in_specs.append(pl.BlockSpec((1, bn), lambda i, j, cb0=cb0: (0, cb0 + j)))
        else:
            in_specs.append(pl.BlockSpec((bm, LANES), lambda i, j: (i, 0)))
        operands.append(arr)
    tile_spec = pl.BlockSpec((bm, bn), lambda i, j: (i, j))
    out_shape = jax.ShapeDtypeStruct((t, n_out), out_dtype)
    out_specs = tile_spec
    if emit_norm:
        out_shape = (out_shape, jax.ShapeDtypeStruct((t, n_out), BF16),
                     jax.ShapeDtypeStruct((t, LANES), F32))
        out_specs = (tile_spec, tile_spec, pl.BlockSpec((bm, LANES), lambda i, j: (i, 0)))
    return pl.pallas_call(
        functools.partial(_mm_kernel, n_x=len(xs), n_w=len(ws), n_extra=len(extras),
                          dot_pairs=dot_pairs, epilogue=epilogue, emit_norm=emit_norm),
        out_shape=out_shape,
        grid=(t // bm, n_out // bn),
        in_specs=in_specs,
        out_specs=out_specs,
        compiler_params=_params("parallel", "arbitrary" if emit_norm else "parallel"),
        name=name,
    )(*operands)


def _rms_scale(ssq_ref, n):
    return lax.rsqrt(ssq_ref[:, :1] / n + RMS_EPS)


def _gelu_exact(x):
    return 0.5 * x * (1.0 + lax.erf(x * math.sqrt(0.5)))


def _ep_gelu(accs, extras):
    return _gelu_exact(accs[0])


def _ep_colscale(accs, extras):
    return accs[0] * extras[0][...]


def _ep_sigmoid_bias(accs, extras):
    return jax.nn.sigmoid(accs[0] + extras[0][...])


def _ep_gated(accs, extras):
    return extras[0][...].astype(F32) * accs[0]


def _ep_gated_add(accs, extras):
    return extras[1][...] + extras[0][...].astype(F32) * accs[0]


def _ep_residual(accs, extras):
    return extras[0][...] + accs[0]


def _ep_swiglu_normed(accs, extras, *, n):
    r = _rms_scale(extras[0], n)
    return jax.nn.silu(r * accs[0]) * (r * accs[1])


def _ep_ple_normed(accs, extras, *, n):
    r = _rms_scale(extras[1], n)
    return extras[0][...] + jax.nn.sigmoid(r * accs[0]) * accs[1]


def _gmlp_kernel(gu_ref, gv_ref, lng_ref, lnb_ref, ws_ref, bs_ref, o_ref, wm_ref,
                 *, rows, n_groups, group_dim):
    @pl.when(pl.program_id(0) == 0)
    def _():
        r = lax.broadcasted_iota(jnp.int32, (CHUNK, CHUNK), 0)
        c = lax.broadcasted_iota(jnp.int32, (CHUNK, CHUNK), 1)
        for g in range(n_groups):
            wm_ref[g] = jnp.where(c <= r, ws_ref[g], 0.0).astype(BF16)

    for ch in range(rows // CHUNK):
        rs = slice(ch * CHUNK, (ch + 1) * CHUNK)
        v = gv_ref[rs, :].astype(F32)
        mu = jnp.mean(v, axis=-1, keepdims=True)
        vc = v - mu
        var = jnp.mean(vc * vc, axis=-1, keepdims=True)
        vn = (vc * lax.rsqrt(var + LN_EPS) * lng_ref[...] + lnb_ref[...]).astype(BF16)
        for g in range(n_groups):
            cs = slice(g * group_dim, (g + 1) * group_dim)
            mixed = jnp.dot(wm_ref[g], vn[:, cs], preferred_element_type=F32) + bs_ref[g]
            o_ref[rs, cs] = (gu_ref[rs, cs].astype(F32) * mixed).astype(o_ref.dtype)


def _gmlp(guv, ln_g, ln_b, w_s, b_s, *, rows=256):
    t = guv.shape[0]
    d = guv.shape[1] // 2
    n_groups = w_s.shape[0]
    return pl.pallas_call(
        functools.partial(_gmlp_kernel, rows=rows, n_groups=n_groups, group_dim=d // n_groups),
        out_shape=jax.ShapeDtypeStruct((t, d), BF16),
        grid=(t // rows,),
        in_specs=[pl.BlockSpec((rows, d), lambda i: (i, 0)),
                  pl.BlockSpec((rows, d), lambda i: (i, 1)),
                  pl.BlockSpec((1, d), lambda i: (0, 0)),
                  pl.BlockSpec((1, d), lambda i: (0, 0)),
                  pl.BlockSpec((n_groups, CHUNK, CHUNK), lambda i: (0, 0, 0)),
                  pl.BlockSpec((n_groups, CHUNK, 1), lambda i: (0, 0, 0))],
        out_specs=pl.BlockSpec((rows, d), lambda i: (i, 0)),
        scratch_shapes=[pltpu.VMEM((n_groups, CHUNK, CHUNK), BF16)],
        compiler_params=_params("arbitrary"),
        name="gmlp_gating",
    )(guv, guv, ln_g.reshape(1, d), ln_b.reshape(1, d), w_s, b_s.reshape(n_groups, CHUNK, 1))


def _lanes(x, n):
    return jnp.concatenate([x] * (n // LANES), axis=1)


def _attn_kernel(q_ref, k_ref, v_ref, lq1_ref, lk1_ref, lq2_ref, lk2_ref, sg_ref, o_ref,
                 s_ref, p_ref, m_ref, l_ref, a_ref, acc_ref, *, blk, strip, n_heads, lam_init):
    h = pl.program_id(1)
    i = pl.program_id(2)
    slope = jnp.exp2(-8.0 * jnp.full((1, blk), h + 1, F32) / n_heads) * LOG2_E
    k_idx = lax.broadcasted_iota(jnp.int32, (1, blk), 1).astype(F32)

    m_ref[...] = jnp.full(m_ref.shape, MASK_VALUE, F32)
    l_ref[...] = jnp.zeros(l_ref.shape, F32)
    acc_ref[...] = jnp.zeros(acc_ref.shape, F32)
    p_ref[1] = jnp.zeros(p_ref.shape[1:], BF16)
    a_ref[1] = jnp.ones(a_ref.shape[1:], F32)

    def qk(mp, j):
        off = pl.multiple_of(j * blk, blk)
        cs = slice(mp * HEAD_DIM, (mp + 1) * HEAD_DIM)
        s_ref[mp] = lax.dot_general(q_ref[:, cs], k_ref[pl.ds(off, blk), cs],
                                    (((1,), (1,)), ((), ())),
                                    preferred_element_type=F32)

    def pv(mp, j):
        off = pl.multiple_of(j * blk, blk)
        acc_ref[mp] = (acc_ref[mp] * _lanes(a_ref[mp], V_HEAD_DIM)
                       + jnp.dot(p_ref[mp], v_ref[pl.ds(off, blk), :],
                                 preferred_element_type=F32))

    def softmax_strips(mp, bias, masked):
        for r in range(blk // strip):
            rows = slice(r * strip, (r + 1) * strip)
            s = s_ref[mp, rows, :] + bias
            if masked:
                row = r * strip + lax.broadcasted_iota(jnp.int32, (strip, blk), 0)
                col = lax.broadcasted_iota(jnp.int32, (strip, blk), 1)
                s = jnp.where(col <= row, s, MASK_VALUE)
            m_old = m_ref[mp, rows, :]
            m_new = jnp.maximum(m_old, jnp.max(s, axis=-1, keepdims=True))
            alpha = jnp.exp2(m_old - m_new)
            p = jnp.exp2(s - _lanes(m_new, blk))
            p_sum = p[:, :LANES]
            for c in range(1, blk // LANES):
                p_sum = p_sum + p[:, c * LANES:(c + 1) * LANES]
            l_ref[mp, rows, :] = alpha * l_ref[mp, rows, :] + p_sum
            m_ref[mp, rows, :] = m_new
            a_ref[mp, rows, :] = alpha
            p_ref[mp, rows, :] = p.astype(BF16)

    def step(j, masked):
        bias = slope * (k_idx + ((j - i) * blk).astype(F32))
        pv(1, jnp.maximum(j - 1, 0))
        qk(1, j)
        softmax_strips(0, bias, masked)
        pv(0, j)
        if not masked:
            qk(0, j + 1)
        softmax_strips(1, bias, masked)

    def two_steps(t, carry):
        step(2 * t, masked=False)
        step(2 * t + 1, masked=False)
        return carry

    qk(0, 0)
    lax.fori_loop(0, lax.shift_right_logical(i, 1), two_steps, 0)

    @pl.when(lax.bitwise_and(i, 1) == 1)
    def _():
        step(i - 1, masked=False)

    step(i, masked=True)
    pv(1, i)

    lam = (jnp.exp(jnp.sum(lq1_ref[...] * lk1_ref[...], axis=-1, keepdims=True))
           - jnp.exp(jnp.sum(lq2_ref[...] * lk2_ref[...], axis=-1, keepdims=True))
           + lam_init)
    l1 = jnp.sum(l_ref[0], axis=-1, keepdims=True)
    l2 = jnp.sum(l_ref[1], axis=-1, keepdims=True)
    o = acc_ref[0] / l1 - lam * (acc_ref[1] / l2)
    ms = jnp.mean(o * o, axis=-1, keepdims=True)
    y = (o * lax.rsqrt(ms + LN_EPS) * sg_ref[...]) * (1.0 - lam_init)
    o_ref[...] = y.astype(o_ref.dtype)


def _diff_attention(qkv, lq1, lk1, lq2, lk2, subln_g, *, batch, seq, n_heads, lam_init,
                    blk=512, strip=32):
    t = qkv.shape[0]
    nq = seq // blk
    vec = lambda a: a.reshape(1, -1)
    vec_spec = lambda n: pl.BlockSpec((1, n), lambda b, h, i: (0, 0))
    return pl.pallas_call(
        functools.partial(_attn_kernel, blk=blk, strip=strip, n_heads=n_heads,
                          lam_init=lam_init),
        out_shape=jax.ShapeDtypeStruct((t, n_heads * V_HEAD_DIM), BF16),
        grid=(batch, n_heads, nq),
        in_specs=[pl.BlockSpec((blk, 2 * HEAD_DIM), lambda b, h, i: (b * nq + i, h)),
                  pl.BlockSpec((seq, 2 * HEAD_DIM), lambda b, h, i: (b, n_heads + h)),
                  pl.BlockSpec((seq, V_HEAD_DIM), lambda b, h, i: (b, 2 * n_heads + h)),
                  vec_spec(HEAD_DIM), vec_spec(HEAD_DIM), vec_spec(HEAD_DIM), vec_spec(HEAD_DIM),
                  vec_spec(V_HEAD_DIM)],
        out_specs=pl.BlockSpec((blk, V_HEAD_DIM), lambda b, h, i: (b * nq + i, h)),
        scratch_shapes=[pltpu.VMEM((2, blk, blk), F32),
                        pltpu.VMEM((2, blk, blk), BF16),
                        pltpu.VMEM((2, blk, LANES), F32),
                        pltpu.VMEM((2, blk, LANES), F32),
                        pltpu.VMEM((2, blk, LANES), F32),
                        pltpu.VMEM((2, blk, V_HEAD_DIM), F32)],
        compiler_params=_params("parallel", "parallel", "arbitrary"),
        name="diff_attention",
    )(qkv, qkv, qkv, vec(lq1), vec(lk1), vec(lq2), vec(lk2), vec(subln_g))


def kernel(x, p, g_mix, w_in, b_gate, ln_v_g, ln_v_b, w_s, b_s, lambda_q1, lambda_k1, lambda_q2, lambda_k2, subln_g, w_br_a, w_br_b, w_o, g_ffn, w_gu, w_down, g_ple, w_ple_gate, w_ple_proj, g_final):
    batch, seq, d_model = x.shape
    depth = w_in.shape[0]
    t = batch * seq
    d_gmlp = ln_v_g.shape[1]
    d_attn = w_br_b.shape[1]
    n_heads = d_attn // V_HEAD_DIM
    d_qk = n_heads * 2 * HEAD_DIM
    d_ff = w_down.shape[1]
    ple_dim = p.shape[-1]

    xf = x.reshape(t, d_model)
    for layer in range(depth):
        lam_init = 0.8 - 0.6 * math.exp(-0.3 * layer)
        w_in_b = w_in[layer].astype(BF16)

        h = _rmsnorm(xf, g_mix[layer], BF16)
        hx = [(h, d_model, 0)]
        bn = 1024
        guv = _matmul("in_proj_gelu", hx, [(w_in_b, 0)], [], _ep_gelu, dot_pairs=((0, 0),),
                      n_out=2 * d_gmlp, out_dtype=BF16, bm=1024, bn=bn)
        qkv_scale = jnp.concatenate([jnp.full((d_qk,), HEAD_DIM ** -0.5 * LOG2_E, F32),
                                     jnp.ones((d_qk + d_attn,), F32)]).reshape(1, -1)
        qkv = _matmul("in_proj_qkv", hx, [(w_in_b, 2 * d_gmlp // bn)], [("row", qkv_scale, 0)], _ep_colscale,
                      dot_pairs=((0, 0),), n_out=2 * d_qk + d_attn, out_dtype=BF16, bm=1024, bn=bn)
        gates = _matmul("in_proj_gates", hx, [(w_in_b, (2 * d_gmlp + 2 * d_qk + d_attn) // bn)],
                        [("row", b_gate[layer].reshape(1, -1), 0)], _ep_sigmoid_bias,
                        dot_pairs=((0, 0),), n_out=2 * d_model, out_dtype=BF16, bm=1024, bn=bn)

        y_a = _gmlp(guv, ln_v_g[layer], ln_v_b[layer], w_s[layer], b_s[layer])
        y_b = _diff_attention(qkv, lambda_q1[layer], lambda_k1[layer], lambda_q2[layer],
                              lambda_k2[layer], subln_g[layer], batch=batch, seq=seq,
                              n_heads=n_heads, lam_init=lam_init)

        bn = 1024
        merged_a = _matmul("merge_a", [(y_a, d_gmlp, 0)], [(w_br_a[layer].astype(BF16), 0)],
                           [("tile", gates, 0)], _ep_gated, dot_pairs=((0, 0),),
                           n_out=d_model, out_dtype=F32, bm=1024, bn=bn)
        merged = _matmul("merge_b", [(y_b, d_attn, 0)], [(w_br_b[layer].astype(BF16), 0)],
                         [("tile", gates, d_model // bn), ("tile", merged_a, 0)], _ep_gated_add,
                         dot_pairs=((0, 0),), n_out=d_model, out_dtype=BF16, bm=1024, bn=bn)
        xf, xg, ssq = _matmul("out_proj", [(merged, d_model, 0)], [(w_o[layer].astype(BF16), 0)],
                              [("tile", xf, 0)], _ep_residual, dot_pairs=((0, 0),),
                              n_out=d_model, out_dtype=F32, bm=1024, bn=512,
                              norm_gain=g_ffn[layer].reshape(1, -1))

        bn = 256
        w_gu_b = w_gu[layer].astype(BF16)
        act = _matmul("ffn_up", [(xg, d_model, 0)], [(w_gu_b, 0), (w_gu_b, d_ff // bn)],
                      [("rowstat", ssq, 0)], functools.partial(_ep_swiglu_normed, n=d_model),
                      dot_pairs=((0, 0), (0, 1)), n_out=d_ff, out_dtype=BF16, bm=2048, bn=bn)
        xf, xg, ssq = _matmul("ffn_down", [(act, d_ff, 0)], [(w_down[layer].astype(BF16), 0)],
                              [("tile", xf, 0)], _ep_residual, dot_pairs=((0, 0),),
                              n_out=d_model, out_dtype=F32, bm=512, bn=512,
                              norm_gain=g_ple[layer].reshape(1, -1))

        xf = _matmul("ple", [(xg, d_model, 0), (p[layer].reshape(t, ple_dim), ple_dim, 0)],
                     [(w_ple_gate[layer].astype(BF16), 0), (w_ple_proj[layer].astype(BF16), 0)],
                     [("tile", xf, 0), ("rowstat", ssq, 0)],
                     functools.partial(_ep_ple_normed, n=d_model), dot_pairs=((0, 0), (1, 1)),
                     n_out=d_model, out_dtype=F32, bm=1024, bn=512)

    return _rmsnorm(xf, g_final, F32).reshape(batch, seq, d_model)
```

```python
import functools
import math

import jax
import jax.numpy as jnp
from jax import lax
from jax.experimental import pallas as pl
from jax.experimental.pallas import tpu as pltpu

F32 = jnp.float32
BF16 = jnp.bfloat16

V7X_VMEM_LIMIT_BYTES = 56 * 1024 * 1024

LANES = 128
LOG2_E = math.log2(math.e)

HEAD_DIM = 128
V_HEAD_DIM = 2 * HEAD_DIM
CHUNK = 128
RMS_EPS = 1e-6
LN_EPS = 1e-5
MASK_VALUE = -1e30


def _params(*semantics):
    return pltpu.CompilerParams(dimension_semantics=semantics,
                                vmem_limit_bytes=V7X_VMEM_LIMIT_BYTES)


def _rmsnorm_kernel(x_ref, g_ref, o_ref, *, eps):
    x = x_ref[...]
    ms = jnp.mean(x * x, axis=-1, keepdims=True)
    o_ref[...] = (x * lax.rsqrt(ms + eps) * g_ref[...]).astype(o_ref.dtype)


def _rmsnorm(x, g, out_dtype, *, rows=256):
    t, d = x.shape
    return pl.pallas_call(
        functools.partial(_rmsnorm_kernel, eps=RMS_EPS),
        out_shape=jax.ShapeDtypeStruct((t, d), out_dtype),
        grid=(t // rows,),
        in_specs=[pl.BlockSpec((rows, d), lambda i: (i, 0)),
                  pl.BlockSpec((1, d), lambda i: (0, 0))],
        out_specs=pl.BlockSpec((rows, d), lambda i: (i, 0)),
        compiler_params=_params("parallel"),
        name="rmsnorm",
    )(x, g.reshape(1, d))


def _mm_kernel(*refs, n_x, n_w, n_extra, dot_pairs, epilogue, emit_norm, row_norm, bn):
    x_refs = refs[:n_x]
    w_refs = refs[n_x:n_x + n_w]
    extra_refs = refs[n_x + n_w:n_x + n_w + n_extra]
    out_refs = refs[n_x + n_w + n_extra:]
    accs = [jnp.dot(x_refs[a][...].astype(BF16), w_refs[b][...],
                    preferred_element_type=F32) for a, b in dot_pairs]
    y = epilogue(accs, extra_refs)
    if row_norm:
        o_ref, ssq_ref = out_refs
        gain_ref = extra_refs[-1]
        j = pl.program_id(1)
        o_ref[:, pl.ds(pl.multiple_of(j * bn, bn), bn)] = y
        part = jnp.broadcast_to(jnp.sum(y * y, axis=-1, keepdims=True), ssq_ref.shape)

        @pl.when(j == 0)
        def _():
            ssq_ref[...] = part

        @pl.when(j != 0)
        def _():
            ssq_ref[...] += part

        @pl.when(j == pl.num_programs(1) - 1)
        def _():
            r = _rms_scale(ssq_ref, o_ref.shape[1])
            for c in range(o_ref.shape[1] // LANES):
                cs = slice(c * LANES, (c + 1) * LANES)
                o_ref[:, cs] = o_ref[:, cs] * r * gain_ref[:, cs]
        return
    out_refs[0][...] = y.astype(out_refs[0].dtype)
    if emit_norm:
        gain_ref = extra_refs[-1]
        xg_ref, ssq_ref = out_refs[1:]
        xg_ref[...] = (y * gain_ref[...]).astype(xg_ref.dtype)
        part = jnp.broadcast_to(jnp.sum(y * y, axis=-1, keepdims=True), ssq_ref.shape)
        j = pl.program_id(1)

        @pl.when(j == 0)
        def _():
            ssq_ref[...] = part

        @pl.when(j != 0)
        def _():
            ssq_ref[...] += part


def _matmul(name, xs, ws, extras, epilogue, *, dot_pairs, n_out, out_dtype, bm, bn,
            norm_gain=None, row_norm_gain=None):
    t = xs[0][0].shape[0]
    emit_norm = norm_gain is not None
    row_norm = row_norm_gain is not None
    if emit_norm:
        extras = list(extras) + [("row", norm_gain, 0)]
    if row_norm:
        extras = list(extras) + [("fullrow", row_norm_gain, 0)]
    in_specs, operands = [], []
    for arr, k, cb in xs:
        in_specs.append(pl.BlockSpec((bm, k), lambda i, j, cb=cb: (i, cb)))
        operands.append(arr)
    for arr, cb0 in ws:
        k = arr.shape[0]
        in_specs.append(pl.BlockSpec((k, bn), lambda i, j, cb0=cb0: (0, cb0 + j)))
        operands.append(arr)
    for kind, arr, cb0 in extras:
        if kind == "tile":
            in_specs.append(pl.BlockSpec((bm, bn), lambda i, j, cb0=cb0: (i, cb0 + j)))
        elif kind == "row":
            in_specs.append(pl.BlockSpec((1, bn), lambda i, j, cb0=cb0: (0, cb0 + j)))
        elif kind == "rowstat":
            in_specs.append(pl.BlockSpec((bm, LANES), lambda i, j: (i, 0)))
        else:
            in_specs.append(pl.BlockSpec((1, n_out), lambda i, j: (0, 0)))
        operands.append(arr)
    tile_spec = pl.BlockSpec((bm, bn), lambda i, j: (i, j))
    stat_spec = pl.BlockSpec((bm, LANES), lambda i, j: (i, 0))
    out_shape = jax.ShapeDtypeStruct((t, n_out), out_dtype)
    out_specs = tile_spec
    scratch_shapes = []
    if emit_norm:
        out_shape = (out_shape, jax.ShapeDtypeStruct((t, n_out), BF16),
                     jax.ShapeDtypeStruct((t, LANES), F32))
        out_specs = (tile_spec, tile_spec, stat_spec)
    if row_norm:
        assert out_dtype == F32 and not emit_norm
        out_specs = pl.BlockSpec((bm, n_out), lambda i, j: (i, 0))
        scratch_shapes = [pltpu.VMEM((bm, LANES), F32)]
    sequential_cols = emit_norm or row_norm
    return pl.pallas_call(
        functools.partial(_mm_kernel, n_x=len(xs), n_w=len(ws), n_extra=len(extras),
                          dot_pairs=dot_pairs, epilogue=epilogue, emit_norm=emit_norm,
                          row_norm=row_norm, bn=bn),
        out_shape=out_shape,
        grid=(t // bm, n_out // bn),
        in_specs=in_specs,
        out_specs=out_specs,
        scratch_shapes=scratch_shapes,
        compiler_params=_params("parallel", "arbitrary" if sequential_cols else "parallel"),
        name=name,
    )(*operands)


def _rms_scale(ssq_ref, n):
    return lax.rsqrt(ssq_ref[:, :1] / n + RMS_EPS)


def _gelu_exact(x):
    return 0.5 * x * (1.0 + lax.erf(x * math.sqrt(0.5)))


def _ep_gelu(accs, extras):
    return _gelu_exact(accs[0])


def _ep_colscale(accs, extras):
    return accs[0] * extras[0][...]


def _ep_sigmoid_bias(accs, extras):
    return jax.nn.sigmoid(accs[0] + extras[0][...])


def _ep_gated(accs, extras):
    return extras[0][...].astype(F32) * accs[0]


def _ep_gated_add(accs, extras):
    return extras[1][...] + extras[0][...].astype(F32) * accs[0]


def _ep_residual(accs, extras):
    return extras[0][...] + accs[0]


def _ep_swiglu_normed(accs, extras, *, n):
    r = _rms_scale(extras[0], n)
    return jax.nn.silu(r * accs[0]) * (r * accs[1])


def _ep_ple_normed(accs, extras, *, n):
    r = _rms_scale(extras[1], n)
    return extras[0][...] + jax.nn.sigmoid(r * accs[0]) * accs[1]


def _gmlp_kernel(gu_ref, gv_ref, lng_ref, lnb_ref, ws_ref, bs_ref, o_ref, wm_ref,
                 *, rows, n_groups, group_dim):
    @pl.when(pl.program_id(0) == 0)
    def _():
        r = lax.broadcasted_iota(jnp.int32, (CHUNK, CHUNK), 0)
        c = lax.broadcasted_iota(jnp.int32, (CHUNK, CHUNK), 1)
        for g in range(n_groups):
            wm_ref[g] = jnp.where(c <= r, ws_ref[g], 0.0).astype(BF16)

    for ch in range(rows // CHUNK):
        rs = slice(ch * CHUNK, (ch + 1) * CHUNK)
        v = gv_ref[rs, :].astype(F32)
        mu = jnp.mean(v, axis=-1, keepdims=True)
        vc = v - mu
        var = jnp.mean(vc * vc, axis=-1, keepdims=True)
        vn = (vc * lax.rsqrt(var + LN_EPS) * lng_ref[...] + lnb_ref[...]).astype(BF16)
        for g in range(n_groups):
            cs = slice(g * group_dim, (g + 1) * group_dim)
            mixed = jnp.dot(wm_ref[g], vn[:, cs], preferred_element_type=F32) + bs_ref[g]
            o_ref[rs, cs] = (gu_ref[rs, cs].astype(F32) * mixed).astype(o_ref.dtype)


def _gmlp(guv, ln_g, ln_b, w_s, b_s, *, rows=256):
    t = guv.shape[0]
    d = guv.shape[1] // 2
    n_groups = w_s.shape[0]
    return pl.pallas_call(
        functools.partial(_gmlp_kernel, rows=rows, n_groups=n_groups, group_dim=d // n_groups),
        out_shape=jax.ShapeDtypeStruct((t, d), BF16),
        grid=(t // rows,),
        in_specs=[pl.BlockSpec((rows, d), lambda i: (i, 0)),
                  pl.BlockSpec((rows, d), lambda i: (i, 1)),
                  pl.BlockSpec((1, d), lambda i: (0, 0)),
                  pl.BlockSpec((1, d), lambda i: (0, 0)),
                  pl.BlockSpec((n_groups, CHUNK, CHUNK), lambda i: (0, 0, 0)),
                  pl.BlockSpec((n_groups, CHUNK, 1), lambda i: (0, 0, 0))],
        out_specs=pl.BlockSpec((rows, d), lambda i: (i, 0)),
        scratch_shapes=[pltpu.VMEM((n_groups, CHUNK, CHUNK), BF16)],
        compiler_params=_params("arbitrary"),
        name="gmlp_gating",
    )(guv, guv, ln_g.reshape(1, d), ln_b.reshape(1, d), w_s, b_s.reshape(n_groups, CHUNK, 1))


def _lanes(x, n):
    return jnp.concatenate([x] * (n // LANES), axis=1)


def _attn_kernel(q_ref, k_ref, v_ref, lq1_ref, lk1_ref, lq2_ref, lk2_ref, sg_ref, o_ref,
                 s_ref, p_ref, m_ref, l_ref, a_ref, acc_ref, *, blk, strip, n_heads, lam_init):
    h = pl.program_id(1)
    i = pl.program_id(2)
    slope = jnp.exp2(-8.0 * jnp.full((1, blk), h + 1, F32) / n_heads) * LOG2_E
    k_idx = lax.broadcasted_iota(jnp.int32, (1, blk), 1).astype(F32)

    m_ref[...] = jnp.full(m_ref.shape, MASK_VALUE, F32)
    l_ref[...] = jnp.zeros(l_ref.shape, F32)
    acc_ref[...] = jnp.zeros(acc_ref.shape, F32)
    p_ref[1] = jnp.zeros(p_ref.shape[1:], BF16)
    a_ref[1] = jnp.ones(a_ref.shape[1:], F32)

    def qk(mp, j):
        off = pl.multiple_of(j * blk, blk)
        cs = slice(mp * HEAD_DIM, (mp + 1) * HEAD_DIM)
        s_ref[mp] = lax.dot_general(q_ref[:, cs], k_ref[pl.ds(off, blk), cs],
                                    (((1,), (1,)), ((), ())),
                                    preferred_element_type=F32)

    def pv(mp, j):
        off = pl.multiple_of(j * blk, blk)
        acc_ref[mp] = (acc_ref[mp] * _lanes(a_ref[mp], V_HEAD_DIM)
                       + jnp.dot(p_ref[mp], v_ref[pl.ds(off, blk), :],
                                 preferred_element_type=F32))

    def softmax_strips(mp, bias, masked):
        for r in range(blk // strip):
            rows = slice(r * strip, (r + 1) * strip)
            s = s_ref[mp, rows, :] + bias
            if masked:
                row = r * strip + lax.broadcasted_iota(jnp.int32, (strip, blk), 0)
                col = lax.broadcasted_iota(jnp.int32, (strip, blk), 1)
                s = jnp.where(col <= row, s, MASK_VALUE)
            m_old = m_ref[mp, rows, :]
            m_new = jnp.maximum(m_old, jnp.max(s, axis=-1, keepdims=True))
            alpha = jnp.exp2(m_old - m_new)
            p = jnp.exp2(s - _lanes(m_new, blk))
            p_sum = p[:, :LANES]
            for c in range(1, blk // LANES):
                p_sum = p_sum + p[:, c * LANES:(c + 1) * LANES]
            l_ref[mp, rows, :] = alpha * l_ref[mp, rows, :] + p_sum
            m_ref[mp, rows, :] = m_new
            a_ref[mp, rows, :] = alpha
            p_ref[mp, rows, :] = p.astype(BF16)

    def step(j, masked):
        bias = slope * (k_idx + ((j - i) * blk).astype(F32))
        pv(1, jnp.maximum(j - 1, 0))
        qk(1, j)
        softmax_strips(0, bias, masked)
        pv(0, j)
        if not masked:
            qk(0, j + 1)
        softmax_strips(1, bias, masked)

    def two_steps(t, carry):
        step(2 * t, masked=False)
        step(2 * t + 1, masked=False)
        return carry

    qk(0, 0)
    lax.fori_loop(0, lax.shift_right_logical(i, 1), two_steps, 0)

    @pl.when(lax.bitwise_and(i, 1) == 1)
    def _():
        step(i - 1, masked=False)

    step(i, masked=True)
    pv(1, i)

    lam = (jnp.exp(jnp.sum(lq1_ref[...] * lk1_ref[...], axis=-1, keepdims=True))
           - jnp.exp(jnp.sum(lq2_ref[...] * lk2_ref[...], axis=-1, keepdims=True))
           + lam_init)
    l1 = jnp.sum(l_ref[0], axis=-1, keepdims=True)
    l2 = jnp.sum(l_ref[1], axis=-1, keepdims=True)
    o = acc_ref[0] / l1 - lam * (acc_ref[1] / l2)
    ms = jnp.mean(o * o, axis=-1, keepdims=True)
    y = (o * lax.rsqrt(ms + LN_EPS) * sg_ref[...]) * (1.0 - lam_init)
    o_ref[...] = y.astype(o_ref.dtype)


def _diff_attention(qkv, lq1, lk1, lq2, lk2, subln_g, *, batch, seq, n_heads, lam_init,
                    blk=512, strip=32):
    t = qkv.shape[0]
    nq = seq // blk
    vec = lambda a: a.reshape(1, -1)
    vec_spec = lambda n: pl.BlockSpec((1, n), lambda b, h, i: (0, 0))
    return pl.pallas_call(
        functools.partial(_attn_kernel, blk=blk, strip=strip, n_heads=n_heads,
                          lam_init=lam_init),
        out_shape=jax.ShapeDtypeStruct((t, n_heads * V_HEAD_DIM), BF16),
        grid=(batch, n_heads, nq),
        in_specs=[pl.BlockSpec((blk, 2 * HEAD_DIM), lambda b, h, i: (b * nq + i, h)),
                  pl.BlockSpec((seq, 2 * HEAD_DIM), lambda b, h, i: (b, n_heads + h)),
                  pl.BlockSpec((seq, V_HEAD_DIM), lambda b, h, i: (b, 2 * n_heads + h)),
                  vec_spec(HEAD_DIM), vec_spec(HEAD_DIM), vec_spec(HEAD_DIM), vec_spec(HEAD_DIM),
                  vec_spec(V_HEAD_DIM)],
        out_specs=pl.BlockSpec((blk, V_HEAD_DIM), lambda b, h, i: (b * nq + i, h)),
        scratch_shapes=[pltpu.VMEM((2, blk, blk), F32),
                        pltpu.VMEM((2, blk, blk), BF16),
                        pltpu.VMEM((2, blk, LANES), F32),
                        pltpu.VMEM((2, blk, LANES), F32),
                        pltpu.VMEM((2, blk, LANES), F32),
                        pltpu.VMEM((2, blk, V_HEAD_DIM), F32)],
        compiler_params=_params("parallel", "parallel", "arbitrary"),
        name="diff_attention",
    )(qkv, qkv, qkv, vec(lq1), vec(lk1), vec(lq2), vec(lk2), vec(subln_g))


def kernel(x, p, g_mix, w_in, b_gate, ln_v_g, ln_v_b, w_s, b_s, lambda_q1, lambda_k1, lambda_q2, lambda_k2, subln_g, w_br_a, w_br_b, w_o, g_ffn, w_gu, w_down, g_ple, w_ple_gate, w_ple_proj, g_final):
    batch, seq, d_model = x.shape
    depth = w_in.shape[0]
    t = batch * seq
    d_gmlp = ln_v_g.shape[1]
    d_attn = w_br_b.shape[1]
    n_heads = d_attn // V_HEAD_DIM
    d_qk = n_heads * 2 * HEAD_DIM
    d_ff = w_down.shape[1]
    ple_dim = p.shape[-1]

    xf = x.reshape(t, d_model)
    for layer in range(depth):
        lam_init = 0.8 - 0.6 * math.exp(-0.3 * layer)
        w_in_b = w_in[layer].astype(BF16)

        h = _rmsnorm(xf, g_mix[layer], BF16)
        hx = [(h, d_model, 0)]
        bn = 1024
        guv = _matmul("in_proj_gelu", hx, [(w_in_b, 0)], [], _ep_gelu, dot_pairs=((0, 0),),
                      n_out=2 * d_gmlp, out_dtype=BF16, bm=1024, bn=bn)
        qkv_scale = jnp.concatenate([jnp.full((d_qk,), HEAD_DIM ** -0.5 * LOG2_E, F32),
                                     jnp.ones((d_qk + d_attn,), F32)]).reshape(1, -1)
        qkv = _matmul("in_proj_qkv", hx, [(w_in_b, 2 * d_gmlp // bn)], [("row", qkv_scale, 0)], _ep_colscale,
                      dot_pairs=((0, 0),), n_out=2 * d_qk + d_attn, out_dtype=BF16, bm=1024, bn=bn)
        gates = _matmul("in_proj_gates", hx, [(w_in_b, (2 * d_gmlp + 2 * d_qk + d_attn) // bn)],
                        [("row", b_gate[layer].reshape(1, -1), 0)], _ep_sigmoid_bias,
                        dot_pairs=((0, 0),), n_out=2 * d_model, out_dtype=BF16, bm=1024, bn=bn)

        y_a = _gmlp(guv, ln_v_g[layer], ln_v_b[layer], w_s[layer], b_s[layer])
        y_b = _diff_attention(qkv, lambda_q1[layer], lambda_k1[layer], lambda_q2[layer],
                              lambda_k2[layer], subln_g[layer], batch=batch, seq=seq,
                              n_heads=n_heads, lam_init=lam_init)

        bn = 1024
        merged_a = _matmul("merge_a", [(y_a, d_gmlp, 0)], [(w_br_a[layer].astype(BF16), 0)],
                           [("tile", gates, 0)], _ep_gated, dot_pairs=((0, 0),),
                           n_out=d_model, out_dtype=F32, bm=1024, bn=bn)
        merged = _matmul("merge_b", [(y_b, d_attn, 0)], [(w_br_b[layer].astype(BF16), 0)],
                         [("tile", gates, d_model // bn), ("tile", merged_a, 0)], _ep_gated_add,
                         dot_pairs=((0, 0),), n_out=d_model, out_dtype=BF16, bm=1024, bn=bn)
        xf, xg, ssq = _matmul("out_proj", [(merged, d_model, 0)], [(w_o[layer].astype(BF16), 0)],
                              [("tile", xf, 0)], _ep_residual, dot_pairs=((0, 0),),
                              n_out=d_model, out_dtype=F32, bm=1024, bn=512,
                              norm_gain=g_ffn[layer].reshape(1, -1))

        bn = 256
        w_gu_b = w_gu[layer].astype(BF16)
        act = _matmul("ffn_up", [(xg, d_model, 0)], [(w_gu_b, 0), (w_gu_b, d_ff // bn)],
                      [("rowstat", ssq, 0)], functools.partial(_ep_swiglu_normed, n=d_model),
                      dot_pairs=((0, 0), (0, 1)), n_out=d_ff, out_dtype=BF16, bm=2048, bn=bn)
        xf, xg, ssq = _matmul("ffn_down", [(act, d_ff, 0)], [(w_down[layer].astype(BF16), 0)],
                              [("tile", xf, 0)], _ep_residual, dot_pairs=((0, 0),),
                              n_out=d_model, out_dtype=F32, bm=512, bn=512,
                              norm_gain=g_ple[layer].reshape(1, -1))

        ple_bm, ple_bn = (512, 1024) if layer == depth - 1 else (1024, 512)
        xf = _matmul("ple", [(xg, d_model, 0), (p[layer].reshape(t, ple_dim), ple_dim, 0)],
                     [(w_ple_gate[layer].astype(BF16), 0), (w_ple_proj[layer].astype(BF16), 0)],
                     [("tile", xf, 0), ("rowstat", ssq, 0)],
                     functools.partial(_ep_ple_normed, n=d_model), dot_pairs=((0, 0), (1, 1)),
                     n_out=d_model, out_dtype=F32, bm=ple_bm, bn=ple_bn,
                     row_norm_gain=g_final.reshape(1, -1) if layer == depth - 1 else None)

    return xf.reshape(batch, seq, d_model)
```

```python
import functools
import math

import jax
import jax.numpy as jnp
from jax import lax
from jax.experimental import pallas as pl
from jax.experimental.pallas import tpu as pltpu

F32 = jnp.float32
BF16 = jnp.bfloat16

V7X_VMEM_LIMIT_BYTES = 56 * 1024 * 1024

LANES = 128
LOG2_E = math.log2(math.e)

HEAD_DIM = 128
V_HEAD_DIM = 2 * HEAD_DIM
CHUNK = 128
RMS_EPS = 1e-6
LN_EPS = 1e-5
MASK_VALUE = -1e30


def _params(*semantics):
    return pltpu.CompilerParams(dimension_semantics=semantics,
                                vmem_limit_bytes=V7X_VMEM_LIMIT_BYTES)


def _rmsnorm_kernel(x_ref, g_ref, o_ref, *, eps):
    x = x_ref[...]
    ms = jnp.mean(x * x, axis=-1, keepdims=True)
    o_ref[...] = (x * lax.rsqrt(ms + eps) * g_ref[...]).astype(o_ref.dtype)


def _rmsnorm(x, g, out_dtype, *, rows=256):
    t, d = x.shape
    return pl.pallas_call(
        functools.partial(_rmsnorm_kernel, eps=RMS_EPS),
        out_shape=jax.ShapeDtypeStruct((t, d), out_dtype),
        grid=(t // rows,),
        in_specs=[pl.BlockSpec((rows, d), lambda i: (i, 0)),
                  pl.BlockSpec((1, d), lambda i: (0, 0))],
        out_specs=pl.BlockSpec((rows, d), lambda i: (i, 0)),
        compiler_params=_params("parallel"),
        name="rmsnorm",
    )(x, g.reshape(1, d))


def _mm_kernel(*refs, n_x, n_w, n_extra, dot_pairs, epilogue, emit_norm, row_norm, bn):
    x_refs = refs[:n_x]
    w_refs = refs[n_x:n_x + n_w]
    extra_refs = refs[n_x + n_w:n_x + n_w + n_extra]
    out_refs = refs[n_x + n_w + n_extra:]
    accs = [jnp.dot(x_refs[a][...].astype(BF16), w_refs[b][...],
                    preferred_element_type=F32) for a, b in dot_pairs]
    y = epilogue(accs, extra_refs)
    if row_norm:
        o_ref, ssq_ref = out_refs
        gain_ref = extra_refs[-1]
        j = pl.program_id(1)
        o_ref[:, pl.ds(pl.multiple_of(j * bn, bn), bn)] = y
        part = jnp.broadcast_to(jnp.sum(y * y, axis=-1, keepdims=True), ssq_ref.shape)

        @pl.when(j == 0)
        def _():
            ssq_ref[...] = part

        @pl.when(j != 0)
        def _():
            ssq_ref[...] += part

        @pl.when(j == pl.num_programs(1) - 1)
        def _():
            r = _rms_scale(ssq_ref, o_ref.shape[1])
            for c in range(o_ref.shape[1] // LANES):
                cs = slice(c * LANES, (c + 1) * LANES)
                o_ref[:, cs] = o_ref[:, cs] * r * gain_ref[:, cs]
        return
    out_refs[0][...] = y.astype(out_refs[0].dtype)
    if emit_norm:
        gain_ref = extra_refs[-1]
        xg_ref, ssq_ref = out_refs[1:]
        xg_ref[...] = (y * gain_ref[...]).astype(xg_ref.dtype)
        part = jnp.broadcast_to(jnp.sum(y * y, axis=-1, keepdims=True), ssq_ref.shape)
        j = pl.program_id(1)

        @pl.when(j == 0)
        def _():
            ssq_ref[...] = part

        @pl.when(j != 0)
        def _():
            ssq_ref[...] += part


def _matmul(name, xs, ws, extras, epilogue, *, dot_pairs, n_out, out_dtype, bm, bn,
            norm_gain=None, row_norm_gain=None):
    t = xs[0][0].shape[0]
    emit_norm = norm_gain is not None
    row_norm = row_norm_gain is not None
    if emit_norm:
        extras = list(extras) + [("row", norm_gain, 0)]
    if row_norm:
        extras = list(extras) + [("fullrow", row_norm_gain, 0)]
    in_specs, operands = [], []
    for arr, k, cb in xs:
        in_specs.append(pl.BlockSpec((bm, k), lambda i, j, cb=cb: (i, cb)))
        operands.append(arr)
    for arr, cb0 in ws:
        k = arr.shape[0]
        in_specs.append(pl.BlockSpec((k, bn), lambda i, j, cb0=cb0: (0, cb0 + j)))
        operands.append(arr)
    for kind, arr, cb0 in extras:
        if kind == "tile":
            in_specs.append(pl.BlockSpec((bm, bn), lambda i, j, cb0=cb0: (i, cb0 + j)))
        elif kind == "row":
            in_specs.append(pl.BlockSpec((1, bn), lambda i, j, cb0=cb0: (0, cb0 + j)))
        elif kind == "rowstat":
            in_specs.append(pl.BlockSpec((bm, LANES), lambda i, j: (i, 0)))
        else:
            in_specs.append(pl.BlockSpec((1, n_out), lambda i, j: (0, 0)))
        operands.append(arr)
    tile_spec = pl.BlockSpec((bm, bn), lambda i, j: (i, j))
    stat_spec = pl.BlockSpec((bm, LANES), lambda i, j: (i, 0))
    out_shape = jax.ShapeDtypeStruct((t, n_out), out_dtype)
    out_specs = tile_spec
    scratch_shapes = []
    if emit_norm:
        out_shape = (out_shape, jax.ShapeDtypeStruct((t, n_out), BF16),
                     jax.ShapeDtypeStruct((t, LANES), F32))
        out_specs = (tile_spec, tile_spec, stat_spec)
    if row_norm:
        assert out_dtype == F32 and not emit_norm
        out_specs = pl.BlockSpec((bm, n_out), lambda i, j: (i, 0))
        scratch_shapes = [pltpu.VMEM((bm, LANES), F32)]
    sequential_cols = emit_norm or row_norm
    return pl.pallas_call(
        functools.partial(_mm_kernel, n_x=len(xs), n_w=len(ws), n_extra=len(extras),
                          dot_pairs=dot_pairs, epilogue=epilogue, emit_norm=emit_norm,
                          row_norm=row_norm, bn=bn),
        out_shape=out_shape,
        grid=(t // bm, n_out // bn),
        in_specs=in_specs,
        out_specs=out_specs,
        scratch_shapes=scratch_shapes,
        compiler_params=_params("parallel", "arbitrary" if sequential_cols else "parallel"),
        name=name,
    )(*operands)


def _rms_scale(ssq_ref, n):
    return lax.rsqrt(ssq_ref[:, :1] / n + RMS_EPS)


def _gelu_exact(x):
    return 0.5 * x * (1.0 + lax.erf(x * math.sqrt(0.5)))


def _ep_gelu(accs, extras):
    return _gelu_exact(accs[0])


def _ep_colscale(accs, extras):
    return accs[0] * extras[0][...]


def _ep_sigmoid_bias(accs, extras):
    return jax.nn.sigmoid(accs[0] + extras[0][...])


def _ep_gated(accs, extras):
    return extras[0][...].astype(F32) * accs[0]


def _ep_gated_add(accs, extras):
    return extras[1][...] + extras[0][...].astype(F32) * accs[0]


def _ep_residual(accs, extras):
    return extras[0][...] + accs[0]


def _ep_swiglu_normed(accs, extras, *, n):
    r = _rms_scale(extras[0], n)
    return jax.nn.silu(r * accs[0]) * (r * accs[1])


def _ep_ple_normed(accs, extras, *, n):
    r = _rms_scale(extras[1], n)
    return extras[0][...] + jax.nn.sigmoid(r * accs[0]) * accs[1]


def _swiglu_up_kernel(x_ref, wg_ref, wu_ref, ssq_ref, o_ref, w_ref, *, bn, n):
    @pl.when(pl.program_id(1) == 0)
    def _():
        w_ref[:, :bn] = wg_ref[...].astype(BF16)
        w_ref[:, bn:] = wu_ref[...].astype(BF16)

    acc = jnp.dot(x_ref[...], w_ref[...], preferred_element_type=F32)
    r = _rms_scale(ssq_ref, n)
    o_ref[...] = (jax.nn.silu(r * acc[:, :bn]) * (r * acc[:, bn:])).astype(o_ref.dtype)


def _swiglu_up(xg, ssq, w_gu, *, bm, bn):
    t, d = xg.shape
    d_ff = w_gu.shape[1] // 2
    return pl.pallas_call(
        functools.partial(_swiglu_up_kernel, bn=bn, n=d),
        out_shape=jax.ShapeDtypeStruct((t, d_ff), BF16),
        grid=(d_ff // bn, t // bm),
        in_specs=[pl.BlockSpec((bm, d), lambda j, i: (i, 0)),
                  pl.BlockSpec((d, bn), lambda j, i: (0, j)),
                  pl.BlockSpec((d, bn), lambda j, i: (0, d_ff // bn + j)),
                  pl.BlockSpec((bm, LANES), lambda j, i: (i, 0))],
        out_specs=pl.BlockSpec((bm, bn), lambda j, i: (i, j)),
        scratch_shapes=[pltpu.VMEM((d, 2 * bn), BF16)],
        compiler_params=_params("arbitrary", "arbitrary"),
        name="ffn_up",
    )(xg, w_gu, w_gu, ssq)


def _gmlp_kernel(gu_ref, gv_ref, lng_ref, lnb_ref, ws_ref, bs_ref, o_ref, wm_ref,
                 *, rows, n_groups, group_dim):
    @pl.when(pl.program_id(0) == 0)
    def _():
        r = lax.broadcasted_iota(jnp.int32, (CHUNK, CHUNK), 0)
        c = lax.broadcasted_iota(jnp.int32, (CHUNK, CHUNK), 1)
        for g in range(n_groups):
            wm_ref[g] = jnp.where(c <= r, ws_ref[g], 0.0).astype(BF16)

    for ch in range(rows // CHUNK):
        rs = slice(ch * CHUNK, (ch + 1) * CHUNK)
        v = gv_ref[rs, :].astype(F32)
        mu = jnp.mean(v, axis=-1, keepdims=True)
        vc = v - mu
        var = jnp.mean(vc * vc, axis=-1, keepdims=True)
        vn = (vc * lax.rsqrt(var + LN_EPS) * lng_ref[...] + lnb_ref[...]).astype(BF16)
        for g in range(n_groups):
            cs = slice(g * group_dim, (g + 1) * group_dim)
            mixed = jnp.dot(wm_ref[g], vn[:, cs], preferred_element_type=F32) + bs_ref[g]
            o_ref[rs, cs] = (gu_ref[rs, cs].astype(F32) * mixed).astype(o_ref.dtype)


def _gmlp(guv, ln_g, ln_b, w_s, b_s, *, rows=256):
    t = guv.shape[0]
    d = guv.shape[1] // 2
    n_groups = w_s.shape[0]
    return pl.pallas_call(
        functools.partial(_gmlp_kernel, rows=rows, n_groups=n_groups, group_dim=d // n_groups),
        out_shape=jax.ShapeDtypeStruct((t, d), BF16),
        grid=(t // rows,),
        in_specs=[pl.BlockSpec((rows, d), lambda i: (i, 0)),
                  pl.BlockSpec((rows, d), lambda i: (i, 1)),
                  pl.BlockSpec((1, d), lambda i: (0, 0)),
                  pl.BlockSpec((1, d), lambda i: (0, 0)),
                  pl.BlockSpec((n_groups, CHUNK, CHUNK), lambda i: (0, 0, 0)),
                  pl.BlockSpec((n_groups, CHUNK, 1), lambda i: (0, 0, 0))],
        out_specs=pl.BlockSpec((rows, d), lambda i: (i, 0)),
        scratch_shapes=[pltpu.VMEM((n_groups, CHUNK, CHUNK), BF16)],
        compiler_params=_params("arbitrary"),
        name="gmlp_gating",
    )(guv, guv, ln_g.reshape(1, d), ln_b.reshape(1, d), w_s, b_s.reshape(n_groups, CHUNK, 1))


def _lanes(x, n):
    return jnp.concatenate([x] * (n // LANES), axis=1)


def _attn_kernel(q_ref, k_ref, v_ref, lq1_ref, lk1_ref, lq2_ref, lk2_ref, sg_ref, o_ref,
                 s_ref, p_ref, m_ref, l_ref, a_ref, acc_ref, *, blk, strip, n_heads, lam_init):
    h = pl.program_id(1)
    i = pl.program_id(2)
    slope = jnp.exp2(-8.0 * jnp.full((1, blk), h + 1, F32) / n_heads) * LOG2_E
    k_idx = lax.broadcasted_iota(jnp.int32, (1, blk), 1).astype(F32)

    m_ref[...] = jnp.full(m_ref.shape, MASK_VALUE, F32)
    l_ref[...] = jnp.zeros(l_ref.shape, F32)
    acc_ref[...] = jnp.zeros(acc_ref.shape, F32)
    p_ref[1] = jnp.zeros(p_ref.shape[1:], BF16)
    a_ref[1] = jnp.ones(a_ref.shape[1:], F32)

    def qk(mp, j):
        off = pl.multiple_of(j * blk, blk)
        cs = slice(mp * HEAD_DIM, (mp + 1) * HEAD_DIM)
        s_ref[mp] = lax.dot_general(q_ref[:, cs], k_ref[pl.ds(off, blk), cs],
                                    (((1,), (1,)), ((), ())),
                                    preferred_element_type=F32)

    def pv(mp, j):
        off = pl.multiple_of(j * blk, blk)
        acc_ref[mp] = (acc_ref[mp] * _lanes(a_ref[mp], V_HEAD_DIM)
                       + jnp.dot(p_ref[mp], v_ref[pl.ds(off, blk), :],
                                 preferred_element_type=F32))

    def softmax_strips(mp, bias, masked):
        for r in range(blk // strip):
            rows = slice(r * strip, (r + 1) * strip)
            s = s_ref[mp, rows, :] + bias
            if masked:
                row = r * strip + lax.broadcasted_iota(jnp.int32, (strip, blk), 0)
                col = lax.broadcasted_iota(jnp.int32, (strip, blk), 1)
                s = jnp.where(col <= row, s, MASK_VALUE)
            m_old = m_ref[mp, rows, :]
            m_new = jnp.maximum(m_old, jnp.max(s, axis=-1, keepdims=True))
            alpha = jnp.exp2(m_old - m_new)
            p = jnp.exp2(s - _lanes(m_new, blk))
            p_sum = p[:, :LANES]
            for c in range(1, blk // LANES):
                p_sum = p_sum + p[:, c * LANES:(c + 1) * LANES]
            l_ref[mp, rows, :] = alpha * l_ref[mp, rows, :] + p_sum
            m_ref[mp, rows, :] = m_new
            a_ref[mp, rows, :] = alpha
            p_ref[mp, rows, :] = p.astype(BF16)

    def step(j, masked):
        bias = slope * (k_idx + ((j - i) * blk).astype(F32))
        pv(1, jnp.maximum(j - 1, 0))
        qk(1, j)
        softmax_strips(0, bias, masked)
        pv(0, j)
        if not masked:
            qk(0, j + 1)
        softmax_strips(1, bias, masked)

    def two_steps(t, carry):
        step(2 * t, masked=False)
        step(2 * t + 1, masked=False)
        return carry

    qk(0, 0)
    lax.fori_loop(0, lax.shift_right_logical(i, 1), two_steps, 0)

    @pl.when(lax.bitwise_and(i, 1) == 1)
    def _():
        step(i - 1, masked=False)

    step(i, masked=True)
    pv(1, i)

    lam = (jnp.exp(jnp.sum(lq1_ref[...] * lk1_ref[...], axis=-1, keepdims=True))
           - jnp.exp(jnp.sum(lq2_ref[...] * lk2_ref[...], axis=-1, keepdims=True))
           + lam_init)
    l1 = jnp.sum(l_ref[0], axis=-1, keepdims=True)
    l2 = jnp.sum(l_ref[1], axis=-1, keepdims=True)
    o = acc_ref[0] / l1 - lam * (acc_ref[1] / l2)
    ms = jnp.mean(o * o, axis=-1, keepdims=True)
    y = (o * lax.rsqrt(ms + LN_EPS) * sg_ref[...]) * (1.0 - lam_init)
    o_ref[...] = y.astype(o_ref.dtype)


def _diff_attention(qkv, lq1, lk1, lq2, lk2, subln_g, *, batch, seq, n_heads, lam_init,
                    blk=512, strip=32):
    t = qkv.shape[0]
    nq = seq // blk
    vec = lambda a: a.reshape(1, -1)
    vec_spec = lambda n: pl.BlockSpec((1, n), lambda b, h, i: (0, 0))
    return pl.pallas_call(
        functools.partial(_attn_kernel, blk=blk, strip=strip, n_heads=n_heads,
                          lam_init=lam_init),
        out_shape=jax.ShapeDtypeStruct((t, n_heads * V_HEAD_DIM), BF16),
        grid=(batch, n_heads, nq),
        in_specs=[pl.BlockSpec((blk, 2 * HEAD_DIM), lambda b, h, i: (b * nq + i, h)),
                  pl.BlockSpec((seq, 2 * HEAD_DIM), lambda b, h, i: (b, n_heads + h)),
                  pl.BlockSpec((seq, V_HEAD_DIM), lambda b, h, i: (b, 2 * n_heads + h)),
                  vec_spec(HEAD_DIM), vec_spec(HEAD_DIM), vec_spec(HEAD_DIM), vec_spec(HEAD_DIM),
                  vec_spec(V_HEAD_DIM)],
        out_specs=pl.BlockSpec((blk, V_HEAD_DIM), lambda b, h, i: (b * nq + i, h)),
        scratch_shapes=[pltpu.VMEM((2, blk, blk), F32),
                        pltpu.VMEM((2, blk, blk), BF16),
                        pltpu.VMEM((2, blk, LANES), F32),
                        pltpu.VMEM((2, blk, LANES), F32),
                        pltpu.VMEM((2, blk, LANES), F32),
                        pltpu.VMEM((2, blk, V_HEAD_DIM), F32)],
        compiler_params=_params("parallel", "parallel", "arbitrary"),
        name="diff_attention",
    )(qkv, qkv, qkv, vec(lq1), vec(lk1), vec(lq2), vec(lk2), vec(subln_g))


def kernel(x, p, g_mix, w_in, b_gate, ln_v_g, ln_v_b, w_s, b_s, lambda_q1, lambda_k1, lambda_q2, lambda_k2, subln_g, w_br_a, w_br_b, w_o, g_ffn, w_gu, w_down, g_ple, w_ple_gate, w_ple_proj, g_final):
    batch, seq, d_model = x.shape
    depth = w_in.shape[0]
    t = batch * seq
    d_gmlp = ln_v_g.shape[1]
    d_attn = w_br_b.shape[1]
    n_heads = d_attn // V_HEAD_DIM
    d_qk = n_heads * 2 * HEAD_DIM
    d_ff = w_down.shape[1]
    ple_dim = p.shape[-1]

    xf = x.reshape(t, d_model)
    for layer in range(depth):
        lam_init = 0.8 - 0.6 * math.exp(-0.3 * layer)
        w_in_b = w_in[layer].astype(BF16)

        h = _rmsnorm(xf, g_mix[layer], BF16)
        hx = [(h, d_model, 0)]
        bn = 1024
        guv = _matmul("in_proj_gelu", hx, [(w_in_b, 0)], [], _ep_gelu, dot_pairs=((0, 0),),
                      n_out=2 * d_gmlp, out_dtype=BF16, bm=1024, bn=bn)
        qkv_scale = jnp.concatenate([jnp.full((d_qk,), HEAD_DIM ** -0.5 * LOG2_E, F32),
                                     jnp.ones((d_qk + d_attn,), F32)]).reshape(1, -1)
        qkv = _matmul("in_proj_qkv", hx, [(w_in_b, 2 * d_gmlp // bn)], [("row", qkv_scale, 0)], _ep_colscale,
                      dot_pairs=((0, 0),), n_out=2 * d_qk + d_attn, out_dtype=BF16, bm=1024, bn=bn)
        gates = _matmul("in_proj_gates", hx, [(w_in_b, (2 * d_gmlp + 2 * d_qk + d_attn) // bn)],
                        [("row", b_gate[layer].reshape(1, -1), 0)], _ep_sigmoid_bias,
                        dot_pairs=((0, 0),), n_out=2 * d_model, out_dtype=BF16, bm=1024, bn=bn)

        y_a = _gmlp(guv, ln_v_g[layer], ln_v_b[layer], w_s[layer], b_s[layer])
        y_b = _diff_attention(qkv, lambda_q1[layer], lambda_k1[layer], lambda_q2[layer],
                              lambda_k2[layer], subln_g[layer], batch=batch, seq=seq,
                              n_heads=n_heads, lam_init=lam_init)

        bn = 1024
        merged_a = _matmul("merge_a", [(y_a, d_gmlp, 0)], [(w_br_a[layer].astype(BF16), 0)],
                           [("tile", gates, 0)], _ep_gated, dot_pairs=((0, 0),),
                           n_out=d_model, out_dtype=F32, bm=1024, bn=bn)
        merged = _matmul("merge_b", [(y_b, d_attn, 0)], [(w_br_b[layer].astype(BF16), 0)],
                         [("tile", gates, d_model // bn), ("tile", merged_a, 0)], _ep_gated_add,
                         dot_pairs=((0, 0),), n_out=d_model, out_dtype=BF16, bm=1024, bn=bn)
        xf, xg, ssq = _matmul("out_proj", [(merged, d_model, 0)], [(w_o[layer].astype(BF16), 0)],
                              [("tile", xf, 0)], _ep_residual, dot_pairs=((0, 0),),
                              n_out=d_model, out_dtype=F32, bm=1024, bn=512,
                              norm_gain=g_ffn[layer].reshape(1, -1))

        act = _swiglu_up(xg, ssq, w_gu[layer], bm=1024, bn=256)
        xf, xg, ssq = _matmul("ffn_down", [(act, d_ff, 0)], [(w_down[layer].astype(BF16), 0)],
                              [("tile", xf, 0)], _ep_residual, dot_pairs=((0, 0),),
                              n_out=d_model, out_dtype=F32, bm=512, bn=512,
                              norm_gain=g_ple[layer].reshape(1, -1))

        ple_bm, ple_bn = (512, 1024) if layer == depth - 1 else (1024, 512)
        xf = _matmul("ple", [(xg, d_model, 0), (p[layer].reshape(t, ple_dim), ple_dim, 0)],
                     [(w_ple_gate[layer].astype(BF16), 0), (w_ple_proj[layer].astype(BF16), 0)],
                     [("tile", xf, 0), ("rowstat", ssq, 0)],
                     functools.partial(_ep_ple_normed, n=d_model), dot_pairs=((0, 0), (1, 1)),
                     n_out=d_model, out_dtype=F32, bm=ple_bm, bn=ple_bn,
                     row_norm_gain=g_final.reshape(1, -1) if layer == depth - 1 else None)

    return xf.reshape(batch, seq, d_model)
```

```python
import functools
import math

import jax
import jax.numpy as jnp
from jax import lax
from jax.experimental import pallas as pl
from jax.experimental.pallas import tpu as pltpu

F32 = jnp.float32
BF16 = jnp.bfloat16

V7X_VMEM_LIMIT_BYTES = 56 * 1024 * 1024

LANES = 128
LOG2_E = math.log2(math.e)

HEAD_DIM = 128
V_HEAD_DIM = 2 * HEAD_DIM
CHUNK = 128
RMS_EPS = 1e-6
LN_EPS = 1e-5
MASK_VALUE = -1e30


def _params(*semantics):
    return pltpu.CompilerParams(dimension_semantics=semantics,
                                vmem_limit_bytes=V7X_VMEM_LIMIT_BYTES)


def _rmsnorm_kernel(x_ref, g_ref, o_ref, *, eps):
    x = x_ref[...]
    ms = jnp.mean(x * x, axis=-1, keepdims=True)
    o_ref[...] = (x * lax.rsqrt(ms + eps) * g_ref[...]).astype(o_ref.dtype)


def _rmsnorm(x, g, out_dtype, *, rows=512):
    t, d = x.shape
    return pl.pallas_call(
        functools.partial(_rmsnorm_kernel, eps=RMS_EPS),
        out_shape=jax.ShapeDtypeStruct((t, d), out_dtype),
        grid=(t // rows,),
        in_specs=[pl.BlockSpec((rows, d), lambda i: (i, 0)),
                  pl.BlockSpec((1, d), lambda i: (0, 0))],
        out_specs=pl.BlockSpec((rows, d), lambda i: (i, 0)),
        compiler_params=_params("parallel"),
        name="rmsnorm",
    )(x, g.reshape(1, d))


def _mm_kernel(*refs, n_x, n_w, n_extra, dot_pairs, epilogue, emit_norm, row_norm, bn):
    x_refs = refs[:n_x]
    w_refs = refs[n_x:n_x + n_w]
    extra_refs = refs[n_x + n_w:n_x + n_w + n_extra]
    out_refs = refs[n_x + n_w + n_extra:]
    accs = [jnp.dot(x_refs[a][...].astype(BF16), w_refs[b][...],
                    preferred_element_type=F32) for a, b in dot_pairs]
    y = epilogue(accs, extra_refs)
    if row_norm:
        o_ref, ssq_ref = out_refs
        gain_ref = extra_refs[-1]
        j = pl.program_id(1)
        o_ref[:, pl.ds(pl.multiple_of(j * bn, bn), bn)] = y
        part = jnp.broadcast_to(jnp.sum(y * y, axis=-1, keepdims=True), ssq_ref.shape)

        @pl.when(j == 0)
        def _():
            ssq_ref[...] = part

        @pl.when(j != 0)
        def _():
            ssq_ref[...] += part

        @pl.when(j == pl.num_programs(1) - 1)
        def _():
            r = _rms_scale(ssq_ref, o_ref.shape[1])
            for c in range(o_ref.shape[1] // LANES):
                cs = slice(c * LANES, (c + 1) * LANES)
                o_ref[:, cs] = o_ref[:, cs] * r * gain_ref[:, cs]
        return
    out_refs[0][...] = y.astype(out_refs[0].dtype)
    if emit_norm:
        gain_ref = extra_refs[-1]
        xg_ref, ssq_ref = out_refs[1:]
        xg_ref[...] = (y * gain_ref[...]).astype(xg_ref.dtype)
        part = jnp.broadcast_to(jnp.sum(y * y, axis=-1, keepdims=True), ssq_ref.shape)
        j = pl.program_id(1)

        @pl.when(j == 0)
        def _():
            ssq_ref[...] = part

        @pl.when(j != 0)
        def _():
            ssq_ref[...] += part


def _matmul(name, xs, ws, extras, epilogue, *, dot_pairs, n_out, out_dtype, bm, bn,
            norm_gain=None, row_norm_gain=None):
    t = xs[0][0].shape[0]
    emit_norm = norm_gain is not None
    row_norm = row_norm_gain is not None
    if emit_norm:
        extras = list(extras) + [("row", norm_gain, 0)]
    if row_norm:
        extras = list(extras) + [("fullrow", row_norm_gain, 0)]
    in_specs, operands = [], []
    for arr, k, cb in xs:
        in_specs.append(pl.BlockSpec((bm, k), lambda i, j, cb=cb: (i, cb)))
        operands.append(arr)
    for arr, cb0 in ws:
        k = arr.shape[0]
        in_specs.append(pl.BlockSpec((k, bn), lambda i, j, cb0=cb0: (0, cb0 + j)))
        operands.append(arr)
    for kind, arr, cb0 in extras:
        if kind == "tile":
            in_specs.append(pl.BlockSpec((bm, bn), lambda i, j, cb0=cb0: (i, cb0 + j)))
        elif kind == "row":
            in_specs.append(pl.BlockSpec((1, bn), lambda i, j, cb0=cb0: (0, cb0 + j)))
        elif kind == "rowstat":
            in_specs.append(pl.BlockSpec((bm, LANES), lambda i, j: (i, 0)))
        else:
            in_specs.append(pl.BlockSpec((1, n_out), lambda i, j: (0, 0)))
        operands.append(arr)
    tile_spec = pl.BlockSpec((bm, bn), lambda i, j: (i, j))
    stat_spec = pl.BlockSpec((bm, LANES), lambda i, j: (i, 0))
    out_shape = jax.ShapeDtypeStruct((t, n_out), out_dtype)
    out_specs = tile_spec
    scratch_shapes = []
    if emit_norm:
        out_shape = (out_shape, jax.ShapeDtypeStruct((t, n_out), BF16),
                     jax.ShapeDtypeStruct((t, LANES), F32))
        out_specs = (tile_spec, tile_spec, stat_spec)
    if row_norm:
        assert out_dtype == F32 and not emit_norm
        out_specs = pl.BlockSpec((bm, n_out), lambda i, j: (i, 0))
        scratch_shapes = [pltpu.VMEM((bm, LANES), F32)]
    sequential_cols = emit_norm or row_norm
    return pl.pallas_call(
        functools.partial(_mm_kernel, n_x=len(xs), n_w=len(ws), n_extra=len(extras),
                          dot_pairs=dot_pairs, epilogue=epilogue, emit_norm=emit_norm,
                          row_norm=row_norm, bn=bn),
        out_shape=out_shape,
        grid=(t // bm, n_out // bn),
        in_specs=in_specs,
        out_specs=out_specs,
        scratch_shapes=scratch_shapes,
        compiler_params=_params("parallel", "arbitrary" if sequential_cols else "parallel"),
        name=name,
    )(*operands)


def _rms_scale(ssq_ref, n):
    return lax.rsqrt(ssq_ref[:, :1] / n + RMS_EPS)


def _gelu_exact(x):
    return 0.5 * x * (1.0 + lax.erf(x * math.sqrt(0.5)))


def _ep_gelu(accs, extras):
    return _gelu_exact(accs[0])


def _ep_colscale(accs, extras):
    return accs[0] * extras[0][...]


def _ep_sigmoid_bias(accs, extras):
    return jax.nn.sigmoid(accs[0] + extras[0][...])


def _ep_gated(accs, extras):
    return extras[0][...].astype(F32) * accs[0]


def _ep_gated_add(accs, extras):
    return extras[1][...] + extras[0][...].astype(F32) * accs[0]


def _ep_residual(accs, extras):
    return extras[0][...] + accs[0]


def _ep_ple_normed(accs, extras, *, n):
    r = _rms_scale(extras[1], n)
    return extras[0][...] + jax.nn.sigmoid(r * accs[0]) * accs[1]


def _swiglu_up_kernel(x_ref, wg_ref, wu_ref, ssq_ref, o_ref, w_ref, *, bn, n):
    @pl.when(pl.program_id(1) == 0)
    def _():
        w_ref[:, :bn] = wg_ref[...].astype(BF16)
        w_ref[:, bn:] = wu_ref[...].astype(BF16)

    acc = jnp.dot(x_ref[...], w_ref[...], preferred_element_type=F32)
    r = _rms_scale(ssq_ref, n)
    o_ref[...] = (jax.nn.silu(r * acc[:, :bn]) * (r * acc[:, bn:])).astype(o_ref.dtype)


def _swiglu_up(xg, ssq, w_gu, *, bm, bn):
    t, d = xg.shape
    d_ff = w_gu.shape[1] // 2
    return pl.pallas_call(
        functools.partial(_swiglu_up_kernel, bn=bn, n=d),
        out_shape=jax.ShapeDtypeStruct((t, d_ff), BF16),
        grid=(d_ff // bn, t // bm),
        in_specs=[pl.BlockSpec((bm, d), lambda j, i: (i, 0)),
                  pl.BlockSpec((d, bn), lambda j, i: (0, j)),
                  pl.BlockSpec((d, bn), lambda j, i: (0, d_ff // bn + j)),
                  pl.BlockSpec((bm, LANES), lambda j, i: (i, 0))],
        out_specs=pl.BlockSpec((bm, bn), lambda j, i: (i, j)),
        scratch_shapes=[pltpu.VMEM((d, 2 * bn), BF16)],
        compiler_params=_params("arbitrary", "arbitrary"),
        name="ffn_up",
    )(xg, w_gu, w_gu, ssq)


def _gmlp_kernel(gu_ref, gv_ref, lng_ref, lnb_ref, ws_ref, bs_ref, o_ref, wm_ref,
                 *, rows, n_groups, group_dim):
    @pl.when(pl.program_id(0) == 0)
    def _():
        r = lax.broadcasted_iota(jnp.int32, (CHUNK, CHUNK), 0)
        c = lax.broadcasted_iota(jnp.int32, (CHUNK, CHUNK), 1)
        for g in range(n_groups):
            wm_ref[g] = jnp.where(c <= r, ws_ref[g], 0.0).astype(BF16)

    for ch in range(rows // CHUNK):
        rs = slice(ch * CHUNK, (ch + 1) * CHUNK)
        v = gv_ref[rs, :].astype(F32)
        mu = jnp.mean(v, axis=-1, keepdims=True)
        vc = v - mu
        var = jnp.mean(vc * vc, axis=-1, keepdims=True)
        vn = (vc * lax.rsqrt(var + LN_EPS) * lng_ref[...] + lnb_ref[...]).astype(BF16)
        for g in range(n_groups):
            cs = slice(g * group_dim, (g + 1) * group_dim)
            mixed = jnp.dot(wm_ref[g], vn[:, cs], preferred_element_type=F32) + bs_ref[g]
            o_ref[rs, cs] = (gu_ref[rs, cs].astype(F32) * mixed).astype(o_ref.dtype)


def _gmlp(guv, ln_g, ln_b, w_s, b_s, *, rows=512):
    t = guv.shape[0]
    d = guv.shape[1] // 2
    n_groups = w_s.shape[0]
    return pl.pallas_call(
        functools.partial(_gmlp_kernel, rows=rows, n_groups=n_groups, group_dim=d // n_groups),
        out_shape=jax.ShapeDtypeStruct((t, d), BF16),
        grid=(t // rows,),
        in_specs=[pl.BlockSpec((rows, d), lambda i: (i, 0)),
                  pl.BlockSpec((rows, d), lambda i: (i, 1)),
                  pl.BlockSpec((1, d), lambda i: (0, 0)),
                  pl.BlockSpec((1, d), lambda i: (0, 0)),
                  pl.BlockSpec((n_groups, CHUNK, CHUNK), lambda i: (0, 0, 0)),
                  pl.BlockSpec((n_groups, CHUNK, 1), lambda i: (0, 0, 0))],
        out_specs=pl.BlockSpec((rows, d), lambda i: (i, 0)),
        scratch_shapes=[pltpu.VMEM((n_groups, CHUNK, CHUNK), BF16)],
        compiler_params=_params("arbitrary"),
        name="gmlp_gating",
    )(guv, guv, ln_g.reshape(1, d), ln_b.reshape(1, d), w_s, b_s.reshape(n_groups, CHUNK, 1))


def _lanes(x, n):
    return jnp.concatenate([x] * (n // LANES), axis=1)


def _attn_kernel(q_ref, k_ref, v_ref, lq1_ref, lk1_ref, lq2_ref, lk2_ref, sg_ref, o_ref,
                 s_ref, p_ref, m_ref, l_ref, a_ref, acc_ref,
                 *, blk, strip, unroll, n_heads, lam_init):
    h = pl.program_id(1)
    i = pl.program_id(2)
    slope = jnp.exp2(-8.0 * jnp.full((1, blk), h + 1, F32) / n_heads) * LOG2_E
    k_idx = lax.broadcasted_iota(jnp.int32, (1, blk), 1).astype(F32)

    m_ref[...] = jnp.full(m_ref.shape, MASK_VALUE, F32)
    l_ref[...] = jnp.zeros(l_ref.shape, F32)
    acc_ref[...] = jnp.zeros(acc_ref.shape, F32)
    p_ref[1] = jnp.zeros(p_ref.shape[1:], BF16)
    a_ref[1] = jnp.ones(a_ref.shape[1:], F32)

    def qk(mp, j):
        off = pl.multiple_of(j * blk, blk)
        cs = slice(mp * HEAD_DIM, (mp + 1) * HEAD_DIM)
        s_ref[mp] = lax.dot_general(q_ref[:, cs], k_ref[pl.ds(off, blk), cs],
                                    (((1,), (1,)), ((), ())),
                                    preferred_element_type=F32)

    def pv(mp, j):
        off = pl.multiple_of(j * blk, blk)
        acc_ref[mp] = (acc_ref[mp] * _lanes(a_ref[mp], V_HEAD_DIM)
                       + jnp.dot(p_ref[mp], v_ref[pl.ds(off, blk), :],
                                 preferred_element_type=F32))

    def softmax_strips(mp, bias, masked):
        for r in range(blk // strip):
            rows = slice(r * strip, (r + 1) * strip)
            s = s_ref[mp, rows, :] + bias
            if masked:
                row = r * strip + lax.broadcasted_iota(jnp.int32, (strip, blk), 0)
                col = lax.broadcasted_iota(jnp.int32, (strip, blk), 1)
                s = jnp.where(col <= row, s, MASK_VALUE)
            m_old = m_ref[mp, rows, :]
            m_new = jnp.maximum(m_old, jnp.max(s, axis=-1, keepdims=True))
            alpha = jnp.exp2(m_old - m_new)
            p = jnp.exp2(s - _lanes(m_new, blk))
            p_sum = p[:, :LANES]
            for c in range(1, blk // LANES):
                p_sum = p_sum + p[:, c * LANES:(c + 1) * LANES]
            l_ref[mp, rows, :] = alpha * l_ref[mp, rows, :] + p_sum
            m_ref[mp, rows, :] = m_new
            a_ref[mp, rows, :] = alpha
            p_ref[mp, rows, :] = p.astype(BF16)

    def step(j, masked):
        bias = slope * (k_idx + ((j - i) * blk).astype(F32))
        pv(1, jnp.maximum(j - 1, 0))
        qk(1, j)
        softmax_strips(0, bias, masked)
        pv(0, j)
        if not masked:
            qk(0, j + 1)
        softmax_strips(1, bias, masked)

    def unrolled_steps(t, carry):
        for u in range(unroll):
            step(unroll * t + u, masked=False)
        return carry

    def single_step(j, carry):
        step(j, masked=False)
        return carry

    qk(0, 0)
    n_trips = i // unroll
    lax.fori_loop(0, n_trips, unrolled_steps, 0)
    lax.fori_loop(n_trips * unroll, i, single_step, 0)
    step(i, masked=True)
    pv(1, i)

    lam = (jnp.exp(jnp.sum(lq1_ref[...] * lk1_ref[...], axis=-1, keepdims=True))
           - jnp.exp(jnp.sum(lq2_ref[...] * lk2_ref[...], axis=-1, keepdims=True))
           + lam_init)
    l1 = jnp.sum(l_ref[0], axis=-1, keepdims=True)
    l2 = jnp.sum(l_ref[1], axis=-1, keepdims=True)
    o = acc_ref[0] / l1 - lam * (acc_ref[1] / l2)
    ms = jnp.mean(o * o, axis=-1, keepdims=True)
    y = (o * lax.rsqrt(ms + LN_EPS) * sg_ref[...]) * (1.0 - lam_init)
    o_ref[...] = y.astype(o_ref.dtype)


def _diff_attention(qkv, lq1, lk1, lq2, lk2, subln_g, *, batch, seq, n_heads, lam_init,
                    blk=512, strip=32, unroll=4):
    t = qkv.shape[0]
    nq = seq // blk
    vec = lambda a: a.reshape(1, -1)
    vec_spec = lambda n: pl.BlockSpec((1, n), lambda b, h, i: (0, 0))
    return pl.pallas_call(
        functools.partial(_attn_kernel, blk=blk, strip=strip, unroll=unroll, n_heads=n_heads,
                          lam_init=lam_init),
        out_shape=jax.ShapeDtypeStruct((t, n_heads * V_HEAD_DIM), BF16),
        grid=(batch, n_heads, nq),
        in_specs=[pl.BlockSpec((blk, 2 * HEAD_DIM), lambda b, h, i: (b * nq + i, h)),
                  pl.BlockSpec((seq, 2 * HEAD_DIM), lambda b, h, i: (b, n_heads + h)),
                  pl.BlockSpec((seq, V_HEAD_DIM), lambda b, h, i: (b, 2 * n_heads + h)),
                  vec_spec(HEAD_DIM), vec_spec(HEAD_DIM), vec_spec(HEAD_DIM), vec_spec(HEAD_DIM),
                  vec_spec(V_HEAD_DIM)],
        out_specs=pl.BlockSpec((blk, V_HEAD_DIM), lambda b, h, i: (b * nq + i, h)),
        scratch_shapes=[pltpu.VMEM((2, blk, blk), F32),
                        pltpu.VMEM((2, blk, blk), BF16),
                        pltpu.VMEM((2, blk, LANES), F32),
                        pltpu.VMEM((2, blk, LANES), F32),
                        pltpu.VMEM((2, blk, LANES), F32),
                        pltpu.VMEM((2, blk, V_HEAD_DIM), F32)],
        compiler_params=_params("parallel", "parallel", "arbitrary"),
        name="diff_attention",
    )(qkv, qkv, qkv, vec(lq1), vec(lk1), vec(lq2), vec(lk2), vec(subln_g))


def kernel(x, p, g_mix, w_in, b_gate, ln_v_g, ln_v_b, w_s, b_s, lambda_q1, lambda_k1, lambda_q2, lambda_k2, subln_g, w_br_a, w_br_b, w_o, g_ffn, w_gu, w_down, g_ple, w_ple_gate, w_ple_proj, g_final):
    batch, seq, d_model = x.shape
    depth = w_in.shape[0]
    t = batch * seq
    d_gmlp = ln_v_g.shape[1]
    d_attn = w_br_b.shape[1]
    n_heads = d_attn // V_HEAD_DIM
    d_qk = n_heads * 2 * HEAD_DIM
    d_ff = w_down.shape[1]
    ple_dim = p.shape[-1]

    xf = x.reshape(t, d_model)
    for layer in range(depth):
        lam_init = 0.8 - 0.6 * math.exp(-0.3 * layer)
        w_in_b = w_in[layer].astype(BF16)

        h = _rmsnorm(xf, g_mix[layer], BF16)
        hx = [(h, d_model, 0)]
        bn = 1024
        guv = _matmul("in_proj_gelu", hx, [(w_in_b, 0)], [], _ep_gelu, dot_pairs=((0, 0),),
                      n_out=2 * d_gmlp, out_dtype=BF16, bm=1024, bn=bn)
        qkv_scale = jnp.concatenate([jnp.full((d_qk,), HEAD_DIM ** -0.5 * LOG2_E, F32),
                                     jnp.ones((d_qk + d_attn,), F32)]).reshape(1, -1)
        qkv = _matmul("in_proj_qkv", hx, [(w_in_b, 2 * d_gmlp // bn)], [("row", qkv_scale, 0)], _ep_colscale,
                      dot_pairs=((0, 0),), n_out=2 * d_qk + d_attn, out_dtype=BF16, bm=1024, bn=bn)
        gates = _matmul("in_proj_gates", hx, [(w_in_b, (2 * d_gmlp + 2 * d_qk + d_attn) // bn)],
                        [("row", b_gate[layer].reshape(1, -1), 0)], _ep_sigmoid_bias,
                        dot_pairs=((0, 0),), n_out=2 * d_model, out_dtype=BF16, bm=1024, bn=bn)

        y_a = _gmlp(guv, ln_v_g[layer], ln_v_b[layer], w_s[layer], b_s[layer])
        y_b = _diff_attention(qkv, lambda_q1[layer], lambda_k1[layer], lambda_q2[layer],
                              lambda_k2[layer], subln_g[layer], batch=batch, seq=seq,
                              n_heads=n_heads, lam_init=lam_init)

        bn = 1024
        merged_a = _matmul("merge_a", [(y_a, d_gmlp, 0)], [(w_br_a[layer].astype(BF16), 0)],
                           [("tile", gates, 0)], _ep_gated, dot_pairs=((0, 0),),
                           n_out=d_model, out_dtype=F32, bm=1024, bn=bn)
        merged = _matmul("merge_b", [(y_b, d_attn, 0)], [(w_br_b[layer].astype(BF16), 0)],
                         [("tile", gates, d_model // bn), ("tile", merged_a, 0)], _ep_gated_add,
                         dot_pairs=((0, 0),), n_out=d_model, out_dtype=BF16, bm=1024, bn=bn)
        xf, xg, ssq = _matmul("out_proj", [(merged, d_model, 0)], [(w_o[layer].astype(BF16), 0)],
                              [("tile", xf, 0)], _ep_residual, dot_pairs=((0, 0),),
                              n_out=d_model, out_dtype=F32, bm=1024, bn=512,
                              norm_gain=g_ffn[layer].reshape(1, -1))

        act = _swiglu_up(xg, ssq, w_gu[layer], bm=1024, bn=256)
        xf, xg, ssq = _matmul("ffn_down", [(act, d_ff, 0)], [(w_down[layer].astype(BF16), 0)],
                              [("tile", xf, 0)], _ep_residual, dot_pairs=((0, 0),),
                              n_out=d_model, out_dtype=F32, bm=512, bn=512,
                              norm_gain=g_ple[layer].reshape(1, -1))

        ple_bm, ple_bn = (512, 1024) if layer == depth - 1 else (1024, 512)
        xf = _matmul("ple", [(xg, d_model, 0), (p[layer].reshape(t, ple_dim), ple_dim, 0)],
                     [(w_ple_gate[layer].astype(BF16), 0), (w_ple_proj[layer].astype(BF16), 0)],
                     [("tile", xf, 0), ("rowstat", ssq, 0)],
                     functools.partial(_ep_ple_normed, n=d_model), dot_pairs=((0, 0), (1, 1)),
                     n_out=d_model, out_dtype=F32, bm=ple_bm, bn=ple_bn,
                     row_norm_gain=g_final.reshape(1, -1) if layer == depth - 1 else None)

    return xf.reshape(batch, seq, d_model)
```

```python
import functools
import math

import jax
import jax.numpy as jnp
from jax import lax
from jax.experimental import pallas as pl
from jax.experimental.pallas import tpu as pltpu

F32 = jnp.float32
BF16 = jnp.bfloat16

V7X_VMEM_LIMIT_BYTES = 56 * 1024 * 1024

LANES = 128
LOG2_E = math.log2(math.e)

HEAD_DIM = 128
V_HEAD_DIM = 2 * HEAD_DIM
CHUNK = 128
RMS_EPS = 1e-6
LN_EPS = 1e-5
MASK_VALUE = -1e30


def _params(*semantics):
    return pltpu.CompilerParams(dimension_semantics=semantics,
                                vmem_limit_bytes=V7X_VMEM_LIMIT_BYTES)


def _rmsnorm_kernel(x_ref, g_ref, o_ref, *, eps):
    x = x_ref[...]
    ms = jnp.mean(x * x, axis=-1, keepdims=True)
    o_ref[...] = (x * lax.rsqrt(ms + eps) * g_ref[...]).astype(o_ref.dtype)


def _rmsnorm(x, g, out_dtype, *, rows=512):
    t, d = x.shape
    return pl.pallas_call(
        functools.partial(_rmsnorm_kernel, eps=RMS_EPS),
        out_shape=jax.ShapeDtypeStruct((t, d), out_dtype),
        grid=(t // rows,),
        in_specs=[pl.BlockSpec((rows, d), lambda i: (i, 0)),
                  pl.BlockSpec((1, d), lambda i: (0, 0))],
        out_specs=pl.BlockSpec((rows, d), lambda i: (i, 0)),
        compiler_params=_params("parallel"),
        name="rmsnorm",
    )(x, g.reshape(1, d))


def _mm_kernel(*refs, n_x, n_w, n_extra, dot_pairs, epilogue, emit_norm, row_norm, bn):
    x_refs = refs[:n_x]
    w_refs = refs[n_x:n_x + n_w]
    extra_refs = refs[n_x + n_w:n_x + n_w + n_extra]
    out_refs = refs[n_x + n_w + n_extra:]
    accs = [jnp.dot(x_refs[a][...].astype(BF16), w_refs[b][...],
                    preferred_element_type=F32) for a, b in dot_pairs]
    y = epilogue(accs, extra_refs)
    if row_norm:
        o_ref, ssq_ref = out_refs
        gain_ref = extra_refs[-1]
        j = pl.program_id(1)
        o_ref[:, pl.ds(pl.multiple_of(j * bn, bn), bn)] = y
        part = jnp.broadcast_to(jnp.sum(y * y, axis=-1, keepdims=True), ssq_ref.shape)

        @pl.when(j == 0)
        def _():
            ssq_ref[...] = part

        @pl.when(j != 0)
        def _():
            ssq_ref[...] += part

        @pl.when(j == pl.num_programs(1) - 1)
        def _():
            r = _rms_scale(ssq_ref, o_ref.shape[1])
            for c in range(o_ref.shape[1] // LANES):
                cs = slice(c * LANES, (c + 1) * LANES)
                o_ref[:, cs] = o_ref[:, cs] * r * gain_ref[:, cs]
        return
    out_refs[0][...] = y.astype(out_refs[0].dtype)
    if emit_norm:
        gain_ref = extra_refs[-1]
        xg_ref, ssq_ref = out_refs[1:]
        xg_ref[...] = (y * gain_ref[...]).astype(xg_ref.dtype)
        part = jnp.broadcast_to(jnp.sum(y * y, axis=-1, keepdims=True), ssq_ref.shape)
        j = pl.program_id(1)

        @pl.when(j == 0)
        def _():
            ssq_ref[...] = part

        @pl.when(j != 0)
        def _():
            ssq_ref[...] += part


def _matmul(name, xs, ws, extras, epilogue, *, dot_pairs, n_out, out_dtype, bm, bn,
            norm_gain=None, row_norm_gain=None):
    t = xs[0][0].shape[0]
    emit_norm = norm_gain is not None
    row_norm = row_norm_gain is not None
    if emit_norm:
        extras = list(extras) + [("row", norm_gain, 0)]
    if row_norm:
        extras = list(extras) + [("fullrow", row_norm_gain, 0)]
    in_specs, operands = [], []
    for arr, k, cb in xs:
        in_specs.append(pl.BlockSpec((bm, k), lambda i, j, cb=cb: (i, cb)))
        operands.append(arr)
    for arr, cb0 in ws:
        k = arr.shape[0]
        in_specs.append(pl.BlockSpec((k, bn), lambda i, j, cb0=cb0: (0, cb0 + j)))
        operands.append(arr)
    for kind, arr, cb0 in extras:
        if kind == "tile":
            in_specs.append(pl.BlockSpec((bm, bn), lambda i, j, cb0=cb0: (i, cb0 + j)))
        elif kind == "row":
            in_specs.append(pl.BlockSpec((1, bn), lambda i, j, cb0=cb0: (0, cb0 + j)))
        elif kind == "rowstat":
            in_specs.append(pl.BlockSpec((bm, LANES), lambda i, j: (i, 0)))
        else:
            in_specs.append(pl.BlockSpec((1, n_out), lambda i, j: (0, 0)))
        operands.append(arr)
    tile_spec = pl.BlockSpec((bm, bn), lambda i, j: (i, j))
    stat_spec = pl.BlockSpec((bm, LANES), lambda i, j: (i, 0))
    out_shape = jax.ShapeDtypeStruct((t, n_out), out_dtype)
    out_specs = tile_spec
    scratch_shapes = []
    if emit_norm:
        out_shape = (out_shape, jax.ShapeDtypeStruct((t, n_out), BF16),
                     jax.ShapeDtypeStruct((t, LANES), F32))
        out_specs = (tile_spec, tile_spec, stat_spec)
    if row_norm:
        assert out_dtype == F32 and not emit_norm
        out_specs = pl.BlockSpec((bm, n_out), lambda i, j: (i, 0))
        scratch_shapes = [pltpu.VMEM((bm, LANES), F32)]
    sequential_cols = emit_norm or row_norm
    return pl.pallas_call(
        functools.partial(_mm_kernel, n_x=len(xs), n_w=len(ws), n_extra=len(extras),
                          dot_pairs=dot_pairs, epilogue=epilogue, emit_norm=emit_norm,
                          row_norm=row_norm, bn=bn),
        out_shape=out_shape,
        grid=(t // bm, n_out // bn),
        in_specs=in_specs,
        out_specs=out_specs,
        scratch_shapes=scratch_shapes,
        compiler_params=_params("parallel", "arbitrary" if sequential_cols else "parallel"),
        name=name,
    )(*operands)


def _rms_scale(ssq_ref, n):
    return lax.rsqrt(ssq_ref[:, :1] / n + RMS_EPS)


def _gelu_exact(x):
    return 0.5 * x * (1.0 + lax.erf(x * math.sqrt(0.5)))


def _ep_gelu(accs, extras):
    return _gelu_exact(accs[0])


def _ep_colscale(accs, extras):
    return accs[0] * extras[0][...]


def _ep_sigmoid_bias(accs, extras):
    return jax.nn.sigmoid(accs[0] + extras[0][...])


def _ep_gated(accs, extras):
    return extras[0][...].astype(F32) * accs[0]


def _ep_gated_add(accs, extras):
    return extras[1][...] + extras[0][...].astype(F32) * accs[0]


def _ep_residual(accs, extras):
    return extras[0][...] + accs[0]


def _ep_ple_normed(accs, extras, *, n):
    r = _rms_scale(extras[1], n)
    return extras[0][...] + jax.nn.sigmoid(r * accs[0]) * accs[1]


def _swiglu_up_kernel(x_ref, wg_ref, wu_ref, ssq_ref, o_ref, w_ref, *, bn, n):
    @pl.when(pl.program_id(1) == 0)
    def _():
        w_ref[:, :bn] = wg_ref[...].astype(BF16)
        w_ref[:, bn:] = wu_ref[...].astype(BF16)

    acc = jnp.dot(x_ref[...], w_ref[...], preferred_element_type=F32)
    r = _rms_scale(ssq_ref, n)
    o_ref[...] = (jax.nn.silu(r * acc[:, :bn]) * (r * acc[:, bn:])).astype(o_ref.dtype)


def _swiglu_up(xg, ssq, w_gu, *, bm, bn):
    t, d = xg.shape
    d_ff = w_gu.shape[1] // 2
    return pl.pallas_call(
        functools.partial(_swiglu_up_kernel, bn=bn, n=d),
        out_shape=jax.ShapeDtypeStruct((t, d_ff), BF16),
        grid=(d_ff // bn, t // bm),
        in_specs=[pl.BlockSpec((bm, d), lambda j, i: (i, 0)),
                  pl.BlockSpec((d, bn), lambda j, i: (0, j)),
                  pl.BlockSpec((d, bn), lambda j, i: (0, d_ff // bn + j)),
                  pl.BlockSpec((bm, LANES), lambda j, i: (i, 0))],
        out_specs=pl.BlockSpec((bm, bn), lambda j, i: (i, j)),
        scratch_shapes=[pltpu.VMEM((d, 2 * bn), BF16)],
        compiler_params=_params("arbitrary", "arbitrary"),
        name="ffn_up",
    )(xg, w_gu, w_gu, ssq)


def _gmlp_kernel(gu_ref, gv_ref, lng_ref, lnb_ref, ws_ref, bs_ref, o_ref, wm_ref,
                 *, rows, n_groups, group_dim):
    @pl.when(pl.program_id(0) == 0)
    def _():
        r = lax.broadcasted_iota(jnp.int32, (CHUNK, CHUNK), 0)
        c = lax.broadcasted_iota(jnp.int32, (CHUNK, CHUNK), 1)
        for g in range(n_groups):
            wm_ref[g] = jnp.where(c <= r, ws_ref[g], 0.0).astype(BF16)

    for ch in range(rows // CHUNK):
        rs = slice(ch * CHUNK, (ch + 1) * CHUNK)
        v = gv_ref[rs, :].astype(F32)
        mu = jnp.mean(v, axis=-1, keepdims=True)
        vc = v - mu
        var = jnp.mean(vc * vc, axis=-1, keepdims=True)
        vn = (vc * lax.rsqrt(var + LN_EPS) * lng_ref[...] + lnb_ref[...]).astype(BF16)
        for g in range(n_groups):
            cs = slice(g * group_dim, (g + 1) * group_dim)
            mixed = jnp.dot(wm_ref[g], vn[:, cs], preferred_element_type=F32) + bs_ref[g]
            o_ref[rs, cs] = (gu_ref[rs, cs].astype(F32) * mixed).astype(o_ref.dtype)


def _gmlp(guv, ln_g, ln_b, w_s, b_s, *, rows=512):
    t = guv.shape[0]
    d = guv.shape[1] // 2
    n_groups = w_s.shape[0]
    return pl.pallas_call(
        functools.partial(_gmlp_kernel, rows=rows, n_groups=n_groups, group_dim=d // n_groups),
        out_shape=jax.ShapeDtypeStruct((t, d), BF16),
        grid=(t // rows,),
        in_specs=[pl.BlockSpec((rows, d), lambda i: (i, 0)),
                  pl.BlockSpec((rows, d), lambda i: (i, 1)),
                  pl.BlockSpec((1, d), lambda i: (0, 0)),
                  pl.BlockSpec((1, d), lambda i: (0, 0)),
                  pl.BlockSpec((n_groups, CHUNK, CHUNK), lambda i: (0, 0, 0)),
                  pl.BlockSpec((n_groups, CHUNK, 1), lambda i: (0, 0, 0))],
        out_specs=pl.BlockSpec((rows, d), lambda i: (i, 0)),
        scratch_shapes=[pltpu.VMEM((n_groups, CHUNK, CHUNK), BF16)],
        compiler_params=_params("arbitrary"),
        name="gmlp_gating",
    )(guv, guv, ln_g.reshape(1, d), ln_b.reshape(1, d), w_s, b_s.reshape(n_groups, CHUNK, 1))


def _lanes(x, n):
    return jnp.concatenate([x] * (n // LANES), axis=1)


def _attn_kernel(q_ref, k_ref, v_ref, lq1_ref, lk1_ref, lq2_ref, lk2_ref, sg_ref, o_ref,
                 s_ref, p_ref, m_ref, l_ref, a_ref, acc_ref,
                 *, blk, strip, n_q_blocks, n_heads, lam_init):
    h = pl.program_id(1)
    i = pl.program_id(2)
    slope = jnp.exp2(-8.0 * jnp.full((1, blk), h + 1, F32) / n_heads) * LOG2_E
    k_idx = lax.broadcasted_iota(jnp.int32, (1, blk), 1).astype(F32)

    m_ref[...] = jnp.full(m_ref.shape, MASK_VALUE, F32)
    l_ref[...] = jnp.zeros(l_ref.shape, F32)
    acc_ref[...] = jnp.zeros(acc_ref.shape, F32)
    p_ref[1] = jnp.zeros(p_ref.shape[1:], BF16)
    a_ref[1] = jnp.ones(a_ref.shape[1:], F32)

    def qk(mp, j):
        off = pl.multiple_of(j * blk, blk)
        cs = slice(mp * HEAD_DIM, (mp + 1) * HEAD_DIM)
        s_ref[mp] = lax.dot_general(q_ref[:, cs], k_ref[pl.ds(off, blk), cs],
                                    (((1,), (1,)), ((), ())),
                                    preferred_element_type=F32)

    def pv(mp, j):
        off = pl.multiple_of(j * blk, blk)
        acc_ref[mp] = (acc_ref[mp] * _lanes(a_ref[mp], V_HEAD_DIM)
                       + jnp.dot(p_ref[mp], v_ref[pl.ds(off, blk), :],
                                 preferred_element_type=F32))

    def softmax_strips(mp, bias, masked):
        for r in range(blk // strip):
            rows = slice(r * strip, (r + 1) * strip)
            s = s_ref[mp, rows, :] + bias
            if masked:
                row = r * strip + lax.broadcasted_iota(jnp.int32, (strip, blk), 0)
                col = lax.broadcasted_iota(jnp.int32, (strip, blk), 1)
                s = jnp.where(col <= row, s, MASK_VALUE)
            m_old = m_ref[mp, rows, :]
            m_new = jnp.maximum(m_old, jnp.max(s, axis=-1, keepdims=True))
            alpha = jnp.exp2(m_old - m_new)
            p = jnp.exp2(s - _lanes(m_new, blk))
            p_sum = p[:, :LANES]
            for c in range(1, blk // LANES):
                p_sum = p_sum + p[:, c * LANES:(c + 1) * LANES]
            l_ref[mp, rows, :] = alpha * l_ref[mp, rows, :] + p_sum
            m_ref[mp, rows, :] = m_new
            a_ref[mp, rows, :] = alpha
            p_ref[mp, rows, :] = p.astype(BF16)

    def step(j, masked):
        bias = slope * (k_idx + ((j - i) * blk).astype(F32))
        pv(1, jnp.maximum(j - 1, 0))
        qk(1, j)
        softmax_strips(0, bias, masked)
        pv(0, j)
        if not masked:
            qk(0, j + 1)
        softmax_strips(1, bias, masked)

    qk(0, 0)
    run = n_q_blocks // 2
    while run >= 1:
        @pl.when(lax.bitwise_and(i, run) != 0)
        def _(run=run):
            start = lax.bitwise_and(i, -2 * run)
            for u in range(run):
                step(start + u, masked=False)
        run //= 2
    step(i, masked=True)
    pv(1, i)

    lam = (jnp.exp(jnp.sum(lq1_ref[...] * lk1_ref[...], axis=-1, keepdims=True))
           - jnp.exp(jnp.sum(lq2_ref[...] * lk2_ref[...], axis=-1, keepdims=True))
           + lam_init)
    l1 = jnp.sum(l_ref[0], axis=-1, keepdims=True)
    l2 = jnp.sum(l_ref[1], axis=-1, keepdims=True)
    o = acc_ref[0] / l1 - lam * (acc_ref[1] / l2)
    ms = jnp.mean(o * o, axis=-1, keepdims=True)
    y = (o * lax.rsqrt(ms + LN_EPS) * sg_ref[...]) * (1.0 - lam_init)
    o_ref[...] = y.astype(o_ref.dtype)


def _diff_attention(qkv, lq1, lk1, lq2, lk2, subln_g, *, batch, seq, n_heads, lam_init,
                    blk=512, strip=32):
    t = qkv.shape[0]
    nq = seq // blk
    assert nq & (nq - 1) == 0, "the key-block walk assumes a power-of-two block count"
    vec = lambda a: a.reshape(1, -1)
    vec_spec = lambda n: pl.BlockSpec((1, n), lambda b, h, i: (0, 0))
    return pl.pallas_call(
        functools.partial(_attn_kernel, blk=blk, strip=strip, n_q_blocks=nq, n_heads=n_heads,
                          lam_init=lam_init),
        out_shape=jax.ShapeDtypeStruct((t, n_heads * V_HEAD_DIM), BF16),
        grid=(batch, n_heads, nq),
        in_specs=[pl.BlockSpec((blk, 2 * HEAD_DIM), lambda b, h, i: (b * nq + i, h)),
                  pl.BlockSpec((seq, 2 * HEAD_DIM), lambda b, h, i: (b, n_heads + h)),
                  pl.BlockSpec((seq, V_HEAD_DIM), lambda b, h, i: (b, 2 * n_heads + h)),
                  vec_spec(HEAD_DIM), vec_spec(HEAD_DIM), vec_spec(HEAD_DIM), vec_spec(HEAD_DIM),
                  vec_spec(V_HEAD_DIM)],
        out_specs=pl.BlockSpec((blk, V_HEAD_DIM), lambda b, h, i: (b * nq + i, h)),
        scratch_shapes=[pltpu.VMEM((2, blk, blk), F32),
                        pltpu.VMEM((2, blk, blk), BF16),
                        pltpu.VMEM((2, blk, LANES), F32),
                        pltpu.VMEM((2, blk, LANES), F32),
                        pltpu.VMEM((2, blk, LANES), F32),
                        pltpu.VMEM((2, blk, V_HEAD_DIM), F32)],
        compiler_params=_params("parallel", "parallel", "arbitrary"),
        name="diff_attention",
    )(qkv, qkv, qkv, vec(lq1), vec(lk1), vec(lq2), vec(lk2), vec(subln_g))


def kernel(x, p, g_mix, w_in, b_gate, ln_v_g, ln_v_b, w_s, b_s, lambda_q1, lambda_k1, lambda_q2, lambda_k2, subln_g, w_br_a, w_br_b, w_o, g_ffn, w_gu, w_down, g_ple, w_ple_gate, w_ple_proj, g_final):
    batch, seq, d_model = x.shape
    depth = w_in.shape[0]
    t = batch * seq
    d_gmlp = ln_v_g.shape[1]
    d_attn = w_br_b.shape[1]
    n_heads = d_attn // V_HEAD_DIM
    d_qk = n_heads * 2 * HEAD_DIM
    d_ff = w_down.shape[1]
    ple_dim = p.shape[-1]

    xf = x.reshape(t, d_model)
    for layer in range(depth):
        lam_init = 0.8 - 0.6 * math.exp(-0.3 * layer)
        w_in_b = w_in[layer].astype(BF16)

        h = _rmsnorm(xf, g_mix[layer], BF16)
        hx = [(h, d_model, 0)]
        bn = 1024
        guv = _matmul("in_proj_gelu", hx, [(w_in_b, 0)], [], _ep_gelu, dot_pairs=((0, 0),),
                      n_out=2 * d_gmlp, out_dtype=BF16, bm=1024, bn=bn)
        qkv_scale = jnp.concatenate([jnp.full((d_qk,), HEAD_DIM ** -0.5 * LOG2_E, F32),
                                     jnp.ones((d_qk + d_attn,), F32)]).reshape(1, -1)
        qkv = _matmul("in_proj_qkv", hx, [(w_in_b, 2 * d_gmlp // bn)], [("row", qkv_scale, 0)], _ep_colscale,
                      dot_pairs=((0, 0),), n_out=2 * d_qk + d_attn, out_dtype=BF16, bm=1024, bn=bn)
        gates = _matmul("in_proj_gates", hx, [(w_in_b, (2 * d_gmlp + 2 * d_qk + d_attn) // bn)],
                        [("row", b_gate[layer].reshape(1, -1), 0)], _ep_sigmoid_bias,
                        dot_pairs=((0, 0),), n_out=2 * d_model, out_dtype=BF16, bm=1024, bn=bn)

        y_a = _gmlp(guv, ln_v_g[layer], ln_v_b[layer], w_s[layer], b_s[layer])
        y_b = _diff_attention(qkv, lambda_q1[layer], lambda_k1[layer], lambda_q2[layer],
                              lambda_k2[layer], subln_g[layer], batch=batch, seq=seq,
                              n_heads=n_heads, lam_init=lam_init)

        bn = 1024
        merged_a = _matmul("merge_a", [(y_a, d_gmlp, 0)], [(w_br_a[layer].astype(BF16), 0)],
                           [("tile", gates, 0)], _ep_gated, dot_pairs=((0, 0),),
                           n_out=d_model, out_dtype=F32, bm=1024, bn=bn)
        merged = _matmul("merge_b", [(y_b, d_attn, 0)], [(w_br_b[layer].astype(BF16), 0)],
                         [("tile", gates, d_model // bn), ("tile", merged_a, 0)], _ep_gated_add,
                         dot_pairs=((0, 0),), n_out=d_model, out_dtype=BF16, bm=1024, bn=bn)
        xf, xg, ssq = _matmul("out_proj", [(merged, d_model, 0)], [(w_o[layer].astype(BF16), 0)],
                              [("tile", xf, 0)], _ep_residual, dot_pairs=((0, 0),),
                              n_out=d_model, out_dtype=F32, bm=1024, bn=512,
                              norm_gain=g_ffn[layer].reshape(1, -1))

        act = _swiglu_up(xg, ssq, w_gu[layer], bm=1024, bn=256)
        xf, xg, ssq = _matmul("ffn_down", [(act, d_ff, 0)], [(w_down[layer].astype(BF16), 0)],
                              [("tile", xf, 0)], _ep_residual, dot_pairs=((0, 0),),
                              n_out=d_model, out_dtype=F32, bm=512, bn=512,
                              norm_gain=g_ple[layer].reshape(1, -1))

        ple_bm, ple_bn = (512, 1024) if layer == depth - 1 else (1024, 512)
        xf = _matmul("ple", [(xg, d_model, 0), (p[layer].reshape(t, ple_dim), ple_dim, 0)],
                     [(w_ple_gate[layer].astype(BF16), 0), (w_ple_proj[layer].astype(BF16), 0)],
                     [("tile", xf, 0), ("rowstat", ssq, 0)],
                     functools.partial(_ep_ple_normed, n=d_model), dot_pairs=((0, 0), (1, 1)),
                     n_out=d_model, out_dtype=F32, bm=ple_bm, bn=ple_bn,
                     row_norm_gain=g_final.reshape(1, -1) if layer == depth - 1 else None)

    return xf.reshape(batch, seq, d_model)
```

```python
import functools
import math

import jax
import jax.numpy as jnp
from jax import lax
from jax.experimental import pallas as pl
from jax.experimental.pallas import tpu as pltpu

F32 = jnp.float32
BF16 = jnp.bfloat16

V7X_VMEM_LIMIT_BYTES = 56 * 1024 * 1024

LANES = 128
LOG2_E = math.log2(math.e)

HEAD_DIM = 128
V_HEAD_DIM = 2 * HEAD_DIM
CHUNK = 128
RMS_EPS = 1e-6
LN_EPS = 1e-5
MASK_VALUE = -1e30


def _params(*semantics):
    return pltpu.CompilerParams(dimension_semantics=semantics,
                                vmem_limit_bytes=V7X_VMEM_LIMIT_BYTES)


def _rmsnorm_kernel(x_ref, g_ref, o_ref, *, eps):
    x = x_ref[...]
    ms = jnp.mean(x * x, axis=-1, keepdims=True)
    o_ref[...] = (x * lax.rsqrt(ms + eps) * g_ref[...]).astype(o_ref.dtype)


def _rmsnorm(x, g, out_dtype, *, rows=512):
    t, d = x.shape
    return pl.pallas_call(
        functools.partial(_rmsnorm_kernel, eps=RMS_EPS),
        out_shape=jax.ShapeDtypeStruct((t, d), out_dtype),
        grid=(t // rows,),
        in_specs=[pl.BlockSpec((rows, d), lambda i: (i, 0)),
                  pl.BlockSpec((1, d), lambda i: (0, 0))],
        out_specs=pl.BlockSpec((rows, d), lambda i: (i, 0)),
        compiler_params=_params("parallel"),
        name="rmsnorm",
    )(x, g.reshape(1, d))


def _mm_kernel(*refs, n_x, n_w, n_extra, dot_pairs, epilogue, emit_norm, row_norm, bn):
    x_refs = refs[:n_x]
    w_refs = refs[n_x:n_x + n_w]
    extra_refs = refs[n_x + n_w:n_x + n_w + n_extra]
    out_refs = refs[n_x + n_w + n_extra:]
    accs = [jnp.dot(x_refs[a][...].astype(BF16), w_refs[b][...],
                    preferred_element_type=F32) for a, b in dot_pairs]
    y = epilogue(accs, extra_refs)
    if row_norm:
        o_ref, ssq_ref = out_refs
        gain_ref = extra_refs[-1]
        j = pl.program_id(1)
        o_ref[:, pl.ds(pl.multiple_of(j * bn, bn), bn)] = y
        part = jnp.broadcast_to(jnp.sum(y * y, axis=-1, keepdims=True), ssq_ref.shape)

        @pl.when(j == 0)
        def _():
            ssq_ref[...] = part

        @pl.when(j != 0)
        def _():
            ssq_ref[...] += part

        @pl.when(j == pl.num_programs(1) - 1)
        def _():
            r = _rms_scale(ssq_ref, o_ref.shape[1])
            for c in range(o_ref.shape[1] // LANES):
                cs = slice(c * LANES, (c + 1) * LANES)
                o_ref[:, cs] = o_ref[:, cs] * r * gain_ref[:, cs]
        return
    out_refs[0][...] = y.astype(out_refs[0].dtype)
    if emit_norm:
        gain_ref = extra_refs[-1]
        xg_ref, ssq_ref = out_refs[1:]
        xg_ref[...] = (y * gain_ref[...]).astype(xg_ref.dtype)
        part = jnp.broadcast_to(jnp.sum(y * y, axis=-1, keepdims=True), ssq_ref.shape)
        j = pl.program_id(1)

        @pl.when(j == 0)
        def _():
            ssq_ref[...] = part

        @pl.when(j != 0)
        def _():
            ssq_ref[...] += part


def _matmul(name, xs, ws, extras, epilogue, *, dot_pairs, n_out, out_dtype, bm, bn,
            norm_gain=None, row_norm_gain=None):
    t = xs[0][0].shape[0]
    emit_norm = norm_gain is not None
    row_norm = row_norm_gain is not None
    if emit_norm:
        extras = list(extras) + [("row", norm_gain, 0)]
    if row_norm:
        extras = list(extras) + [("fullrow", row_norm_gain, 0)]
    in_specs, operands = [], []
    for arr, k, cb in xs:
        in_specs.append(pl.BlockSpec((bm, k), lambda i, j, cb=cb: (i, cb)))
        operands.append(arr)
    for arr, cb0 in ws:
        k = arr.shape[0]
        in_specs.append(pl.BlockSpec((k, bn), lambda i, j, cb0=cb0: (0, cb0 + j)))
        operands.append(arr)
    for kind, arr, cb0 in extras:
        if kind == "tile":
            in_specs.append(pl.BlockSpec((bm, bn), lambda i, j, cb0=cb0: (i, cb0 + j)))
        elif kind == "row":
            in_specs.append(pl.BlockSpec((1, bn), lambda i, j, cb0=cb0: (0, cb0 + j)))
        elif kind == "rowstat":
            in_specs.append(pl.BlockSpec((bm, LANES), lambda i, j: (i, 0)))
        else:
            in_specs.append(pl.BlockSpec((1, n_out), lambda i, j: (0, 0)))
        operands.append(arr)
    tile_spec = pl.BlockSpec((bm, bn), lambda i, j: (i, j))
    stat_spec = pl.BlockSpec((bm, LANES), lambda i, j: (i, 0))
    out_shape = jax.ShapeDtypeStruct((t, n_out), out_dtype)
    out_specs = tile_spec
    scratch_shapes = []
    if emit_norm:
        out_shape = (out_shape, jax.ShapeDtypeStruct((t, n_out), BF16),
                     jax.ShapeDtypeStruct((t, LANES), F32))
        out_specs = (tile_spec, tile_spec, stat_spec)
    if row_norm:
        assert out_dtype == F32 and not emit_norm
        out_specs = pl.BlockSpec((bm, n_out), lambda i, j: (i, 0))
        scratch_shapes = [pltpu.VMEM((bm, LANES), F32)]
    sequential_cols = emit_norm or row_norm
    return pl.pallas_call(
        functools.partial(_mm_kernel, n_x=len(xs), n_w=len(ws), n_extra=len(extras),
                          dot_pairs=dot_pairs, epilogue=epilogue, emit_norm=emit_norm,
                          row_norm=row_norm, bn=bn),
        out_shape=out_shape,
        grid=(t // bm, n_out // bn),
        in_specs=in_specs,
        out_specs=out_specs,
        scratch_shapes=scratch_shapes,
        compiler_params=_params("parallel", "arbitrary" if sequential_cols else "parallel"),
        name=name,
    )(*operands)


def _rms_scale(ssq_ref, n):
    return lax.rsqrt(ssq_ref[:, :1] / n + RMS_EPS)


def _gelu_exact(x):
    return 0.5 * x * (1.0 + lax.erf(x * math.sqrt(0.5)))


def _ep_gelu(accs, extras):
    return _gelu_exact(accs[0])


def _ep_colscale(accs, extras):
    return accs[0] * extras[0][...]


def _ep_sigmoid_bias(accs, extras):
    return jax.nn.sigmoid(accs[0] + extras[0][...])


def _ep_gated(accs, extras):
    return extras[0][...].astype(F32) * accs[0]


def _ep_gated_add(accs, extras):
    return extras[1][...] + extras[0][...].astype(F32) * accs[0]


def _ep_residual(accs, extras):
    return extras[0][...] + accs[0]


def _ep_ple_normed(accs, extras, *, n):
    r = _rms_scale(extras[1], n)
    return extras[0][...] + jax.nn.sigmoid(r * accs[0]) * accs[1]


def _swiglu_up_kernel(x_ref, wg_ref, wu_ref, ssq_ref, o_ref, w_ref, *, bn, n):
    @pl.when(pl.program_id(1) == 0)
    def _():
        w_ref[:, :bn] = wg_ref[...].astype(BF16)
        w_ref[:, bn:] = wu_ref[...].astype(BF16)

    acc = jnp.dot(x_ref[...], w_ref[...], preferred_element_type=F32)
    r = _rms_scale(ssq_ref, n)
    o_ref[...] = (jax.nn.silu(r * acc[:, :bn]) * (r * acc[:, bn:])).astype(o_ref.dtype)


def _swiglu_up(xg, ssq, w_gu, *, bm, bn):
    t, d = xg.shape
    d_ff = w_gu.shape[1] // 2
    return pl.pallas_call(
        functools.partial(_swiglu_up_kernel, bn=bn, n=d),
        out_shape=jax.ShapeDtypeStruct((t, d_ff), BF16),
        grid=(d_ff // bn, t // bm),
        in_specs=[pl.BlockSpec((bm, d), lambda j, i: (i, 0)),
                  pl.BlockSpec((d, bn), lambda j, i: (0, j)),
                  pl.BlockSpec((d, bn), lambda j, i: (0, d_ff // bn + j)),
                  pl.BlockSpec((bm, LANES), lambda j, i: (i, 0))],
        out_specs=pl.BlockSpec((bm, bn), lambda j, i: (i, j)),
        scratch_shapes=[pltpu.VMEM((d, 2 * bn), BF16)],
        compiler_params=_params("arbitrary", "arbitrary"),
        name="ffn_up",
    )(xg, w_gu, w_gu, ssq)


def _gmlp_kernel(gu_ref, gv_ref, lng_ref, lnb_ref, ws_ref, bs_ref, o_ref, wm_ref,
                 *, rows, n_groups, group_dim):
    @pl.when(pl.program_id(0) == 0)
    def _():
        r = lax.broadcasted_iota(jnp.int32, (CHUNK, CHUNK), 0)
        c = lax.broadcasted_iota(jnp.int32, (CHUNK, CHUNK), 1)
        for g in range(n_groups):
            wm_ref[g] = jnp.where(c <= r, ws_ref[g], 0.0).astype(BF16)

    for ch in range(rows // CHUNK):
        rs = slice(ch * CHUNK, (ch + 1) * CHUNK)
        v = gv_ref[rs, :].astype(F32)
        mu = jnp.mean(v, axis=-1, keepdims=True)
        vc = v - mu
        var = jnp.mean(vc * vc, axis=-1, keepdims=True)
        vn = (vc * lax.rsqrt(var + LN_EPS) * lng_ref[...] + lnb_ref[...]).astype(BF16)
        for g in range(n_groups):
            cs = slice(g * group_dim, (g + 1) * group_dim)
            mixed = jnp.dot(wm_ref[g], vn[:, cs], preferred_element_type=F32) + bs_ref[g]
            o_ref[rs, cs] = (gu_ref[rs, cs].astype(F32) * mixed).astype(o_ref.dtype)


def _gmlp(guv, ln_g, ln_b, w_s, b_s, *, rows=512):
    t = guv.shape[0]
    d = guv.shape[1] // 2
    n_groups = w_s.shape[0]
    return pl.pallas_call(
        functools.partial(_gmlp_kernel, rows=rows, n_groups=n_groups, group_dim=d // n_groups),
        out_shape=jax.ShapeDtypeStruct((t, d), BF16),
        grid=(t // rows,),
        in_specs=[pl.BlockSpec((rows, d), lambda i: (i, 0)),
                  pl.BlockSpec((rows, d), lambda i: (i, 1)),
                  pl.BlockSpec((1, d), lambda i: (0, 0)),
                  pl.BlockSpec((1, d), lambda i: (0, 0)),
                  pl.BlockSpec((n_groups, CHUNK, CHUNK), lambda i: (0, 0, 0)),
                  pl.BlockSpec((n_groups, CHUNK, 1), lambda i: (0, 0, 0))],
        out_specs=pl.BlockSpec((rows, d), lambda i: (i, 0)),
        scratch_shapes=[pltpu.VMEM((n_groups, CHUNK, CHUNK), BF16)],
        compiler_params=_params("arbitrary"),
        name="gmlp_gating",
    )(guv, guv, ln_g.reshape(1, d), ln_b.reshape(1, d), w_s, b_s.reshape(n_groups, CHUNK, 1))


def _lanes(x, n):
    return jnp.concatenate([x] * (n // LANES), axis=1)


def _attn_kernel(q_ref, k_ref, v_ref, lq1_ref, lk1_ref, lq2_ref, lk2_ref, sg_ref, o_ref,
                 s_ref, p_ref, m_ref, l_ref, a_ref, acc_ref,
                 *, blk, strip, n_q_blocks, n_heads, lam_init):
    h = pl.program_id(1)
    slope = jnp.exp2(-8.0 * jnp.full((1, blk), h + 1, F32) / n_heads) * LOG2_E
    k_idx = lax.broadcasted_iota(jnp.int32, (1, blk), 1).astype(F32)
    lam = (jnp.exp(jnp.sum(lq1_ref[...] * lk1_ref[...], axis=-1, keepdims=True))
           - jnp.exp(jnp.sum(lq2_ref[...] * lk2_ref[...], axis=-1, keepdims=True))
           + lam_init)

    def query_block(i, carry):
        _attn_query_block(i, slope, k_idx, lam, q_ref, k_ref, v_ref, sg_ref, o_ref,
                          s_ref, p_ref, m_ref, l_ref, a_ref, acc_ref,
                          blk=blk, strip=strip, n_q_blocks=n_q_blocks, lam_init=lam_init)
        return carry

    lax.fori_loop(0, n_q_blocks, query_block, 0)


def _attn_query_block(i, slope, k_idx, lam, q_ref, k_ref, v_ref, sg_ref, o_ref,
                      s_ref, p_ref, m_ref, l_ref, a_ref, acc_ref,
                      *, blk, strip, n_q_blocks, lam_init):
    q_rows = pl.ds(pl.multiple_of(i * blk, blk), blk)

    m_ref[...] = jnp.full(m_ref.shape, MASK_VALUE, F32)
    l_ref[...] = jnp.zeros(l_ref.shape, F32)
    acc_ref[...] = jnp.zeros(acc_ref.shape, F32)
    p_ref[1] = jnp.zeros(p_ref.shape[1:], BF16)
    a_ref[1] = jnp.ones(a_ref.shape[1:], F32)

    def qk(mp, j):
        off = pl.multiple_of(j * blk, blk)
        cs = slice(mp * HEAD_DIM, (mp + 1) * HEAD_DIM)
        s_ref[mp] = lax.dot_general(q_ref[q_rows, cs], k_ref[pl.ds(off, blk), cs],
                                    (((1,), (1,)), ((), ())),
                                    preferred_element_type=F32)

    def pv(mp, j):
        off = pl.multiple_of(j * blk, blk)
        acc_ref[mp] = (acc_ref[mp] * _lanes(a_ref[mp], V_HEAD_DIM)
                       + jnp.dot(p_ref[mp], v_ref[pl.ds(off, blk), :],
                                 preferred_element_type=F32))

    def softmax_strips(mp, bias, masked):
        for r in range(blk // strip):
            rows = slice(r * strip, (r + 1) * strip)
            s = s_ref[mp, rows, :] + bias
            if masked:
                row = r * strip + lax.broadcasted_iota(jnp.int32, (strip, blk), 0)
                col = lax.broadcasted_iota(jnp.int32, (strip, blk), 1)
                s = jnp.where(col <= row, s, MASK_VALUE)
            m_old = m_ref[mp, rows, :]
            m_new = jnp.maximum(m_old, jnp.max(s, axis=-1, keepdims=True))
            alpha = jnp.exp2(m_old - m_new)
            p = jnp.exp2(s - _lanes(m_new, blk))
            p_sum = p[:, :LANES]
            for c in range(1, blk // LANES):
                p_sum = p_sum + p[:, c * LANES:(c + 1) * LANES]
            l_ref[mp, rows, :] = alpha * l_ref[mp, rows, :] + p_sum
            m_ref[mp, rows, :] = m_new
            a_ref[mp, rows, :] = alpha
            p_ref[mp, rows, :] = p.astype(BF16)

    def step(j, masked):
        bias = slope * (k_idx + ((j - i) * blk).astype(F32))
        pv(1, jnp.maximum(j - 1, 0))
        qk(1, j)
        softmax_strips(0, bias, masked)
        pv(0, j)
        if not masked:
            qk(0, j + 1)
        softmax_strips(1, bias, masked)

    qk(0, 0)
    run = n_q_blocks // 2
    while run >= 1:
        @pl.when(lax.bitwise_and(i, run) != 0)
        def _(run=run):
            start = lax.bitwise_and(i, -2 * run)
            for u in range(run):
                step(start + u, masked=False)
        run //= 2
    step(i, masked=True)
    pv(1, i)

    l1 = jnp.sum(l_ref[0], axis=-1, keepdims=True)
    l2 = jnp.sum(l_ref[1], axis=-1, keepdims=True)
    o = acc_ref[0] / l1 - lam * (acc_ref[1] / l2)
    ms = jnp.mean(o * o, axis=-1, keepdims=True)
    y = (o * lax.rsqrt(ms + LN_EPS) * sg_ref[...]) * (1.0 - lam_init)
    o_ref[q_rows, :] = y.astype(o_ref.dtype)


def _diff_attention(qkv, lq1, lk1, lq2, lk2, subln_g, *, batch, seq, n_heads, lam_init,
                    blk=512, strip=32):
    t = qkv.shape[0]
    nq = seq // blk
    assert nq & (nq - 1) == 0, "the key-block walk assumes a power-of-two block count"
    vec = lambda a: a.reshape(1, -1)
    vec_spec = lambda n: pl.BlockSpec((1, n), lambda b, h: (0, 0))
    return pl.pallas_call(
        functools.partial(_attn_kernel, blk=blk, strip=strip, n_q_blocks=nq, n_heads=n_heads,
                          lam_init=lam_init),
        out_shape=jax.ShapeDtypeStruct((t, n_heads * V_HEAD_DIM), BF16),
        grid=(batch, n_heads),
        in_specs=[pl.BlockSpec((seq, 2 * HEAD_DIM), lambda b, h: (b, h)),
                  pl.BlockSpec((seq, 2 * HEAD_DIM), lambda b, h: (b, n_heads + h)),
                  pl.BlockSpec((seq, V_HEAD_DIM), lambda b, h: (b, 2 * n_heads + h)),
                  vec_spec(HEAD_DIM), vec_spec(HEAD_DIM), vec_spec(HEAD_DIM), vec_spec(HEAD_DIM),
                  vec_spec(V_HEAD_DIM)],
        out_specs=pl.BlockSpec((seq, V_HEAD_DIM), lambda b, h: (b, h)),
        scratch_shapes=[pltpu.VMEM((2, blk, blk), F32),
                        pltpu.VMEM((2, blk, blk), BF16),
                        pltpu.VMEM((2, blk, LANES), F32),
                        pltpu.VMEM((2, blk, LANES), F32),
                        pltpu.VMEM((2, blk, LANES), F32),
                        pltpu.VMEM((2, blk, V_HEAD_DIM), F32)],
        compiler_params=_params("parallel", "parallel"),
        name="diff_attention",
    )(qkv, qkv, qkv, vec(lq1), vec(lk1), vec(lq2), vec(lk2), vec(subln_g))


def kernel(x, p, g_mix, w_in, b_gate, ln_v_g, ln_v_b, w_s, b_s, lambda_q1, lambda_k1, lambda_q2, lambda_k2, subln_g, w_br_a, w_br_b, w_o, g_ffn, w_gu, w_down, g_ple, w_ple_gate, w_ple_proj, g_final):
    batch, seq, d_model = x.shape
    depth = w_in.shape[0]
    t = batch * seq
    d_gmlp = ln_v_g.shape[1]
    d_attn = w_br_b.shape[1]
    n_heads = d_attn // V_HEAD_DIM
    d_qk = n_heads * 2 * HEAD_DIM
    d_ff = w_down.shape[1]
    ple_dim = p.shape[-1]

    xf = x.reshape(t, d_model)
    for layer in range(depth):
        lam_init = 0.8 - 0.6 * math.exp(-0.3 * layer)
        w_in_b = w_in[layer].astype(BF16)

        h = _rmsnorm(xf, g_mix[layer], BF16)
        hx = [(h, d_model, 0)]
        bn = 1024
        guv = _matmul("in_proj_gelu", hx, [(w_in_b, 0)], [], _ep_gelu, dot_pairs=((0, 0),),
                      n_out=2 * d_gmlp, out_dtype=BF16, bm=1024, bn=bn)
        qkv_scale = jnp.concatenate([jnp.full((d_qk,), HEAD_DIM ** -0.5 * LOG2_E, F32),
                                     jnp.ones((d_qk + d_attn,), F32)]).reshape(1, -1)
        qkv = _matmul("in_proj_qkv", hx, [(w_in_b, 2 * d_gmlp // bn)], [("row", qkv_scale, 0)], _ep_colscale,
                      dot_pairs=((0, 0),), n_out=2 * d_qk + d_attn, out_dtype=BF16, bm=1024, bn=bn)
        gates = _matmul("in_proj_gates", hx, [(w_in_b, (2 * d_gmlp + 2 * d_qk + d_attn) // bn)],
                        [("row", b_gate[layer].reshape(1, -1), 0)], _ep_sigmoid_bias,
                        dot_pairs=((0, 0),), n_out=2 * d_model, out_dtype=BF16, bm=1024, bn=bn)

        y_a = _gmlp(guv, ln_v_g[layer], ln_v_b[layer], w_s[layer], b_s[layer])
        y_b = _diff_attention(qkv, lambda_q1[layer], lambda_k1[layer], lambda_q2[layer],
                              lambda_k2[layer], subln_g[layer], batch=batch, seq=seq,
                              n_heads=n_heads, lam_init=lam_init)

        bn = 1024
        merged_a = _matmul("merge_a", [(y_a, d_gmlp, 0)], [(w_br_a[layer].astype(BF16), 0)],
                           [("tile", gates, 0)], _ep_gated, dot_pairs=((0, 0),),
                           n_out=d_model, out_dtype=F32, bm=1024, bn=bn)
        merged = _matmul("merge_b", [(y_b, d_attn, 0)], [(w_br_b[layer].astype(BF16), 0)],
                         [("tile", gates, d_model // bn), ("tile", merged_a, 0)], _ep_gated_add,
                         dot_pairs=((0, 0),), n_out=d_model, out_dtype=BF16, bm=1024, bn=bn)
        xf, xg, ssq = _matmul("out_proj", [(merged, d_model, 0)], [(w_o[layer].astype(BF16), 0)],
                              [("tile", xf, 0)], _ep_residual, dot_pairs=((0, 0),),
                              n_out=d_model, out_dtype=F32, bm=1024, bn=512,
                              norm_gain=g_ffn[layer].reshape(1, -1))

        act = _swiglu_up(xg, ssq, w_gu[layer], bm=1024, bn=256)
        xf, xg, ssq = _matmul("ffn_down", [(act, d_ff, 0)], [(w_down[layer].astype(BF16), 0)],
                              [("tile", xf, 0)], _ep_residual, dot_pairs=((0, 0),),
                              n_out=d_model, out_dtype=F32, bm=512, bn=512,
                              norm_gain=g_ple[layer].reshape(1, -1))

        ple_bm, ple_bn = (512, 1024) if layer == depth - 1 else (1024, 512)
        xf = _matmul("ple", [(xg, d_model, 0), (p[layer].reshape(t, ple_dim), ple_dim, 0)],
                     [(w_ple_gate[layer].astype(BF16), 0), (w_ple_proj[layer].astype(BF16), 0)],
                     [("tile", xf, 0), ("rowstat", ssq, 0)],
                     functools.partial(_ep_ple_normed, n=d_model), dot_pairs=((0, 0), (1, 1)),
                     n_out=d_model, out_dtype=F32, bm=ple_bm, bn=ple_bn,
                     row_norm_gain=g_final.reshape(1, -1) if layer == depth - 1 else None)

    return xf.reshape(batch, seq, d_model)
```

```python
import functools
import math

import jax
import jax.numpy as jnp
from jax import lax
from jax.experimental import pallas as pl
from jax.experimental.pallas import tpu as pltpu

F32 = jnp.float32
BF16 = jnp.bfloat16

V7X_VMEM_LIMIT_BYTES = 56 * 1024 * 1024

LANES = 128
LOG2_E = math.log2(math.e)

HEAD_DIM = 128
V_HEAD_DIM = 2 * HEAD_DIM
CHUNK = 128
RMS_EPS = 1e-6
LN_EPS = 1e-5
MASK_VALUE = -1e30


def _params(*semantics):
    return pltpu.CompilerParams(dimension_semantics=semantics,
                                vmem_limit_bytes=V7X_VMEM_LIMIT_BYTES)


def _rmsnorm_kernel(x_ref, g_ref, o_ref, *, eps):
    x = x_ref[...]
    ms = jnp.mean(x * x, axis=-1, keepdims=True)
    o_ref[...] = (x * lax.rsqrt(ms + eps) * g_ref[...]).astype(o_ref.dtype)


def _rmsnorm(x, g, out_dtype, *, rows=512):
    t, d = x.shape
    return pl.pallas_call(
        functools.partial(_rmsnorm_kernel, eps=RMS_EPS),
        out_shape=jax.ShapeDtypeStruct((t, d), out_dtype),
        grid=(t // rows,),
        in_specs=[pl.BlockSpec((rows, d), lambda i: (i, 0)),
                  pl.BlockSpec((1, d), lambda i: (0, 0))],
        out_specs=pl.BlockSpec((rows, d), lambda i: (i, 0)),
        compiler_params=_params("parallel"),
        name="rmsnorm",
    )(x, g.reshape(1, d))


def _mm_kernel(*refs, n_x, n_w, n_extra, dot_pairs, epilogue, emit_norm, row_norm, bn):
    x_refs = refs[:n_x]
    w_refs = refs[n_x:n_x + n_w]
    extra_refs = refs[n_x + n_w:n_x + n_w + n_extra]
    out_refs = refs[n_x + n_w + n_extra:]
    accs = [jnp.dot(x_refs[a][...].astype(BF16), w_refs[b][...],
                    preferred_element_type=F32) for a, b in dot_pairs]
    y = epilogue(accs, extra_refs)
    if row_norm:
        o_ref, ssq_ref = out_refs
        gain_ref = extra_refs[-1]
        j = pl.program_id(1)
        o_ref[:, pl.ds(pl.multiple_of(j * bn, bn), bn)] = y
        part = jnp.broadcast_to(jnp.sum(y * y, axis=-1, keepdims=True), ssq_ref.shape)

        @pl.when(j == 0)
        def _():
            ssq_ref[...] = part

        @pl.when(j != 0)
        def _():
            ssq_ref[...] += part

        @pl.when(j == pl.num_programs(1) - 1)
        def _():
            r = _rms_scale(ssq_ref, o_ref.shape[1])
            for c in range(o_ref.shape[1] // LANES):
                cs = slice(c * LANES, (c + 1) * LANES)
                o_ref[:, cs] = o_ref[:, cs] * r * gain_ref[:, cs]
        return
    out_refs[0][...] = y.astype(out_refs[0].dtype)
    if emit_norm:
        gain_ref = extra_refs[-1]
        xg_ref, ssq_ref = out_refs[1:]
        xg_ref[...] = (y * gain_ref[...]).astype(xg_ref.dtype)
        part = jnp.broadcast_to(jnp.sum(y * y, axis=-1, keepdims=True), ssq_ref.shape)
        j = pl.program_id(1)

        @pl.when(j == 0)
        def _():
            ssq_ref[...] = part

        @pl.when(j != 0)
        def _():
            ssq_ref[...] += part


def _matmul(name, xs, ws, extras, epilogue, *, dot_pairs, n_out, out_dtype, bm, bn,
            norm_gain=None, row_norm_gain=None):
    t = xs[0][0].shape[0]
    emit_norm = norm_gain is not None
    row_norm = row_norm_gain is not None
    if emit_norm:
        extras = list(extras) + [("row", norm_gain, 0)]
    if row_norm:
        extras = list(extras) + [("fullrow", row_norm_gain, 0)]
    in_specs, operands = [], []
    for arr, k, cb in xs:
        in_specs.append(pl.BlockSpec((bm, k), lambda i, j, cb=cb: (i, cb)))
        operands.append(arr)
    for arr, cb0 in ws:
        k = arr.shape[0]
        in_specs.append(pl.BlockSpec((k, bn), lambda i, j, cb0=cb0: (0, cb0 + j)))
        operands.append(arr)
    for kind, arr, cb0 in extras:
        if kind == "tile":
            in_specs.append(pl.BlockSpec((bm, bn), lambda i, j, cb0=cb0: (i, cb0 + j)))
        elif kind == "row":
            in_specs.append(pl.BlockSpec((1, bn), lambda i, j, cb0=cb0: (0, cb0 + j)))
        elif kind == "rowstat":
            in_specs.append(pl.BlockSpec((bm, LANES), lambda i, j: (i, 0)))
        else:
            in_specs.append(pl.BlockSpec((1, n_out), lambda i, j: (0, 0)))
        operands.append(arr)
    tile_spec = pl.BlockSpec((bm, bn), lambda i, j: (i, j))
    stat_spec = pl.BlockSpec((bm, LANES), lambda i, j: (i, 0))
    out_shape = jax.ShapeDtypeStruct((t, n_out), out_dtype)
    out_specs = tile_spec
    scratch_shapes = []
    if emit_norm:
        out_shape = (out_shape, jax.ShapeDtypeStruct((t, n_out), BF16),
                     jax.ShapeDtypeStruct((t, LANES), F32))
        out_specs = (tile_spec, tile_spec, stat_spec)
    if row_norm:
        assert out_dtype == F32 and not emit_norm
        out_specs = pl.BlockSpec((bm, n_out), lambda i, j: (i, 0))
        scratch_shapes = [pltpu.VMEM((bm, LANES), F32)]
    sequential_cols = emit_norm or row_norm
    return pl.pallas_call(
        functools.partial(_mm_kernel, n_x=len(xs), n_w=len(ws), n_extra=len(extras),
                          dot_pairs=dot_pairs, epilogue=epilogue, emit_norm=emit_norm,
                          row_norm=row_norm, bn=bn),
        out_shape=out_shape,
        grid=(t // bm, n_out // bn),
        in_specs=in_specs,
        out_specs=out_specs,
        scratch_shapes=scratch_shapes,
        compiler_params=_params("parallel", "arbitrary" if sequential_cols else "parallel"),
        name=name,
    )(*operands)


def _rms_scale(ssq_ref, n):
    return lax.rsqrt(ssq_ref[:, :1] / n + RMS_EPS)


def _gelu_exact(x):
    return 0.5 * x * (1.0 + lax.erf(x * math.sqrt(0.5)))


def _ep_gelu(accs, extras):
    return _gelu_exact(accs[0])


def _ep_colscale(accs, extras):
    return accs[0] * extras[0][...]


def _ep_sigmoid_bias(accs, extras):
    return jax.nn.sigmoid(accs[0] + extras[0][...])


def _ep_gated(accs, extras):
    return extras[0][...].astype(F32) * accs[0]


def _ep_gated_add(accs, extras):
    return extras[1][...] + extras[0][...].astype(F32) * accs[0]


def _ep_residual(accs, extras):
    return extras[0][...] + accs[0]


def _ep_ple_normed(accs, extras, *, n):
    r = _rms_scale(extras[1], n)
    return extras[0][...] + jax.nn.sigmoid(r * accs[0]) * accs[1]


def _swiglu_up_kernel(x_ref, wg_ref, wu_ref, ssq_ref, o_ref, w_ref, *, bn, n):
    @pl.when(pl.program_id(1) == 0)
    def _():
        w_ref[:, :bn] = wg_ref[...].astype(BF16)
        w_ref[:, bn:] = wu_ref[...].astype(BF16)

    acc = jnp.dot(x_ref[...], w_ref[...], preferred_element_type=F32)
    r = _rms_scale(ssq_ref, n)
    o_ref[...] = (jax.nn.silu(r * acc[:, :bn]) * (r * acc[:, bn:])).astype(o_ref.dtype)


def _swiglu_up(xg, ssq, w_gu, *, bm, bn):
    t, d = xg.shape
    d_ff = w_gu.shape[1] // 2
    return pl.pallas_call(
        functools.partial(_swiglu_up_kernel, bn=bn, n=d),
        out_shape=jax.ShapeDtypeStruct((t, d_ff), BF16),
        grid=(d_ff // bn, t // bm),
        in_specs=[pl.BlockSpec((bm, d), lambda j, i: (i, 0)),
                  pl.BlockSpec((d, bn), lambda j, i: (0, j)),
                  pl.BlockSpec((d, bn), lambda j, i: (0, d_ff // bn + j)),
                  pl.BlockSpec((bm, LANES), lambda j, i: (i, 0))],
        out_specs=pl.BlockSpec((bm, bn), lambda j, i: (i, j)),
        scratch_shapes=[pltpu.VMEM((d, 2 * bn), BF16)],
        compiler_params=_params("arbitrary", "arbitrary"),
        name="ffn_up",
    )(xg, w_gu, w_gu, ssq)


def _gmlp_kernel(gu_ref, gv_ref, lng_ref, lnb_ref, ws_ref, bs_ref, o_ref, wm_ref,
                 *, rows, n_groups, group_dim):
    @pl.when(pl.program_id(0) == 0)
    def _():
        r = lax.broadcasted_iota(jnp.int32, (CHUNK, CHUNK), 0)
        c = lax.broadcasted_iota(jnp.int32, (CHUNK, CHUNK), 1)
        for g in range(n_groups):
            wm_ref[g] = jnp.where(c <= r, ws_ref[g], 0.0).astype(BF16)

    for ch in range(rows // CHUNK):
        rs = slice(ch * CHUNK, (ch + 1) * CHUNK)
        v = gv_ref[rs, :].astype(F32)
        mu = jnp.mean(v, axis=-1, keepdims=True)
        vc = v - mu
        var = jnp.mean(vc * vc, axis=-1, keepdims=True)
        vn = (vc * lax.rsqrt(var + LN_EPS) * lng_ref[...] + lnb_ref[...]).astype(BF16)
        for g in range(n_groups):
            cs = slice(g * group_dim, (g + 1) * group_dim)
            mixed = jnp.dot(wm_ref[g], vn[:, cs], preferred_element_type=F32) + bs_ref[g]
            o_ref[rs, cs] = (gu_ref[rs, cs].astype(F32) * mixed).astype(o_ref.dtype)


def _gmlp(guv, ln_g, ln_b, w_s, b_s, *, rows=512):
    t = guv.shape[0]
    d = guv.shape[1] // 2
    n_groups = w_s.shape[0]
    return pl.pallas_call(
        functools.partial(_gmlp_kernel, rows=rows, n_groups=n_groups, group_dim=d // n_groups),
        out_shape=jax.ShapeDtypeStruct((t, d), BF16),
        grid=(t // rows,),
        in_specs=[pl.BlockSpec((rows, d), lambda i: (i, 0)),
                  pl.BlockSpec((rows, d), lambda i: (i, 1)),
                  pl.BlockSpec((1, d), lambda i: (0, 0)),
                  pl.BlockSpec((1, d), lambda i: (0, 0)),
                  pl.BlockSpec((n_groups, CHUNK, CHUNK), lambda i: (0, 0, 0)),
                  pl.BlockSpec((n_groups, CHUNK, 1), lambda i: (0, 0, 0))],
        out_specs=pl.BlockSpec((rows, d), lambda i: (i, 0)),
        scratch_shapes=[pltpu.VMEM((n_groups, CHUNK, CHUNK), BF16)],
        compiler_params=_params("arbitrary"),
        name="gmlp_gating",
    )(guv, guv, ln_g.reshape(1, d), ln_b.reshape(1, d), w_s, b_s.reshape(n_groups, CHUNK, 1))


def _lanes(x, n):
    return jnp.concatenate([x] * (n // LANES), axis=1)


def _attn_kernel(q_ref, k_ref, v_ref, lq1_ref, lk1_ref, lq2_ref, lk2_ref, sg_ref, o_ref,
                 s_ref, p_ref, m_ref, l_ref, a_ref, acc_ref,
                 *, blk, strip, n_q_blocks, n_heads, lam_init):
    h = pl.program_id(1)
    slope = jnp.exp2(-8.0 * jnp.full((1, blk), h + 1, F32) / n_heads) * LOG2_E
    k_idx = lax.broadcasted_iota(jnp.int32, (1, blk), 1).astype(F32)
    lam = (jnp.exp(jnp.sum(lq1_ref[...] * lk1_ref[...], axis=-1, keepdims=True))
           - jnp.exp(jnp.sum(lq2_ref[...] * lk2_ref[...], axis=-1, keepdims=True))
           + lam_init)

    def query_block(i, carry):
        _attn_query_block(i, slope, k_idx, lam, q_ref, k_ref, v_ref, sg_ref, o_ref,
                          s_ref, p_ref, m_ref, l_ref, a_ref, acc_ref,
                          blk=blk, strip=strip, n_q_blocks=n_q_blocks, lam_init=lam_init)
        return carry

    lax.fori_loop(0, n_q_blocks, query_block, 0)


def _attn_query_block(i, slope, k_idx, lam, q_ref, k_ref, v_ref, sg_ref, o_ref,
                      s_ref, p_ref, m_ref, l_ref, a_ref, acc_ref,
                      *, blk, strip, n_q_blocks, lam_init):
    q_rows = pl.ds(pl.multiple_of(i * blk, blk), blk)

    m_ref[...] = jnp.full(m_ref.shape, MASK_VALUE, F32)
    l_ref[...] = jnp.zeros(l_ref.shape, F32)
    acc_ref[...] = jnp.zeros(acc_ref.shape, F32)
    p_ref[1] = jnp.zeros(p_ref.shape[1:], BF16)
    a_ref[1] = jnp.ones(a_ref.shape[1:], F32)

    def qk(mp, j):
        off = pl.multiple_of(j * blk, blk)
        cs = slice(mp * HEAD_DIM, (mp + 1) * HEAD_DIM)
        s_ref[mp] = lax.dot_general(q_ref[q_rows, cs], k_ref[pl.ds(off, blk), cs],
                                    (((1,), (1,)), ((), ())),
                                    preferred_element_type=F32)

    def pv(mp, j):
        off = pl.multiple_of(j * blk, blk)
        acc_ref[mp] = (acc_ref[mp] * _lanes(a_ref[mp], V_HEAD_DIM)
                       + jnp.dot(p_ref[mp], v_ref[pl.ds(off, blk), :],
                                 preferred_element_type=F32))

    def softmax_strips(mp, bias, masked):
        for r in range(blk // strip):
            rows = slice(r * strip, (r + 1) * strip)
            s = s_ref[mp, rows, :] + bias
            if masked:
                row = r * strip + lax.broadcasted_iota(jnp.int32, (strip, blk), 0)
                col = lax.broadcasted_iota(jnp.int32, (strip, blk), 1)
                s = jnp.where(col <= row, s, MASK_VALUE)
            m_old = m_ref[mp, rows, :]
            m_new = jnp.maximum(m_old, jnp.max(s, axis=-1, keepdims=True))
            alpha = jnp.exp2(m_old - m_new)
            p = jnp.exp2(s - _lanes(m_new, blk))
            p_sum = p[:, :LANES]
            for c in range(1, blk // LANES):
                p_sum = p_sum + p[:, c * LANES:(c + 1) * LANES]
            l_ref[mp, rows, :] = alpha * l_ref[mp, rows, :] + p_sum
            m_ref[mp, rows, :] = m_new
            a_ref[mp, rows, :] = alpha
            p_ref[mp, rows, :] = p.astype(BF16)

    def step(j, masked):
        bias = slope * (k_idx + jnp.asarray((j - i) * blk, F32))
        pv(1, jnp.maximum(j - 1, 0))
        qk(1, j)
        softmax_strips(0, bias, masked)
        pv(0, j)
        if not masked:
            qk(0, j + 1)
        softmax_strips(1, bias, masked)

    qk(0, 0)
    run = n_q_blocks // 2
    while run >= 1:
        @pl.when(lax.bitwise_and(i, run) != 0)
        def _(run=run):
            start = lax.bitwise_and(i, -2 * run)
            for u in range(run):
                step(start + u, masked=False)
        run //= 2
    step(i, masked=True)
    pv(1, i)

    l1 = jnp.sum(l_ref[0], axis=-1, keepdims=True)
    l2 = jnp.sum(l_ref[1], axis=-1, keepdims=True)
    o = acc_ref[0] / l1 - lam * (acc_ref[1] / l2)
    ms = jnp.mean(o * o, axis=-1, keepdims=True)
    y = (o * lax.rsqrt(ms + LN_EPS) * sg_ref[...]) * (1.0 - lam_init)
    o_ref[q_rows, :] = y.astype(o_ref.dtype)


def _diff_attention(qkv, lq1, lk1, lq2, lk2, subln_g, *, batch, seq, n_heads, lam_init,
                    blk=512, strip=32):
    t = qkv.shape[0]
    nq = seq // blk
    assert nq & (nq - 1) == 0, "the key-block walk assumes a power-of-two block count"
    vec = lambda a: a.reshape(1, -1)
    vec_spec = lambda n: pl.BlockSpec((1, n), lambda b, h: (0, 0))
    return pl.pallas_call(
        functools.partial(_attn_kernel, blk=blk, strip=strip, n_q_blocks=nq, n_heads=n_heads,
                          lam_init=lam_init),
        out_shape=jax.ShapeDtypeStruct((t, n_heads * V_HEAD_DIM), BF16),
        grid=(batch, n_heads),
        in_specs=[pl.BlockSpec((seq, 2 * HEAD_DIM), lambda b, h: (b, h)),
                  pl.BlockSpec((seq, 2 * HEAD_DIM), lambda b, h: (b, n_heads + h)),
                  pl.BlockSpec((seq, V_HEAD_DIM), lambda b, h: (b, 2 * n_heads + h)),
                  vec_spec(HEAD_DIM), vec_spec(HEAD_DIM), vec_spec(HEAD_DIM), vec_spec(HEAD_DIM),
                  vec_spec(V_HEAD_DIM)],
        out_specs=pl.BlockSpec((seq, V_HEAD_DIM), lambda b, h: (b, h)),
        scratch_shapes=[pltpu.VMEM((2, blk, blk), F32),
                        pltpu.VMEM((2, blk, blk), BF16),
                        pltpu.VMEM((2, blk, LANES), F32),
                        pltpu.VMEM((2, blk, LANES), F32),
                        pltpu.VMEM((2, blk, LANES), F32),
                        pltpu.VMEM((2, blk, V_HEAD_DIM), F32)],
        compiler_params=_params("parallel", "parallel"),
        name="diff_attention",
    )(qkv, qkv, qkv, vec(lq1), vec(lk1), vec(lq2), vec(lk2), vec(subln_g))


def kernel(x, p, g_mix, w_in, b_gate, ln_v_g, ln_v_b, w_s, b_s, lambda_q1, lambda_k1, lambda_q2, lambda_k2, subln_g, w_br_a, w_br_b, w_o, g_ffn, w_gu, w_down, g_ple, w_ple_gate, w_ple_proj, g_final):
    batch, seq, d_model = x.shape
    depth = w_in.shape[0]
    t = batch * seq
    d_gmlp = ln_v_g.shape[1]
    d_attn = w_br_b.shape[1]
    n_heads = d_attn // V_HEAD_DIM
    d_qk = n_heads * 2 * HEAD_DIM
    d_ff = w_down.shape[1]
    ple_dim = p.shape[-1]

    xf = x.reshape(t, d_model)
    for layer in range(depth):
        lam_init = 0.8 - 0.6 * math.exp(-0.3 * layer)
        w_in_b = w_in[layer].astype(BF16)

        h = _rmsnorm(xf, g_mix[layer], BF16)
        hx = [(h, d_model, 0)]
        bn = 1024
        guv = _matmul("in_proj_gelu", hx, [(w_in_b, 0)], [], _ep_gelu, dot_pairs=((0, 0),),
                      n_out=2 * d_gmlp, out_dtype=BF16, bm=1024, bn=bn)
        qkv_scale = jnp.concatenate([jnp.full((d_qk,), HEAD_DIM ** -0.5 * LOG2_E, F32),
                                     jnp.ones((d_qk + d_attn,), F32)]).reshape(1, -1)
        qkv = _matmul("in_proj_qkv", hx, [(w_in_b, 2 * d_gmlp // bn)], [("row", qkv_scale, 0)], _ep_colscale,
                      dot_pairs=((0, 0),), n_out=2 * d_qk + d_attn, out_dtype=BF16, bm=1024, bn=bn)
        gates = _matmul("in_proj_gates", hx, [(w_in_b, (2 * d_gmlp + 2 * d_qk + d_attn) // bn)],
                        [("row", b_gate[layer].reshape(1, -1), 0)], _ep_sigmoid_bias,
                        dot_pairs=((0, 0),), n_out=2 * d_model, out_dtype=BF16, bm=1024, bn=bn)

        y_a = _gmlp(guv, ln_v_g[layer], ln_v_b[layer], w_s[layer], b_s[layer])
        y_b = _diff_attention(qkv, lambda_q1[layer], lambda_k1[layer], lambda_q2[layer],
                              lambda_k2[layer], subln_g[layer], batch=batch, seq=seq,
                              n_heads=n_heads, lam_init=lam_init)

        bn = 1024
        merged_a = _matmul("merge_a", [(y_a, d_gmlp, 0)], [(w_br_a[layer].astype(BF16), 0)],
                           [("tile", gates, 0)], _ep_gated, dot_pairs=((0, 0),),
                           n_out=d_model, out_dtype=F32, bm=1024, bn=bn)
        merged = _matmul("merge_b", [(y_b, d_attn, 0)], [(w_br_b[layer].astype(BF16), 0)],
                         [("tile", gates, d_model // bn), ("tile", merged_a, 0)], _ep_gated_add,
                         dot_pairs=((0, 0),), n_out=d_model, out_dtype=BF16, bm=1024, bn=bn)
        xf, xg, ssq = _matmul("out_proj", [(merged, d_model, 0)], [(w_o[layer].astype(BF16), 0)],
                              [("tile", xf, 0)], _ep_residual, dot_pairs=((0, 0),),
                              n_out=d_model, out_dtype=F32, bm=1024, bn=512,
                              norm_gain=g_ffn[layer].reshape(1, -1))

        act = _swiglu_up(xg, ssq, w_gu[layer], bm=1024, bn=256)
        xf, xg, ssq = _matmul("ffn_down", [(act, d_ff, 0)], [(w_down[layer].astype(BF16), 0)],
                              [("tile", xf, 0)], _ep_residual, dot_pairs=((0, 0),),
                              n_out=d_model, out_dtype=F32, bm=512, bn=512,
                              norm_gain=g_ple[layer].reshape(1, -1))

        ple_bm, ple_bn = (512, 1024) if layer == depth - 1 else (1024, 512)
        xf = _matmul("ple", [(xg, d_model, 0), (p[layer].reshape(t, ple_dim), ple_dim, 0)],
                     [(w_ple_gate[layer].astype(BF16), 0), (w_ple_proj[layer].astype(BF16), 0)],
                     [("tile", xf, 0), ("rowstat", ssq, 0)],
                     functools.partial(_ep_ple_normed, n=d_model), dot_pairs=((0, 0), (1, 1)),
                     n_out=d_model, out_dtype=F32, bm=ple_bm, bn=ple_bn,
                     row_norm_gain=g_final.reshape(1, -1) if layer == depth - 1 else None)

    return xf.reshape(batch, seq, d_model)
```

```python
import functools
import math

import jax
import jax.numpy as jnp
from jax import lax
from jax.experimental import pallas as pl
from jax.experimental.pallas import tpu as pltpu

F32 = jnp.float32
BF16 = jnp.bfloat16

V7X_VMEM_LIMIT_BYTES = 56 * 1024 * 1024

LANES = 128
LOG2_E = math.log2(math.e)

HEAD_DIM = 128
V_HEAD_DIM = 2 * HEAD_DIM
CHUNK = 128
RMS_EPS = 1e-6
LN_EPS = 1e-5
MASK_VALUE = -1e30


def _params(*semantics):
    return pltpu.CompilerParams(dimension_semantics=semantics,
                                vmem_limit_bytes=V7X_VMEM_LIMIT_BYTES)


def _rmsnorm_kernel(x_ref, g_ref, o_ref, *, eps):
    x = x_ref[...]
    ms = jnp.mean(x * x, axis=-1, keepdims=True)
    o_ref[...] = (x * lax.rsqrt(ms + eps) * g_ref[...]).astype(o_ref.dtype)


def _rmsnorm(x, g, out_dtype, *, rows=512):
    t, d = x.shape
    return pl.pallas_call(
        functools.partial(_rmsnorm_kernel, eps=RMS_EPS),
        out_shape=jax.ShapeDtypeStruct((t, d), out_dtype),
        grid=(t // rows,),
        in_specs=[pl.BlockSpec((rows, d), lambda i: (i, 0)),
                  pl.BlockSpec((1, d), lambda i: (0, 0))],
        out_specs=pl.BlockSpec((rows, d), lambda i: (i, 0)),
        compiler_params=_params("parallel"),
        name="rmsnorm",
    )(x, g.reshape(1, d))


def _mm_kernel(*refs, n_x, n_w, n_extra, dot_pairs, epilogue, emit_norm, row_norm, bn):
    x_refs = refs[:n_x]
    w_refs = refs[n_x:n_x + n_w]
    extra_refs = refs[n_x + n_w:n_x + n_w + n_extra]
    out_refs = refs[n_x + n_w + n_extra:]
    accs = [jnp.dot(x_refs[a][...].astype(BF16), w_refs[b][...],
                    preferred_element_type=F32) for a, b in dot_pairs]
    y = epilogue(accs, extra_refs)
    if row_norm:
        o_ref, ssq_ref = out_refs
        gain_ref = extra_refs[-1]
        j = pl.program_id(1)
        o_ref[:, pl.ds(pl.multiple_of(j * bn, bn), bn)] = y
        part = jnp.broadcast_to(jnp.sum(y * y, axis=-1, keepdims=True), ssq_ref.shape)

        @pl.when(j == 0)
        def _():
            ssq_ref[...] = part

        @pl.when(j != 0)
        def _():
            ssq_ref[...] += part

        @pl.when(j == pl.num_programs(1) - 1)
        def _():
            r = _rms_scale(ssq_ref, o_ref.shape[1])
            for c in range(o_ref.shape[1] // LANES):
                cs = slice(c * LANES, (c + 1) * LANES)
                o_ref[:, cs] = o_ref[:, cs] * r * gain_ref[:, cs]
        return
    out_refs[0][...] = y.astype(out_refs[0].dtype)
    if emit_norm:
        gain_ref = extra_refs[-1]
        xg_ref, ssq_ref = out_refs[1:]
        xg_ref[...] = (y * gain_ref[...]).astype(xg_ref.dtype)
        part = jnp.broadcast_to(jnp.sum(y * y, axis=-1, keepdims=True), ssq_ref.shape)
        j = pl.program_id(1)

        @pl.when(j == 0)
        def _():
            ssq_ref[...] = part

        @pl.when(j != 0)
        def _():
            ssq_ref[...] += part


def _matmul(name, xs, ws, extras, epilogue, *, dot_pairs, n_out, out_dtype, bm, bn,
            norm_gain=None, row_norm_gain=None):
    t = xs[0][0].shape[0]
    emit_norm = norm_gain is not None
    row_norm = row_norm_gain is not None
    if emit_norm:
        extras = list(extras) + [("row", norm_gain, 0)]
    if row_norm:
        extras = list(extras) + [("fullrow", row_norm_gain, 0)]
    in_specs, operands = [], []
    for arr, k, cb in xs:
        in_specs.append(pl.BlockSpec((bm, k), lambda i, j, cb=cb: (i, cb)))
        operands.append(arr)
    for arr, cb0 in ws:
        k = arr.shape[0]
        in_specs.append(pl.BlockSpec((k, bn), lambda i, j, cb0=cb0: (0, cb0 + j)))
        operands.append(arr)
    for kind, arr, cb0 in extras:
        if kind == "tile":
            in_specs.append(pl.BlockSpec((bm, bn), lambda i, j, cb0=cb0: (i, cb0 + j)))
        elif kind == "row":
            in_specs.append(pl.BlockSpec((1, bn), lambda i, j, cb0=cb0: (0, cb0 + j)))
        elif kind == "rowstat":
            in_specs.append(pl.BlockSpec((bm, LANES), lambda i, j: (i, 0)))
        else:
            in_specs.append(pl.BlockSpec((1, n_out), lambda i, j: (0, 0)))
        operands.append(arr)
    tile_spec = pl.BlockSpec((bm, bn), lambda i, j: (i, j))
    stat_spec = pl.BlockSpec((bm, LANES), lambda i, j: (i, 0))
    out_shape = jax.ShapeDtypeStruct((t, n_out), out_dtype)
    out_specs = tile_spec
    scratch_shapes = []
    if emit_norm:
        out_shape = (out_shape, jax.ShapeDtypeStruct((t, n_out), BF16),
                     jax.ShapeDtypeStruct((t, LANES), F32))
        out_specs = (tile_spec, tile_spec, stat_spec)
    if row_norm:
        assert out_dtype == F32 and not emit_norm
        out_specs = pl.BlockSpec((bm, n_out), lambda i, j: (i, 0),
                                 pipeline_mode=pl.Buffered(1))
        scratch_shapes = [pltpu.VMEM((bm, LANES), F32)]
    sequential_cols = emit_norm or row_norm
    return pl.pallas_call(
        functools.partial(_mm_kernel, n_x=len(xs), n_w=len(ws), n_extra=len(extras),
                          dot_pairs=dot_pairs, epilogue=epilogue, emit_norm=emit_norm,
                          row_norm=row_norm, bn=bn),
        out_shape=out_shape,
        grid=(t // bm, n_out // bn),
        in_specs=in_specs,
        out_specs=out_specs,
        scratch_shapes=scratch_shapes,
        compiler_params=_params("parallel", "arbitrary" if sequential_cols else "parallel"),
        name=name,
    )(*operands)


def _rms_scale(ssq_ref, n):
    return lax.rsqrt(ssq_ref[:, :1] / n + RMS_EPS)


def _gelu_exact(x):
    return 0.5 * x * (1.0 + lax.erf(x * math.sqrt(0.5)))


def _ep_gelu(accs, extras):
    return _gelu_exact(accs[0])


def _ep_colscale(accs, extras):
    return accs[0] * extras[0][...]


def _ep_sigmoid_bias(accs, extras):
    return jax.nn.sigmoid(accs[0] + extras[0][...])


def _ep_gated(accs, extras):
    return extras[0][...].astype(F32) * accs[0]


def _ep_gated_add(accs, extras):
    return extras[1][...] + extras[0][...].astype(F32) * accs[0]


def _ep_residual(accs, extras):
    return extras[0][...] + accs[0]


def _ep_ple_normed(accs, extras, *, n):
    r = _rms_scale(extras[1], n)
    return extras[0][...] + jax.nn.sigmoid(r * accs[0]) * accs[1]


def _swiglu_up_kernel(x_ref, wg_ref, wu_ref, ssq_ref, o_ref, w_ref, *, bn, n):
    @pl.when(pl.program_id(1) == 0)
    def _():
        w_ref[:, :bn] = wg_ref[...].astype(BF16)
        w_ref[:, bn:] = wu_ref[...].astype(BF16)

    acc = jnp.dot(x_ref[...], w_ref[...], preferred_element_type=F32)
    r = _rms_scale(ssq_ref, n)
    o_ref[...] = (jax.nn.silu(r * acc[:, :bn]) * (r * acc[:, bn:])).astype(o_ref.dtype)


def _swiglu_up(xg, ssq, w_gu, *, bm, bn):
    t, d = xg.shape
    d_ff = w_gu.shape[1] // 2
    return pl.pallas_call(
        functools.partial(_swiglu_up_kernel, bn=bn, n=d),
        out_shape=jax.ShapeDtypeStruct((t, d_ff), BF16),
        grid=(d_ff // bn, t // bm),
        in_specs=[pl.BlockSpec((bm, d), lambda j, i: (i, 0)),
                  pl.BlockSpec((d, bn), lambda j, i: (0, j)),
                  pl.BlockSpec((d, bn), lambda j, i: (0, d_ff // bn + j)),
                  pl.BlockSpec((bm, LANES), lambda j, i: (i, 0))],
        out_specs=pl.BlockSpec((bm, bn), lambda j, i: (i, j)),
        scratch_shapes=[pltpu.VMEM((d, 2 * bn), BF16)],
        compiler_params=_params("arbitrary", "arbitrary"),
        name="ffn_up",
    )(xg, w_gu, w_gu, ssq)


def _gmlp_kernel(gu_ref, gv_ref, lng_ref, lnb_ref, ws_ref, bs_ref, o_ref, wm_ref,
                 *, rows, n_groups, group_dim):
    @pl.when(pl.program_id(0) == 0)
    def _():
        r = lax.broadcasted_iota(jnp.int32, (CHUNK, CHUNK), 0)
        c = lax.broadcasted_iota(jnp.int32, (CHUNK, CHUNK), 1)
        for g in range(n_groups):
            wm_ref[g] = jnp.where(c <= r, ws_ref[g], 0.0).astype(BF16)

    for ch in range(rows // CHUNK):
        rs = slice(ch * CHUNK, (ch + 1) * CHUNK)
        v = gv_ref[rs, :].astype(F32)
        mu = jnp.mean(v, axis=-1, keepdims=True)
        vc = v - mu
        var = jnp.mean(vc * vc, axis=-1, keepdims=True)
        vn = (vc * lax.rsqrt(var + LN_EPS) * lng_ref[...] + lnb_ref[...]).astype(BF16)
        for g in range(n_groups):
            cs = slice(g * group_dim, (g + 1) * group_dim)
            mixed = jnp.dot(wm_ref[g], vn[:, cs], preferred_element_type=F32) + bs_ref[g]
            o_ref[rs, cs] = (gu_ref[rs, cs].astype(F32) * mixed).astype(o_ref.dtype)


def _gmlp(guv, ln_g, ln_b, w_s, b_s, *, rows=512):
    t = guv.shape[0]
    d = guv.shape[1] // 2
    n_groups = w_s.shape[0]
    return pl.pallas_call(
        functools.partial(_gmlp_kernel, rows=rows, n_groups=n_groups, group_dim=d // n_groups),
        out_shape=jax.ShapeDtypeStruct((t, d), BF16),
        grid=(t // rows,),
        in_specs=[pl.BlockSpec((rows, d), lambda i: (i, 0)),
                  pl.BlockSpec((rows, d), lambda i: (i, 1)),
                  pl.BlockSpec((1, d), lambda i: (0, 0)),
                  pl.BlockSpec((1, d), lambda i: (0, 0)),
                  pl.BlockSpec((n_groups, CHUNK, CHUNK), lambda i: (0, 0, 0)),
                  pl.BlockSpec((n_groups, CHUNK, 1), lambda i: (0, 0, 0))],
        out_specs=pl.BlockSpec((rows, d), lambda i: (i, 0)),
        scratch_shapes=[pltpu.VMEM((n_groups, CHUNK, CHUNK), BF16)],
        compiler_params=_params("arbitrary"),
        name="gmlp_gating",
    )(guv, guv, ln_g.reshape(1, d), ln_b.reshape(1, d), w_s, b_s.reshape(n_groups, CHUNK, 1))


def _lanes(x, n):
    return jnp.concatenate([x] * (n // LANES), axis=1)


def _attn_kernel(q_ref, k_ref, v_ref, lq1_ref, lk1_ref, lq2_ref, lk2_ref, sg_ref, o_ref,
                 s_ref, p_ref, m_ref, l_ref, a_ref, acc_ref,
                 *, blk, strip, n_q_blocks, n_heads, lam_init):
    h = pl.program_id(1)
    slope = jnp.exp2(-8.0 * jnp.full((1, blk), h + 1, F32) / n_heads) * LOG2_E
    k_idx = lax.broadcasted_iota(jnp.int32, (1, blk), 1).astype(F32)
    lam = (jnp.exp(jnp.sum(lq1_ref[...] * lk1_ref[...], axis=-1, keepdims=True))
           - jnp.exp(jnp.sum(lq2_ref[...] * lk2_ref[...], axis=-1, keepdims=True))
           + lam_init)

    def query_block(i, carry):
        _attn_query_block(i, slope, k_idx, lam, q_ref, k_ref, v_ref, sg_ref, o_ref,
                          s_ref, p_ref, m_ref, l_ref, a_ref, acc_ref,
                          blk=blk, strip=strip, n_q_blocks=n_q_blocks, lam_init=lam_init)
        return carry

    lax.fori_loop(0, n_q_blocks, query_block, 0)


def _attn_query_block(i, slope, k_idx, lam, q_ref, k_ref, v_ref, sg_ref, o_ref,
                      s_ref, p_ref, m_ref, l_ref, a_ref, acc_ref,
                      *, blk, strip, n_q_blocks, lam_init):
    q_rows = pl.ds(pl.multiple_of(i * blk, blk), blk)

    m_ref[...] = jnp.full(m_ref.shape, MASK_VALUE, F32)
    l_ref[...] = jnp.zeros(l_ref.shape, F32)
    acc_ref[...] = jnp.zeros(acc_ref.shape, F32)
    p_ref[1] = jnp.zeros(p_ref.shape[1:], BF16)
    a_ref[1] = jnp.ones(a_ref.shape[1:], F32)

    def qk(mp, j):
        off = pl.multiple_of(j * blk, blk)
        cs = slice(mp * HEAD_DIM, (mp + 1) * HEAD_DIM)
        s_ref[mp] = lax.dot_general(q_ref[q_rows, cs], k_ref[pl.ds(off, blk), cs],
                                    (((1,), (1,)), ((), ())),
                                    preferred_element_type=F32)

    def pv(mp, j):
        off = pl.multiple_of(j * blk, blk)
        acc_ref[mp] = (acc_ref[mp] * _lanes(a_ref[mp], V_HEAD_DIM)
                       + jnp.dot(p_ref[mp], v_ref[pl.ds(off, blk), :],
                                 preferred_element_type=F32))

    def softmax_strips(mp, bias, masked):
        for r in range(blk // strip):
            rows = slice(r * strip, (r + 1) * strip)
            s = s_ref[mp, rows, :] + bias
            if masked:
                row = r * strip + lax.broadcasted_iota(jnp.int32, (strip, blk), 0)
                col = lax.broadcasted_iota(jnp.int32, (strip, blk), 1)
                s = jnp.where(col <= row, s, MASK_VALUE)
            m_old = m_ref[mp, rows, :]
            m_new = jnp.maximum(m_old, jnp.max(s, axis=-1, keepdims=True))
            alpha = jnp.exp2(m_old - m_new)
            p = jnp.exp2(s - _lanes(m_new, blk))
            p_sum = p[:, :LANES]
            for c in range(1, blk // LANES):
                p_sum = p_sum + p[:, c * LANES:(c + 1) * LANES]
            l_ref[mp, rows, :] = alpha * l_ref[mp, rows, :] + p_sum
            m_ref[mp, rows, :] = m_new
            a_ref[mp, rows, :] = alpha
            p_ref[mp, rows, :] = p.astype(BF16)

    def step(j, masked):
        bias = slope * (k_idx + jnp.asarray((j - i) * blk, F32))
        pv(1, jnp.maximum(j - 1, 0))
        qk(1, j)
        softmax_strips(0, bias, masked)
        pv(0, j)
        if not masked:
            qk(0, j + 1)
        softmax_strips(1, bias, masked)

    qk(0, 0)
    run = n_q_blocks // 2
    while run >= 1:
        @pl.when(lax.bitwise_and(i, run) != 0)
        def _(run=run):
            start = lax.bitwise_and(i, -2 * run)
            for u in range(run):
                step(start + u, masked=False)
        run //= 2
    step(i, masked=True)
    pv(1, i)

    l1 = jnp.sum(l_ref[0], axis=-1, keepdims=True)
    l2 = jnp.sum(l_ref[1], axis=-1, keepdims=True)
    o = acc_ref[0] / l1 - lam * (acc_ref[1] / l2)
    ms = jnp.mean(o * o, axis=-1, keepdims=True)
    y = (o * lax.rsqrt(ms + LN_EPS) * sg_ref[...]) * (1.0 - lam_init)
    o_ref[q_rows, :] = y.astype(o_ref.dtype)


def _diff_attention(qkv, lq1, lk1, lq2, lk2, subln_g, *, batch, seq, n_heads, lam_init,
                    blk=512, strip=32):
    t = qkv.shape[0]
    nq = seq // blk
    assert nq & (nq - 1) == 0, "the key-block walk assumes a power-of-two block count"
    vec = lambda a: a.reshape(1, -1)
    vec_spec = lambda n: pl.BlockSpec((1, n), lambda b, h: (0, 0))
    return pl.pallas_call(
        functools.partial(_attn_kernel, blk=blk, strip=strip, n_q_blocks=nq, n_heads=n_heads,
                          lam_init=lam_init),
        out_shape=jax.ShapeDtypeStruct((t, n_heads * V_HEAD_DIM), BF16),
        grid=(batch, n_heads),
        in_specs=[pl.BlockSpec((seq, 2 * HEAD_DIM), lambda b, h: (b, h)),
                  pl.BlockSpec((seq, 2 * HEAD_DIM), lambda b, h: (b, n_heads + h)),
                  pl.BlockSpec((seq, V_HEAD_DIM), lambda b, h: (b, 2 * n_heads + h)),
                  vec_spec(HEAD_DIM), vec_spec(HEAD_DIM), vec_spec(HEAD_DIM), vec_spec(HEAD_DIM),
                  vec_spec(V_HEAD_DIM)],
        out_specs=pl.BlockSpec((seq, V_HEAD_DIM), lambda b, h: (b, h)),
        scratch_shapes=[pltpu.VMEM((2, blk, blk), F32),
                        pltpu.VMEM((2, blk, blk), BF16),
                        pltpu.VMEM((2, blk, LANES), F32),
                        pltpu.VMEM((2, blk, LANES), F32),
                        pltpu.VMEM((2, blk, LANES), F32),
                        pltpu.VMEM((2, blk, V_HEAD_DIM), F32)],
        compiler_params=_params("parallel", "parallel"),
        name="diff_attention",
    )(qkv, qkv, qkv, vec(lq1), vec(lk1), vec(lq2), vec(lk2), vec(subln_g))


def kernel(x, p, g_mix, w_in, b_gate, ln_v_g, ln_v_b, w_s, b_s, lambda_q1, lambda_k1, lambda_q2, lambda_k2, subln_g, w_br_a, w_br_b, w_o, g_ffn, w_gu, w_down, g_ple, w_ple_gate, w_ple_proj, g_final):
    batch, seq, d_model = x.shape
    depth = w_in.shape[0]
    t = batch * seq
    d_gmlp = ln_v_g.shape[1]
    d_attn = w_br_b.shape[1]
    n_heads = d_attn // V_HEAD_DIM
    d_qk = n_heads * 2 * HEAD_DIM
    d_ff = w_down.shape[1]
    ple_dim = p.shape[-1]

    xf = x.reshape(t, d_model)
    for layer in range(depth):
        lam_init = 0.8 - 0.6 * math.exp(-0.3 * layer)
        w_in_b = w_in[layer].astype(BF16)

        h = _rmsnorm(xf, g_mix[layer], BF16)
        hx = [(h, d_model, 0)]
        bn = 1024
        guv = _matmul("in_proj_gelu", hx, [(w_in_b, 0)], [], _ep_gelu, dot_pairs=((0, 0),),
                      n_out=2 * d_gmlp, out_dtype=BF16, bm=1024, bn=bn)
        qkv_scale = jnp.concatenate([jnp.full((d_qk,), HEAD_DIM ** -0.5 * LOG2_E, F32),
                                     jnp.ones((d_qk + d_attn,), F32)]).reshape(1, -1)
        qkv = _matmul("in_proj_qkv", hx, [(w_in_b, 2 * d_gmlp // bn)], [("row", qkv_scale, 0)], _ep_colscale,
                      dot_pairs=((0, 0),), n_out=2 * d_qk + d_attn, out_dtype=BF16, bm=1024, bn=bn)
        gates = _matmul("in_proj_gates", hx, [(w_in_b, (2 * d_gmlp + 2 * d_qk + d_attn) // bn)],
                        [("row", b_gate[layer].reshape(1, -1), 0)], _ep_sigmoid_bias,
                        dot_pairs=((0, 0),), n_out=2 * d_model, out_dtype=BF16, bm=1024, bn=bn)

        y_a = _gmlp(guv, ln_v_g[layer], ln_v_b[layer], w_s[layer], b_s[layer])
        y_b = _diff_attention(qkv, lambda_q1[layer], lambda_k1[layer], lambda_q2[layer],
                              lambda_k2[layer], subln_g[layer], batch=batch, seq=seq,
                              n_heads=n_heads, lam_init=lam_init)

        bn = 1024
        merged_a = _matmul("merge_a", [(y_a, d_gmlp, 0)], [(w_br_a[layer].astype(BF16), 0)],
                           [("tile", gates, 0)], _ep_gated, dot_pairs=((0, 0),),
                           n_out=d_model, out_dtype=F32, bm=1024, bn=bn)
        merged = _matmul("merge_b", [(y_b, d_attn, 0)], [(w_br_b[layer].astype(BF16), 0)],
                         [("tile", gates, d_model // bn), ("tile", merged_a, 0)], _ep_gated_add,
                         dot_pairs=((0, 0),), n_out=d_model, out_dtype=BF16, bm=1024, bn=bn)
        xf, xg, ssq = _matmul("out_proj", [(merged, d_model, 0)], [(w_o[layer].astype(BF16), 0)],
                              [("tile", xf, 0)], _ep_residual, dot_pairs=((0, 0),),
                              n_out=d_model, out_dtype=F32, bm=1024, bn=512,
                              norm_gain=g_ffn[layer].reshape(1, -1))

        act = _swiglu_up(xg, ssq, w_gu[layer], bm=1024, bn=256)
        xf, xg, ssq = _matmul("ffn_down", [(act, d_ff, 0)], [(w_down[layer].astype(BF16), 0)],
                              [("tile", xf, 0)], _ep_residual, dot_pairs=((0, 0),),
                              n_out=d_model, out_dtype=F32, bm=512, bn=512,
                              norm_gain=g_ple[layer].reshape(1, -1))

        ple_bm, ple_bn = (1024, 512)
        xf = _matmul("ple", [(xg, d_model, 0), (p[layer].reshape(t, ple_dim), ple_dim, 0)],
                     [(w_ple_gate[layer].astype(BF16), 0), (w_ple_proj[layer].astype(BF16), 0)],
                     [("tile", xf, 0), ("rowstat", ssq, 0)],
                     functools.partial(_ep_ple_normed, n=d_model), dot_pairs=((0, 0), (1, 1)),
                     n_out=d_model, out_dtype=F32, bm=ple_bm, bn=ple_bn,
                     row_norm_gain=g_final.reshape(1, -1) if layer == depth - 1 else None)

    return xf.reshape(batch, seq, d_model)
```

```python
import functools
import math

import jax
import jax.numpy as jnp
from jax import lax
from jax.experimental import pallas as pl
from jax.experimental.pallas import tpu as pltpu

F32 = jnp.float32
BF16 = jnp.bfloat16

V7X_VMEM_LIMIT_BYTES = 56 * 1024 * 1024

LANES = 128
LOG2_E = math.log2(math.e)

HEAD_DIM = 128
V_HEAD_DIM = 2 * HEAD_DIM
CHUNK = 128
RMS_EPS = 1e-6
LN_EPS = 1e-5
MASK_VALUE = -1e30


def _params(*semantics):
    return pltpu.CompilerParams(dimension_semantics=semantics,
                                vmem_limit_bytes=V7X_VMEM_LIMIT_BYTES)


def _rmsnorm_kernel(x_ref, g_ref, o_ref, *, eps):
    x = x_ref[...]
    ms = jnp.mean(x * x, axis=-1, keepdims=True)
    o_ref[...] = (x * lax.rsqrt(ms + eps) * g_ref[...]).astype(o_ref.dtype)


def _rmsnorm(x, g, out_dtype, *, rows=512):
    t, d = x.shape
    return pl.pallas_call(
        functools.partial(_rmsnorm_kernel, eps=RMS_EPS),
        out_shape=jax.ShapeDtypeStruct((t, d), out_dtype),
        grid=(t // rows,),
        in_specs=[pl.BlockSpec((rows, d), lambda i: (i, 0)),
                  pl.BlockSpec((1, d), lambda i: (0, 0))],
        out_specs=pl.BlockSpec((rows, d), lambda i: (i, 0)),
        compiler_params=_params("parallel"),
        name="rmsnorm",
    )(x, g.reshape(1, d))


def _mm_kernel(*refs, n_x, n_w, n_extra, dot_pairs, epilogue, emit_norm, row_norm, bn):
    x_refs = refs[:n_x]
    w_refs = refs[n_x:n_x + n_w]
    extra_refs = refs[n_x + n_w:n_x + n_w + n_extra]
    out_refs = refs[n_x + n_w + n_extra:]
    accs = [jnp.dot(x_refs[a][...].astype(BF16), w_refs[b][...],
                    preferred_element_type=F32) for a, b in dot_pairs]
    y = epilogue(accs, extra_refs)
    if row_norm:
        o_ref, ssq_ref = out_refs
        gain_ref = extra_refs[-1]
        j = pl.program_id(1)
        o_ref[:, pl.ds(pl.multiple_of(j * bn, bn), bn)] = y
        part = jnp.broadcast_to(jnp.sum(y * y, axis=-1, keepdims=True), ssq_ref.shape)

        @pl.when(j == 0)
        def _():
            ssq_ref[...] = part

        @pl.when(j != 0)
        def _():
            ssq_ref[...] += part

        @pl.when(j == pl.num_programs(1) - 1)
        def _():
            r = _rms_scale(ssq_ref, o_ref.shape[1])
            for c in range(o_ref.shape[1] // LANES):
                cs = slice(c * LANES, (c + 1) * LANES)
                o_ref[:, cs] = o_ref[:, cs] * r * gain_ref[:, cs]
        return
    out_refs[0][...] = y.astype(out_refs[0].dtype)
    if emit_norm:
        gain_ref = extra_refs[-1]
        xg_ref, ssq_ref = out_refs[1:]
        xg_ref[...] = (y * gain_ref[...]).astype(xg_ref.dtype)
        part = jnp.broadcast_to(jnp.sum(y * y, axis=-1, keepdims=True), ssq_ref.shape)
        j = pl.program_id(1)

        @pl.when(j == 0)
        def _():
            ssq_ref[...] = part

        @pl.when(j != 0)
        def _():
            ssq_ref[...] += part


def _matmul(name, xs, ws, extras, epilogue, *, dot_pairs, n_out, out_dtype, bm, bn,
            norm_gain=None, row_norm_gain=None):
    t = xs[0][0].shape[0]
    emit_norm = norm_gain is not None
    row_norm = row_norm_gain is not None
    if emit_norm:
        extras = list(extras) + [("row", norm_gain, 0)]
    if row_norm:
        extras = list(extras) + [("fullrow", row_norm_gain, 0)]
    in_specs, operands = [], []
    for arr, k, cb in xs:
        in_specs.append(pl.BlockSpec((bm, k), lambda i, j, cb=cb: (i, cb)))
        operands.append(arr)
    for arr, cb0 in ws:
        k = arr.shape[0]
        in_specs.append(pl.BlockSpec((k, bn), lambda i, j, cb0=cb0: (0, cb0 + j)))
        operands.append(arr)
    for kind, arr, cb0 in extras:
        if kind == "tile":
            in_specs.append(pl.BlockSpec((bm, bn), lambda i, j, cb0=cb0: (i, cb0 + j)))
        elif kind == "row":
            in_specs.append(pl.BlockSpec((1, bn), lambda i, j, cb0=cb0: (0, cb0 + j)))
        elif kind == "rowstat":
            in_specs.append(pl.BlockSpec((bm, LANES), lambda i, j: (i, 0)))
        else:
            in_specs.append(pl.BlockSpec((1, n_out), lambda i, j: (0, 0)))
        operands.append(arr)
    tile_spec = pl.BlockSpec((bm, bn), lambda i, j: (i, j))
    stat_spec = pl.BlockSpec((bm, LANES), lambda i, j: (i, 0))
    out_shape = jax.ShapeDtypeStruct((t, n_out), out_dtype)
    out_specs = tile_spec
    scratch_shapes = []
    if emit_norm:
        out_shape = (out_shape, jax.ShapeDtypeStruct((t, n_out), BF16),
                     jax.ShapeDtypeStruct((t, LANES), F32))
        out_specs = (tile_spec, tile_spec, stat_spec)
    if row_norm:
        assert out_dtype == F32 and not emit_norm
        out_specs = pl.BlockSpec((bm, n_out), lambda i, j: (i, 0))
        scratch_shapes = [pltpu.VMEM((bm, LANES), F32)]
    sequential_cols = emit_norm or row_norm
    return pl.pallas_call(
        functools.partial(_mm_kernel, n_x=len(xs), n_w=len(ws), n_extra=len(extras),
                          dot_pairs=dot_pairs, epilogue=epilogue, emit_norm=emit_norm,
                          row_norm=row_norm, bn=bn),
        out_shape=out_shape,
        grid=(t // bm, n_out // bn),
        in_specs=in_specs,
        out_specs=out_specs,
        scratch_shapes=scratch_shapes,
        compiler_params=_params("parallel", "arbitrary" if sequential_cols else "parallel"),
        name=name,
    )(*operands)


def _rms_scale(ssq_ref, n):
    return lax.rsqrt(ssq_ref[:, :1] / n + RMS_EPS)


def _gelu_exact(x):
    return 0.5 * x * (1.0 + lax.erf(x * math.sqrt(0.5)))


def _ep_gelu(accs, extras):
    return _gelu_exact(accs[0])


def _ep_colscale(accs, extras):
    return accs[0] * extras[0][...]


def _ep_sigmoid_bias(accs, extras):
    return jax.nn.sigmoid(accs[0] + extras[0][...])


def _ep_gated(accs, extras):
    return extras[0][...].astype(F32) * accs[0]


def _ep_gated_add(accs, extras):
    return extras[1][...] + extras[0][...].astype(F32) * accs[0]


def _ep_residual(accs, extras):
    return extras[0][...] + accs[0]


def _ep_ple_normed(accs, extras, *, n):
    r = _rms_scale(extras[1], n)
    return extras[0][...] + jax.nn.sigmoid(r * accs[0]) * accs[1]


def _swiglu_up_kernel(x_ref, wg_ref, wu_ref, ssq_ref, o_ref, w_ref, *, bn, n):
    @pl.when(pl.program_id(1) == 0)
    def _():
        w_ref[:, :bn] = wg_ref[...].astype(BF16)
        w_ref[:, bn:] = wu_ref[...].astype(BF16)

    acc = jnp.dot(x_ref[...], w_ref[...], preferred_element_type=F32)
    r = _rms_scale(ssq_ref, n)
    o_ref[...] = (jax.nn.silu(r * acc[:, :bn]) * (r * acc[:, bn:])).astype(o_ref.dtype)


def _swiglu_up(xg, ssq, w_gu, *, tile0, n_tiles, bm, bn):
    t, d = xg.shape
    d_ff = w_gu.shape[1] // 2
    return pl.pallas_call(
        functools.partial(_swiglu_up_kernel, bn=bn, n=d),
        out_shape=jax.ShapeDtypeStruct((t, n_tiles * bn), BF16),
        grid=(n_tiles, t // bm),
        in_specs=[pl.BlockSpec((bm, d), lambda j, i: (i, 0)),
                  pl.BlockSpec((d, bn), lambda j, i: (0, tile0 + j)),
                  pl.BlockSpec((d, bn), lambda j, i: (0, d_ff // bn + tile0 + j)),
                  pl.BlockSpec((bm, LANES), lambda j, i: (i, 0))],
        out_specs=pl.BlockSpec((bm, bn), lambda j, i: (i, j)),
        scratch_shapes=[pltpu.VMEM((d, 2 * bn), BF16)],
        compiler_params=_params("arbitrary", "arbitrary"),
        name="ffn_up",
    )(xg, w_gu, w_gu, ssq)


def _gmlp_kernel(gu_ref, gv_ref, lng_ref, lnb_ref, ws_ref, bs_ref, o_ref, wm_ref,
                 *, rows, n_groups, group_dim):
    @pl.when(pl.program_id(0) == 0)
    def _():
        r = lax.broadcasted_iota(jnp.int32, (CHUNK, CHUNK), 0)
        c = lax.broadcasted_iota(jnp.int32, (CHUNK, CHUNK), 1)
        for g in range(n_groups):
            wm_ref[g] = jnp.where(c <= r, ws_ref[g], 0.0).astype(BF16)

    for ch in range(rows // CHUNK):
        rs = slice(ch * CHUNK, (ch + 1) * CHUNK)
        v = gv_ref[rs, :].astype(F32)
        mu = jnp.mean(v, axis=-1, keepdims=True)
        vc = v - mu
        var = jnp.mean(vc * vc, axis=-1, keepdims=True)
        vn = (vc * lax.rsqrt(var + LN_EPS) * lng_ref[...] + lnb_ref[...]).astype(BF16)
        for g in range(n_groups):
            cs = slice(g * group_dim, (g + 1) * group_dim)
            mixed = jnp.dot(wm_ref[g], vn[:, cs], preferred_element_type=F32) + bs_ref[g]
            o_ref[rs, cs] = (gu_ref[rs, cs].astype(F32) * mixed).astype(o_ref.dtype)


def _gmlp(guv, ln_g, ln_b, w_s, b_s, *, rows=512):
    t = guv.shape[0]
    d = guv.shape[1] // 2
    n_groups = w_s.shape[0]
    return pl.pallas_call(
        functools.partial(_gmlp_kernel, rows=rows, n_groups=n_groups, group_dim=d // n_groups),
        out_shape=jax.ShapeDtypeStruct((t, d), BF16),
        grid=(t // rows,),
        in_specs=[pl.BlockSpec((rows, d), lambda i: (i, 0)),
                  pl.BlockSpec((rows, d), lambda i: (i, 1)),
                  pl.BlockSpec((1, d), lambda i: (0, 0)),
                  pl.BlockSpec((1, d), lambda i: (0, 0)),
                  pl.BlockSpec((n_groups, CHUNK, CHUNK), lambda i: (0, 0, 0)),
                  pl.BlockSpec((n_groups, CHUNK, 1), lambda i: (0, 0, 0))],
        out_specs=pl.BlockSpec((rows, d), lambda i: (i, 0)),
        scratch_shapes=[pltpu.VMEM((n_groups, CHUNK, CHUNK), BF16)],
        compiler_params=_params("arbitrary"),
        name="gmlp_gating",
    )(guv, guv, ln_g.reshape(1, d), ln_b.reshape(1, d), w_s, b_s.reshape(n_groups, CHUNK, 1))


def _lanes(x, n):
    return jnp.concatenate([x] * (n // LANES), axis=1)


def _attn_kernel(q_ref, k_ref, v_ref, lq1_ref, lk1_ref, lq2_ref, lk2_ref, sg_ref, o_ref,
                 s_ref, p_ref, m_ref, l_ref, a_ref, acc_ref,
                 *, blk, strip, n_q_blocks, n_heads, lam_init):
    h = pl.program_id(1)
    slope = jnp.exp2(-8.0 * jnp.full((1, blk), h + 1, F32) / n_heads) * LOG2_E
    k_idx = lax.broadcasted_iota(jnp.int32, (1, blk), 1).astype(F32)
    lam = (jnp.exp(jnp.sum(lq1_ref[...] * lk1_ref[...], axis=-1, keepdims=True))
           - jnp.exp(jnp.sum(lq2_ref[...] * lk2_ref[...], axis=-1, keepdims=True))
           + lam_init)

    def query_block(i, carry):
        _attn_query_block(i, slope, k_idx, lam, q_ref, k_ref, v_ref, sg_ref, o_ref,
                          s_ref, p_ref, m_ref, l_ref, a_ref, acc_ref,
                          blk=blk, strip=strip, n_q_blocks=n_q_blocks, lam_init=lam_init)
        return carry

    lax.fori_loop(0, n_q_blocks, query_block, 0)


def _attn_query_block(i, slope, k_idx, lam, q_ref, k_ref, v_ref, sg_ref, o_ref,
                      s_ref, p_ref, m_ref, l_ref, a_ref, acc_ref,
                      *, blk, strip, n_q_blocks, lam_init):
    q_rows = pl.ds(pl.multiple_of(i * blk, blk), blk)

    m_ref[...] = jnp.full(m_ref.shape, MASK_VALUE, F32)
    l_ref[...] = jnp.zeros(l_ref.shape, F32)
    acc_ref[...] = jnp.zeros(acc_ref.shape, F32)
    p_ref[1] = jnp.zeros(p_ref.shape[1:], BF16)
    a_ref[1] = jnp.ones(a_ref.shape[1:], F32)

    def qk(mp, j):
        off = pl.multiple_of(j * blk, blk)
        cs = slice(mp * HEAD_DIM, (mp + 1) * HEAD_DIM)
        s_ref[mp] = lax.dot_general(q_ref[q_rows, cs], k_ref[pl.ds(off, blk), cs],
                                    (((1,), (1,)), ((), ())),
                                    preferred_element_type=F32)

    def pv(mp, j):
        off = pl.multiple_of(j * blk, blk)
        acc_ref[mp] = (acc_ref[mp] * _lanes(a_ref[mp], V_HEAD_DIM)
                       + jnp.dot(p_ref[mp], v_ref[pl.ds(off, blk), :],
                                 preferred_element_type=F32))

    def softmax_strips(mp, bias, masked):
        for r in range(blk // strip):
            rows = slice(r * strip, (r + 1) * strip)
            s = s_ref[mp, rows, :] + bias
            if masked:
                row = r * strip + lax.broadcasted_iota(jnp.int32, (strip, blk), 0)
                col = lax.broadcasted_iota(jnp.int32, (strip, blk), 1)
                s = jnp.where(col <= row, s, MASK_VALUE)
            m_old = m_ref[mp, rows, :]
            m_new = jnp.maximum(m_old, jnp.max(s, axis=-1, keepdims=True))
            alpha = jnp.exp2(m_old - m_new)
            p = jnp.exp2(s - _lanes(m_new, blk))
            p_sum = p[:, :LANES]
            for c in range(1, blk // LANES):
                p_sum = p_sum + p[:, c * LANES:(c + 1) * LANES]
            l_ref[mp, rows, :] = alpha * l_ref[mp, rows, :] + p_sum
            m_ref[mp, rows, :] = m_new
            a_ref[mp, rows, :] = alpha
            p_ref[mp, rows, :] = p.astype(BF16)

    def step(j, masked):
        bias = slope * (k_idx + jnp.asarray((j - i) * blk, F32))
        pv(1, jnp.maximum(j - 1, 0))
        qk(1, j)
        softmax_strips(0, bias, masked)
        pv(0, j)
        if not masked:
            qk(0, j + 1)
        softmax_strips(1, bias, masked)

    qk(0, 0)
    run = n_q_blocks // 2
    while run >= 1:
        @pl.when(lax.bitwise_and(i, run) != 0)
        def _(run=run):
            start = lax.bitwise_and(i, -2 * run)
            for u in range(run):
                step(start + u, masked=False)
        run //= 2
    step(i, masked=True)
    pv(1, i)

    l1 = jnp.sum(l_ref[0], axis=-1, keepdims=True)
    l2 = jnp.sum(l_ref[1], axis=-1, keepdims=True)
    o = acc_ref[0] / l1 - lam * (acc_ref[1] / l2)
    ms = jnp.mean(o * o, axis=-1, keepdims=True)
    y = (o * lax.rsqrt(ms + LN_EPS) * sg_ref[...]) * (1.0 - lam_init)
    o_ref[q_rows, :] = y.astype(o_ref.dtype)


def _diff_attention(qkv, lq1, lk1, lq2, lk2, subln_g, *, batch, seq, n_heads, lam_init,
                    blk=512, strip=32):
    t = qkv.shape[0]
    nq = seq // blk
    assert nq & (nq - 1) == 0, "the key-block walk assumes a power-of-two block count"
    vec = lambda a: a.reshape(1, -1)
    vec_spec = lambda n: pl.BlockSpec((1, n), lambda b, h: (0, 0))
    return pl.pallas_call(
        functools.partial(_attn_kernel, blk=blk, strip=strip, n_q_blocks=nq, n_heads=n_heads,
                          lam_init=lam_init),
        out_shape=jax.ShapeDtypeStruct((t, n_heads * V_HEAD_DIM), BF16),
        grid=(batch, n_heads),
        in_specs=[pl.BlockSpec((seq, 2 * HEAD_DIM), lambda b, h: (b, h)),
                  pl.BlockSpec((seq, 2 * HEAD_DIM), lambda b, h: (b, n_heads + h)),
                  pl.BlockSpec((seq, V_HEAD_DIM), lambda b, h: (b, 2 * n_heads + h)),
                  vec_spec(HEAD_DIM), vec_spec(HEAD_DIM), vec_spec(HEAD_DIM), vec_spec(HEAD_DIM),
                  vec_spec(V_HEAD_DIM)],
        out_specs=pl.BlockSpec((seq, V_HEAD_DIM), lambda b, h: (b, h)),
        scratch_shapes=[pltpu.VMEM((2, blk, blk), F32),
                        pltpu.VMEM((2, blk, blk), BF16),
                        pltpu.VMEM((2, blk, LANES), F32),
                        pltpu.VMEM((2, blk, LANES), F32),
                        pltpu.VMEM((2, blk, LANES), F32),
                        pltpu.VMEM((2, blk, V_HEAD_DIM), F32)],
        compiler_params=_params("parallel", "parallel"),
        name="diff_attention",
    )(qkv, qkv, qkv, vec(lq1), vec(lk1), vec(lq2), vec(lk2), vec(subln_g))


def kernel(x, p, g_mix, w_in, b_gate, ln_v_g, ln_v_b, w_s, b_s, lambda_q1, lambda_k1, lambda_q2, lambda_k2, subln_g, w_br_a, w_br_b, w_o, g_ffn, w_gu, w_down, g_ple, w_ple_gate, w_ple_proj, g_final):
    batch, seq, d_model = x.shape
    depth = w_in.shape[0]
    t = batch * seq
    d_gmlp = ln_v_g.shape[1]
    d_attn = w_br_b.shape[1]
    n_heads = d_attn // V_HEAD_DIM
    d_qk = n_heads * 2 * HEAD_DIM
    d_ff = w_down.shape[1]
    ple_dim = p.shape[-1]

    xf = x.reshape(t, d_model)
    for layer in range(depth):
        lam_init = 0.8 - 0.6 * math.exp(-0.3 * layer)
        w_in_b = w_in[layer].astype(BF16)

        h = _rmsnorm(xf, g_mix[layer], BF16)
        hx = [(h, d_model, 0)]
        bn = 1024
        guv = _matmul("in_proj_gelu", hx, [(w_in_b, 0)], [], _ep_gelu, dot_pairs=((0, 0),),
                      n_out=2 * d_gmlp, out_dtype=BF16, bm=1024, bn=bn)
        qkv_scale = jnp.concatenate([jnp.full((d_qk,), HEAD_DIM ** -0.5 * LOG2_E, F32),
                                     jnp.ones((d_qk + d_attn,), F32)]).reshape(1, -1)
        qkv = _matmul("in_proj_qkv", hx, [(w_in_b, 2 * d_gmlp // bn)], [("row", qkv_scale, 0)], _ep_colscale,
                      dot_pairs=((0, 0),), n_out=2 * d_qk + d_attn, out_dtype=BF16, bm=1024, bn=bn)
        gates = _matmul("in_proj_gates", hx, [(w_in_b, (2 * d_gmlp + 2 * d_qk + d_attn) // bn)],
                        [("row", b_gate[layer].reshape(1, -1), 0)], _ep_sigmoid_bias,
                        dot_pairs=((0, 0),), n_out=2 * d_model, out_dtype=BF16, bm=1024, bn=bn)

        y_a = _gmlp(guv, ln_v_g[layer], ln_v_b[layer], w_s[layer], b_s[layer])
        y_b = _diff_attention(qkv, lambda_q1[layer], lambda_k1[layer], lambda_q2[layer],
                              lambda_k2[layer], subln_g[layer], batch=batch, seq=seq,
                              n_heads=n_heads, lam_init=lam_init)

        bn = 1024
        merged_a = _matmul("merge_a", [(y_a, d_gmlp, 0)], [(w_br_a[layer].astype(BF16), 0)],
                           [("tile", gates, 0)], _ep_gated, dot_pairs=((0, 0),),
                           n_out=d_model, out_dtype=F32, bm=1024, bn=bn)
        merged = _matmul("merge_b", [(y_b, d_attn, 0)], [(w_br_b[layer].astype(BF16), 0)],
                         [("tile", gates, d_model // bn), ("tile", merged_a, 0)], _ep_gated_add,
                         dot_pairs=((0, 0),), n_out=d_model, out_dtype=BF16, bm=1024, bn=bn)
        xf, xg, ssq = _matmul("out_proj", [(merged, d_model, 0)], [(w_o[layer].astype(BF16), 0)],
                              [("tile", xf, 0)], _ep_residual, dot_pairs=((0, 0),),
                              n_out=d_model, out_dtype=F32, bm=1024, bn=512,
                              norm_gain=g_ffn[layer].reshape(1, -1))

        bn = 256
        n_tiles = d_ff // bn
        tiles_lo = (n_tiles + 1) // 2
        k_lo = tiles_lo * bn
        act_lo = _swiglu_up(xg, ssq, w_gu[layer], tile0=0, n_tiles=tiles_lo, bm=1024, bn=bn)
        act_hi = _swiglu_up(xg, ssq, w_gu[layer], tile0=tiles_lo, n_tiles=n_tiles - tiles_lo,
                            bm=1024, bn=bn)
        xf = _matmul("ffn_down_lo", [(act_lo, k_lo, 0)],
                     [(w_down[layer, :k_lo].astype(BF16), 0)], [("tile", xf, 0)], _ep_residual,
                     dot_pairs=((0, 0),), n_out=d_model, out_dtype=F32, bm=1024, bn=512)
        xf, xg, ssq = _matmul("ffn_down_hi", [(act_hi, d_ff - k_lo, 0)],
                              [(w_down[layer, k_lo:].astype(BF16), 0)], [("tile", xf, 0)],
                              _ep_residual, dot_pairs=((0, 0),), n_out=d_model, out_dtype=F32,
                              bm=1024, bn=512, norm_gain=g_ple[layer].reshape(1, -1))

        ple_bm, ple_bn = (512, 1024) if layer == depth - 1 else (1024, 512)
        xf = _matmul("ple", [(xg, d_model, 0), (p[layer].reshape(t, ple_dim), ple_dim, 0)],
                     [(w_ple_gate[layer].astype(BF16), 0), (w_ple_proj[layer].astype(BF16), 0)],
                     [("tile", xf, 0), ("rowstat", ssq, 0)],
                     functools.partial(_ep_ple_normed, n=d_model), dot_pairs=((0, 0), (1, 1)),
                     n_out=d_model, out_dtype=F32, bm=ple_bm, bn=ple_bn,
                     row_norm_gain=g_final.reshape(1, -1) if layer == depth - 1 else None)

    return xf.reshape(batch, seq, d_model)
```

```python
import functools
import math

import jax
import jax.numpy as jnp
from jax import lax
from jax.experimental import pallas as pl
from jax.experimental.pallas import tpu as pltpu

F32 = jnp.float32
BF16 = jnp.bfloat16

V7X_VMEM_LIMIT_BYTES = 56 * 1024 * 1024

LANES = 128
LOG2_E = math.log2(math.e)

HEAD_DIM = 128
V_HEAD_DIM = 2 * HEAD_DIM
CHUNK = 128
RMS_EPS = 1e-6
LN_EPS = 1e-5
MASK_VALUE = -1e30
TAIL_RUNS = 4


def _params(*semantics):
    return pltpu.CompilerParams(dimension_semantics=semantics,
                                vmem_limit_bytes=V7X_VMEM_LIMIT_BYTES)


def _rmsnorm_kernel(x_ref, g_ref, o_ref, *, eps):
    x = x_ref[...]
    ms = jnp.mean(x * x, axis=-1, keepdims=True)
    o_ref[...] = (x * lax.rsqrt(ms + eps) * g_ref[...]).astype(o_ref.dtype)


def _rmsnorm(x, g, out_dtype, *, rows=512):
    t, d = x.shape
    return pl.pallas_call(
        functools.partial(_rmsnorm_kernel, eps=RMS_EPS),
        out_shape=jax.ShapeDtypeStruct((t, d), out_dtype),
        grid=(t // rows,),
        in_specs=[pl.BlockSpec((rows, d), lambda i: (i, 0)),
                  pl.BlockSpec((1, d), lambda i: (0, 0))],
        out_specs=pl.BlockSpec((rows, d), lambda i: (i, 0)),
        compiler_params=_params("parallel"),
        name="rmsnorm",
    )(x, g.reshape(1, d))


def _mm_kernel(*refs, n_x, n_w, n_extra, dot_pairs, epilogue, emit_norm, row_norm, bn):
    x_refs = refs[:n_x]
    w_refs = refs[n_x:n_x + n_w]
    extra_refs = refs[n_x + n_w:n_x + n_w + n_extra]
    out_refs = refs[n_x + n_w + n_extra:]
    accs = [jnp.dot(x_refs[a][...].astype(BF16), w_refs[b][...],
                    preferred_element_type=F32) for a, b in dot_pairs]
    y = epilogue(accs, extra_refs)
    if row_norm:
        o_ref, ssq_ref = out_refs
        gain_ref = extra_refs[-1]
        j = pl.program_id(1)
        o_ref[:, pl.ds(pl.multiple_of(j * bn, bn), bn)] = y
        part = jnp.broadcast_to(jnp.sum(y * y, axis=-1, keepdims=True), ssq_ref.shape)

        @pl.when(j == 0)
        def _():
            ssq_ref[...] = part

        @pl.when(j != 0)
        def _():
            ssq_ref[...] += part

        @pl.when(j == pl.num_programs(1) - 1)
        def _():
            r = _rms_scale(ssq_ref, o_ref.shape[1])
            for c in range(o_ref.shape[1] // LANES):
                cs = slice(c * LANES, (c + 1) * LANES)
                o_ref[:, cs] = o_ref[:, cs] * r * gain_ref[:, cs]
        return
    out_refs[0][...] = y.astype(out_refs[0].dtype)
    if emit_norm:
        gain_ref = extra_refs[-1]
        xg_ref, ssq_ref = out_refs[1:]
        xg_ref[...] = (y * gain_ref[...]).astype(xg_ref.dtype)
        part = jnp.broadcast_to(jnp.sum(y * y, axis=-1, keepdims=True), ssq_ref.shape)
        j = pl.program_id(1)

        @pl.when(j == 0)
        def _():
            ssq_ref[...] = part

        @pl.when(j != 0)
        def _():
            ssq_ref[...] += part


def _matmul(name, xs, ws, extras, epilogue, *, dot_pairs, n_out, out_dtype, bm, bn,
            norm_gain=None, row_norm_gain=None):
    t = xs[0][0].shape[0]
    emit_norm = norm_gain is not None
    row_norm = row_norm_gain is not None
    if emit_norm:
        extras = list(extras) + [("row", norm_gain, 0)]
    if row_norm:
        extras = list(extras) + [("fullrow", row_norm_gain, 0)]
    in_specs, operands = [], []
    for arr, k, cb in xs:
        in_specs.append(pl.BlockSpec((bm, k), lambda i, j, cb=cb: (i, cb)))
        operands.append(arr)
    for arr, cb0 in ws:
        k = arr.shape[0]
        in_specs.append(pl.BlockSpec((k, bn), lambda i, j, cb0=cb0: (0, cb0 + j)))
        operands.append(arr)
    for kind, arr, cb0 in extras:
        if kind == "tile":
            in_specs.append(pl.BlockSpec((bm, bn), lambda i, j, cb0=cb0: (i, cb0 + j)))
        elif kind == "row":
            in_specs.append(pl.BlockSpec((1, bn), lambda i, j, cb0=cb0: (0, cb0 + j)))
        elif kind == "rowstat":
            in_specs.append(pl.BlockSpec((bm, LANES), lambda i, j: (i, 0)))
        else:
            in_specs.append(pl.BlockSpec((1, n_out), lambda i, j: (0, 0)))
        operands.append(arr)
    tile_spec = pl.BlockSpec((bm, bn), lambda i, j: (i, j))
    stat_spec = pl.BlockSpec((bm, LANES), lambda i, j: (i, 0))
    out_shape = jax.ShapeDtypeStruct((t, n_out), out_dtype)
    out_specs = tile_spec
    scratch_shapes = []
    if emit_norm:
        out_shape = (out_shape, jax.ShapeDtypeStruct((t, n_out), BF16),
                     jax.ShapeDtypeStruct((t, LANES), F32))
        out_specs = (tile_spec, tile_spec, stat_spec)
    if row_norm:
        assert out_dtype == F32 and not emit_norm
        out_specs = pl.BlockSpec((bm, n_out), lambda i, j: (i, 0))
        scratch_shapes = [pltpu.VMEM((bm, LANES), F32)]
    sequential_cols = emit_norm or row_norm
    return pl.pallas_call(
        functools.partial(_mm_kernel, n_x=len(xs), n_w=len(ws), n_extra=len(extras),
                          dot_pairs=dot_pairs, epilogue=epilogue, emit_norm=emit_norm,
                          row_norm=row_norm, bn=bn),
        out_shape=out_shape,
        grid=(t // bm, n_out // bn),
        in_specs=in_specs,
        out_specs=out_specs,
        scratch_shapes=scratch_shapes,
        compiler_params=_params("parallel", "arbitrary" if sequential_cols else "parallel"),
        name=name,
    )(*operands)


def _rms_scale(ssq_ref, n):
    return lax.rsqrt(ssq_ref[:, :1] / n + RMS_EPS)


def _gelu_exact(x):
    return 0.5 * x * (1.0 + lax.erf(x * math.sqrt(0.5)))


def _ep_gelu(accs, extras):
    return _gelu_exact(accs[0])


def _ep_colscale(accs, extras):
    return accs[0] * extras[0][...]


def _ep_sigmoid_bias(accs, extras):
    return jax.nn.sigmoid(accs[0] + extras[0][...])


def _ep_gated(accs, extras):
    return extras[0][...].astype(F32) * accs[0]


def _ep_gated_add(accs, extras):
    return extras[1][...] + extras[0][...].astype(F32) * accs[0]


def _ep_residual(accs, extras):
    return extras[0][...] + accs[0]


def _ep_ple_normed(accs, extras, *, n):
    r = _rms_scale(extras[1], n)
    return extras[0][...] + jax.nn.sigmoid(r * accs[0]) * accs[1]


def _swiglu_up_kernel(x_ref, wg_ref, wu_ref, ssq_ref, o_ref, w_ref, *, bn, n):
    @pl.when(pl.program_id(1) == 0)
    def _():
        w_ref[:, :bn] = wg_ref[...].astype(BF16)
        w_ref[:, bn:] = wu_ref[...].astype(BF16)

    acc = jnp.dot(x_ref[...], w_ref[...], preferred_element_type=F32)
    r = _rms_scale(ssq_ref, n)
    o_ref[...] = (jax.nn.silu(r * acc[:, :bn]) * (r * acc[:, bn:])).astype(o_ref.dtype)


def _swiglu_up(xg, ssq, w_gu, *, bm, bn):
    t, d = xg.shape
    d_ff = w_gu.shape[1] // 2
    return pl.pallas_call(
        functools.partial(_swiglu_up_kernel, bn=bn, n=d),
        out_shape=jax.ShapeDtypeStruct((t, d_ff), BF16),
        grid=(d_ff // bn, t // bm),
        in_specs=[pl.BlockSpec((bm, d), lambda j, i: (i, 0)),
                  pl.BlockSpec((d, bn), lambda j, i: (0, j)),
                  pl.BlockSpec((d, bn), lambda j, i: (0, d_ff // bn + j)),
                  pl.BlockSpec((bm, LANES), lambda j, i: (i, 0))],
        out_specs=pl.BlockSpec((bm, bn), lambda j, i: (i, j)),
        scratch_shapes=[pltpu.VMEM((d, 2 * bn), BF16)],
        compiler_params=_params("arbitrary", "arbitrary"),
        name="ffn_up",
    )(xg, w_gu, w_gu, ssq)


def _gmlp_kernel(gu_ref, gv_ref, lng_ref, lnb_ref, ws_ref, bs_ref, o_ref, wm_ref,
                 *, rows, n_groups, group_dim):
    @pl.when(pl.program_id(0) == 0)
    def _():
        r = lax.broadcasted_iota(jnp.int32, (CHUNK, CHUNK), 0)
        c = lax.broadcasted_iota(jnp.int32, (CHUNK, CHUNK), 1)
        for g in range(n_groups):
            wm_ref[g] = jnp.where(c <= r, ws_ref[g], 0.0).astype(BF16)

    for ch in range(rows // CHUNK):
        rs = slice(ch * CHUNK, (ch + 1) * CHUNK)
        v = gv_ref[rs, :].astype(F32)
        mu = jnp.mean(v, axis=-1, keepdims=True)
        vc = v - mu
        var = jnp.mean(vc * vc, axis=-1, keepdims=True)
        vn = (vc * lax.rsqrt(var + LN_EPS) * lng_ref[...] + lnb_ref[...]).astype(BF16)
        for g in range(n_groups):
            cs = slice(g * group_dim, (g + 1) * group_dim)
            mixed = jnp.dot(wm_ref[g], vn[:, cs], preferred_element_type=F32) + bs_ref[g]
            o_ref[rs, cs] = (gu_ref[rs, cs].astype(F32) * mixed).astype(o_ref.dtype)


def _gmlp(guv, ln_g, ln_b, w_s, b_s, *, rows=512):
    t = guv.shape[0]
    d = guv.shape[1] // 2
    n_groups = w_s.shape[0]
    return pl.pallas_call(
        functools.partial(_gmlp_kernel, rows=rows, n_groups=n_groups, group_dim=d // n_groups),
        out_shape=jax.ShapeDtypeStruct((t, d), BF16),
        grid=(t // rows,),
        in_specs=[pl.BlockSpec((rows, d), lambda i: (i, 0)),
                  pl.BlockSpec((rows, d), lambda i: (i, 1)),
                  pl.BlockSpec((1, d), lambda i: (0, 0)),
                  pl.BlockSpec((1, d), lambda i: (0, 0)),
                  pl.BlockSpec((n_groups, CHUNK, CHUNK), lambda i: (0, 0, 0)),
                  pl.BlockSpec((n_groups, CHUNK, 1), lambda i: (0, 0, 0))],
        out_specs=pl.BlockSpec((rows, d), lambda i: (i, 0)),
        scratch_shapes=[pltpu.VMEM((n_groups, CHUNK, CHUNK), BF16)],
        compiler_params=_params("arbitrary"),
        name="gmlp_gating",
    )(guv, guv, ln_g.reshape(1, d), ln_b.reshape(1, d), w_s, b_s.reshape(n_groups, CHUNK, 1))


def _lanes(x, n):
    return jnp.concatenate([x] * (n // LANES), axis=1)


def _attn_kernel(q_ref, k_ref, v_ref, lq1_ref, lk1_ref, lq2_ref, lk2_ref, sg_ref, o_ref,
                 s_ref, p_ref, m_ref, l_ref, a_ref, acc_ref,
                 *, blk, strip, n_q_blocks, n_heads, lam_init):
    h = pl.program_id(1)
    slope = jnp.exp2(-8.0 * jnp.full((1, blk), h + 1, F32) / n_heads) * LOG2_E
    k_idx = lax.broadcasted_iota(jnp.int32, (1, blk), 1).astype(F32)
    lam = (jnp.exp(jnp.sum(lq1_ref[...] * lk1_ref[...], axis=-1, keepdims=True))
           - jnp.exp(jnp.sum(lq2_ref[...] * lk2_ref[...], axis=-1, keepdims=True))
           + lam_init)

    def query_block(i, carry):
        _attn_query_block(i, slope, k_idx, lam, q_ref, k_ref, v_ref, sg_ref, o_ref,
                          s_ref, p_ref, m_ref, l_ref, a_ref, acc_ref,
                          blk=blk, strip=strip, n_q_blocks=n_q_blocks, lam_init=lam_init)
        return carry

    lax.fori_loop(0, n_q_blocks, query_block, 0)


def _attn_query_block(i, slope, k_idx, lam, q_ref, k_ref, v_ref, sg_ref, o_ref,
                      s_ref, p_ref, m_ref, l_ref, a_ref, acc_ref,
                      *, blk, strip, n_q_blocks, lam_init):
    q_rows = pl.ds(pl.multiple_of(i * blk, blk), blk)

    m_ref[...] = jnp.full(m_ref.shape, MASK_VALUE, F32)
    l_ref[...] = jnp.zeros(l_ref.shape, F32)
    acc_ref[...] = jnp.zeros(acc_ref.shape, F32)
    p_ref[1] = jnp.zeros(p_ref.shape[1:], BF16)
    a_ref[1] = jnp.ones(a_ref.shape[1:], F32)

    def qk(mp, j):
        off = pl.multiple_of(j * blk, blk)
        cs = slice(mp * HEAD_DIM, (mp + 1) * HEAD_DIM)
        s_ref[mp] = lax.dot_general(q_ref[q_rows, cs], k_ref[pl.ds(off, blk), cs],
                                    (((1,), (1,)), ((), ())),
                                    preferred_element_type=F32)

    def pv(mp, j):
        off = pl.multiple_of(j * blk, blk)
        acc_ref[mp] = (acc_ref[mp] * _lanes(a_ref[mp], V_HEAD_DIM)
                       + jnp.dot(p_ref[mp], v_ref[pl.ds(off, blk), :],
                                 preferred_element_type=F32))

    def softmax_strips(mp, bias, masked):
        for r in range(blk // strip):
            rows = slice(r * strip, (r + 1) * strip)
            s = s_ref[mp, rows, :] + bias
            if masked:
                row = r * strip + lax.broadcasted_iota(jnp.int32, (strip, blk), 0)
                col = lax.broadcasted_iota(jnp.int32, (strip, blk), 1)
                s = jnp.where(col <= row, s, MASK_VALUE)
            m_old = m_ref[mp, rows, :]
            m_new = jnp.maximum(m_old, jnp.max(s, axis=-1, keepdims=True))
            alpha = jnp.exp2(m_old - m_new)
            p = jnp.exp2(s - _lanes(m_new, blk))
            p_sum = p[:, :LANES]
            for c in range(1, blk // LANES):
                p_sum = p_sum + p[:, c * LANES:(c + 1) * LANES]
            l_ref[mp, rows, :] = alpha * l_ref[mp, rows, :] + p_sum
            m_ref[mp, rows, :] = m_new
            a_ref[mp, rows, :] = alpha
            p_ref[mp, rows, :] = p.astype(BF16)

    def step(j, masked):
        bias = slope * (k_idx + jnp.asarray((j - i) * blk, F32))
        pv(1, jnp.maximum(j - 1, 0))
        qk(1, j)
        softmax_strips(0, bias, masked)
        pv(0, j)
        if not masked:
            qk(0, j + 1)
        softmax_strips(1, bias, masked)

    qk(0, 0)
    run = n_q_blocks // 2
    while run >= TAIL_RUNS:
        @pl.when(lax.bitwise_and(i, run) != 0)
        def _(run=run):
            start = lax.bitwise_and(i, -2 * run)
            for u in range(run):
                step(start + u, masked=False)
        run //= 2

    def tail(n_rest):
        for u in range(n_rest):
            step(i - n_rest + u, masked=False)
        step(i, masked=True)
        pv(1, i)
        l1 = jnp.sum(l_ref[0], axis=-1, keepdims=True)
        l2 = jnp.sum(l_ref[1], axis=-1, keepdims=True)
        o = acc_ref[0] / l1 - lam * (acc_ref[1] / l2)
        ms = jnp.mean(o * o, axis=-1, keepdims=True)
        y = (o * lax.rsqrt(ms + LN_EPS) * sg_ref[...]) * (1.0 - lam_init)
        o_ref[q_rows, :] = y.astype(o_ref.dtype)

    for n_rest in range(TAIL_RUNS):
        @pl.when(lax.bitwise_and(i, TAIL_RUNS - 1) == n_rest)
        def _(n_rest=n_rest):
            tail(n_rest)


def _diff_attention(qkv, lq1, lk1, lq2, lk2, subln_g, *, batch, seq, n_heads, lam_init,
                    blk=512, strip=32):
    t = qkv.shape[0]
    nq = seq // blk
    assert nq & (nq - 1) == 0, "the key-block walk assumes a power-of-two block count"
    vec = lambda a: a.reshape(1, -1)
    vec_spec = lambda n: pl.BlockSpec((1, n), lambda b, h: (0, 0))
    return pl.pallas_call(
        functools.partial(_attn_kernel, blk=blk, strip=strip, n_q_blocks=nq, n_heads=n_heads,
                          lam_init=lam_init),
        out_shape=jax.ShapeDtypeStruct((t, n_heads * V_HEAD_DIM), BF16),
        grid=(batch, n_heads),
        in_specs=[pl.BlockSpec((seq, 2 * HEAD_DIM), lambda b, h: (b, h)),
                  pl.BlockSpec((seq, 2 * HEAD_DIM), lambda b, h: (b, n_heads + h)),
                  pl.BlockSpec((seq, V_HEAD_DIM), lambda b, h: (b, 2 * n_heads + h)),
                  vec_spec(HEAD_DIM), vec_spec(HEAD_DIM), vec_spec(HEAD_DIM), vec_spec(HEAD_DIM),
                  vec_spec(V_HEAD_DIM)],
        out_specs=pl.BlockSpec((seq, V_HEAD_DIM), lambda b, h: (b, h)),
        scratch_shapes=[pltpu.VMEM((2, blk, blk), F32),
                        pltpu.VMEM((2, blk, blk), BF16),
                        pltpu.VMEM((2, blk, LANES), F32),
                        pltpu.VMEM((2, blk, LANES), F32),
                        pltpu.VMEM((2, blk, LANES), F32),
                        pltpu.VMEM((2, blk, V_HEAD_DIM), F32)],
        compiler_params=_params("parallel", "parallel"),
        name="diff_attention",
    )(qkv, qkv, qkv, vec(lq1), vec(lk1), vec(lq2), vec(lk2), vec(subln_g))


def kernel(x, p, g_mix, w_in, b_gate, ln_v_g, ln_v_b, w_s, b_s, lambda_q1, lambda_k1, lambda_q2, lambda_k2, subln_g, w_br_a, w_br_b, w_o, g_ffn, w_gu, w_down, g_ple, w_ple_gate, w_ple_proj, g_final):
    batch, seq, d_model = x.shape
    depth = w_in.shape[0]
    t = batch * seq
    d_gmlp = ln_v_g.shape[1]
    d_attn = w_br_b.shape[1]
    n_heads = d_attn // V_HEAD_DIM
    d_qk = n_heads * 2 * HEAD_DIM
    d_ff = w_down.shape[1]
    ple_dim = p.shape[-1]

    xf = x.reshape(t, d_model)
    for layer in range(depth):
        lam_init = 0.8 - 0.6 * math.exp(-0.3 * layer)
        w_in_b = w_in[layer].astype(BF16)

        h = _rmsnorm(xf, g_mix[layer], BF16)
        hx = [(h, d_model, 0)]
        bn = 1024
        guv = _matmul("in_proj_gelu", hx, [(w_in_b, 0)], [], _ep_gelu, dot_pairs=((0, 0),),
                      n_out=2 * d_gmlp, out_dtype=BF16, bm=1024, bn=bn)
        qkv_scale = jnp.concatenate([jnp.full((d_qk,), HEAD_DIM ** -0.5 * LOG2_E, F32),
                                     jnp.ones((d_qk + d_attn,), F32)]).reshape(1, -1)
        qkv = _matmul("in_proj_qkv", hx, [(w_in_b, 2 * d_gmlp // bn)], [("row", qkv_scale, 0)], _ep_colscale,
                      dot_pairs=((0, 0),), n_out=2 * d_qk + d_attn, out_dtype=BF16, bm=1024, bn=bn)
        gates = _matmul("in_proj_gates", hx, [(w_in_b, (2 * d_gmlp + 2 * d_qk + d_attn) // bn)],
                        [("row", b_gate[layer].reshape(1, -1), 0)], _ep_sigmoid_bias,
                        dot_pairs=((0, 0),), n_out=2 * d_model, out_dtype=BF16, bm=1024, bn=bn)

        y_a = _gmlp(guv, ln_v_g[layer], ln_v_b[layer], w_s[layer], b_s[layer])
        y_b = _diff_attention(qkv, lambda_q1[layer], lambda_k1[layer], lambda_q2[layer],
                              lambda_k2[layer], subln_g[layer], batch=batch, seq=seq,
                              n_heads=n_heads, lam_init=lam_init)

        bn = 1024
        merged_a = _matmul("merge_a", [(y_a, d_gmlp, 0)], [(w_br_a[layer].astype(BF16), 0)],
                           [("tile", gates, 0)], _ep_gated, dot_pairs=((0, 0),),
                           n_out=d_model, out_dtype=F32, bm=1024, bn=bn)
        merged = _matmul("merge_b", [(y_b, d_attn, 0)], [(w_br_b[layer].astype(BF16), 0)],
                         [("tile", gates, d_model // bn), ("tile", merged_a, 0)], _ep_gated_add,
                         dot_pairs=((0, 0),), n_out=d_model, out_dtype=BF16, bm=1024, bn=bn)
        xf, xg, ssq = _matmul("out_proj", [(merged, d_model, 0)], [(w_o[layer].astype(BF16), 0)],
                              [("tile", xf, 0)], _ep_residual, dot_pairs=((0, 0),),
                              n_out=d_model, out_dtype=F32, bm=1024, bn=512,
                              norm_gain=g_ffn[layer].reshape(1, -1))

        act = _swiglu_up(xg, ssq, w_gu[layer], bm=1024, bn=256)
        xf, xg, ssq = _matmul("ffn_down", [(act, d_ff, 0)], [(w_down[layer].astype(BF16), 0)],
                              [("tile", xf, 0)], _ep_residual, dot_pairs=((0, 0),),
                              n_out=d_model, out_dtype=F32, bm=512, bn=512,
                              norm_gain=g_ple[layer].reshape(1, -1))

        ple_bm, ple_bn = (512, 1024) if layer == depth - 1 else (1024, 512)
        xf = _matmul("ple", [(xg, d_model, 0), (p[layer].reshape(t, ple_dim), ple_dim, 0)],
                     [(w_ple_gate[layer].astype(BF16), 0), (w_ple_proj[layer].astype(BF16), 0)],
                     [("tile", xf, 0), ("rowstat", ssq, 0)],
                     functools.partial(_ep_ple_normed, n=d_model), dot_pairs=((0, 0), (1, 1)),
                     n_out=d_model, out_dtype=F32, bm=ple_bm, bn=ple_bn,
                     row_norm_gain=g_final.reshape(1, -1) if layer == depth - 1 else None)

    return xf.reshape(batch, seq, d_model)
```

```python
import functools
import math

import jax
import jax.numpy as jnp
from jax import lax
from jax.experimental import pallas as pl
from jax.experimental.pallas import tpu as pltpu

F32 = jnp.float32
BF16 = jnp.bfloat16

V7X_VMEM_LIMIT_BYTES = 56 * 1024 * 1024

LANES = 128
LOG2_E = math.log2(math.e)

HEAD_DIM = 128
V_HEAD_DIM = 2 * HEAD_DIM
CHUNK = 128
RMS_EPS = 1e-6
LN_EPS = 1e-5
MASK_VALUE = -1e30
TAIL_RUNS = 8


def _params(*semantics):
    return pltpu.CompilerParams(dimension_semantics=semantics,
                                vmem_limit_bytes=V7X_VMEM_LIMIT_BYTES)


def _rmsnorm_kernel(x_ref, g_ref, o_ref, *, eps):
    x = x_ref[...]
    ms = jnp.mean(x * x, axis=-1, keepdims=True)
    o_ref[...] = (x * lax.rsqrt(ms + eps) * g_ref[...]).astype(o_ref.dtype)


def _rmsnorm(x, g, out_dtype, *, rows=512):
    t, d = x.shape
    return pl.pallas_call(
        functools.partial(_rmsnorm_kernel, eps=RMS_EPS),
        out_shape=jax.ShapeDtypeStruct((t, d), out_dtype),
        grid=(t // rows,),
        in_specs=[pl.BlockSpec((rows, d), lambda i: (i, 0)),
                  pl.BlockSpec((1, d), lambda i: (0, 0))],
        out_specs=pl.BlockSpec((rows, d), lambda i: (i, 0)),
        compiler_params=_params("parallel"),
        name="rmsnorm",
    )(x, g.reshape(1, d))


def _mm_kernel(*refs, n_x, n_w, n_extra, dot_pairs, epilogue, emit_norm, row_norm, bn):
    x_refs = refs[:n_x]
    w_refs = refs[n_x:n_x + n_w]
    extra_refs = refs[n_x + n_w:n_x + n_w + n_extra]
    out_refs = refs[n_x + n_w + n_extra:]
    accs = [jnp.dot(x_refs[a][...].astype(BF16), w_refs[b][...],
                    preferred_element_type=F32) for a, b in dot_pairs]
    y = epilogue(accs, extra_refs)
    if row_norm:
        o_ref, ssq_ref = out_refs
        gain_ref = extra_refs[-1]
        j = pl.program_id(1)
        o_ref[:, pl.ds(pl.multiple_of(j * bn, bn), bn)] = y
        part = jnp.broadcast_to(jnp.sum(y * y, axis=-1, keepdims=True), ssq_ref.shape)

        @pl.when(j == 0)
        def _():
            ssq_ref[...] = part

        @pl.when(j != 0)
        def _():
            ssq_ref[...] += part

        @pl.when(j == pl.num_programs(1) - 1)
        def _():
            r = _rms_scale(ssq_ref, o_ref.shape[1])
            for c in range(o_ref.shape[1] // LANES):
                cs = slice(c * LANES, (c + 1) * LANES)
                o_ref[:, cs] = o_ref[:, cs] * r * gain_ref[:, cs]
        return
    out_refs[0][...] = y.astype(out_refs[0].dtype)
    if emit_norm:
        gain_ref = extra_refs[-1]
        xg_ref, ssq_ref = out_refs[1:]
        xg_ref[...] = (y * gain_ref[...]).astype(xg_ref.dtype)
        part = jnp.broadcast_to(jnp.sum(y * y, axis=-1, keepdims=True), ssq_ref.shape)
        j = pl.program_id(1)

        @pl.when(j == 0)
        def _():
            ssq_ref[...] = part

        @pl.when(j != 0)
        def _():
            ssq_ref[...] += part


def _matmul(name, xs, ws, extras, epilogue, *, dot_pairs, n_out, out_dtype, bm, bn,
            norm_gain=None, row_norm_gain=None):
    t = xs[0][0].shape[0]
    emit_norm = norm_gain is not None
    row_norm = row_norm_gain is not None
    if emit_norm:
        extras = list(extras) + [("row", norm_gain, 0)]
    if row_norm:
        extras = list(extras) + [("fullrow", row_norm_gain, 0)]
    in_specs, operands = [], []
    for arr, k, cb in xs:
        in_specs.append(pl.BlockSpec((bm, k), lambda i, j, cb=cb: (i, cb)))
        operands.append(arr)
    for arr, cb0 in ws:
        k = arr.shape[0]
        in_specs.append(pl.BlockSpec((k, bn), lambda i, j, cb0=cb0: (0, cb0 + j)))
        operands.append(arr)
    for kind, arr, cb0 in extras:
        if kind == "tile":
            in_specs.append(pl.BlockSpec((bm, bn), lambda i, j, cb0=cb0: (i, cb0 + j)))
        elif kind == "row":
            in_specs.append(pl.BlockSpec((1, bn), lambda i, j, cb0=cb0: (0, cb0 + j)))
        elif kind == "rowstat":
            in_specs.append(pl.BlockSpec((bm, LANES), lambda i, j: (i, 0)))
        else:
            in_specs.append(pl.BlockSpec((1, n_out), lambda i, j: (0, 0)))
        operands.append(arr)
    tile_spec = pl.BlockSpec((bm, bn), lambda i, j: (i, j))
    stat_spec = pl.BlockSpec((bm, LANES), lambda i, j: (i, 0))
    out_shape = jax.ShapeDtypeStruct((t, n_out), out_dtype)
    out_specs = tile_spec
    scratch_shapes = []
    if emit_norm:
        out_shape = (out_shape, jax.ShapeDtypeStruct((t, n_out), BF16),
                     jax.ShapeDtypeStruct((t, LANES), F32))
        out_specs = (tile_spec, tile_spec, stat_spec)
    if row_norm:
        assert out_dtype == F32 and not emit_norm
        out_specs = pl.BlockSpec((bm, n_out), lambda i, j: (i, 0))
        scratch_shapes = [pltpu.VMEM((bm, LANES), F32)]
    sequential_cols = emit_norm or row_norm
    return pl.pallas_call(
        functools.partial(_mm_kernel, n_x=len(xs), n_w=len(ws), n_extra=len(extras),
                          dot_pairs=dot_pairs, epilogue=epilogue, emit_norm=emit_norm,
                          row_norm=row_norm, bn=bn),
        out_shape=out_shape,
        grid=(t // bm, n_out // bn),
        in_specs=in_specs,
        out_specs=out_specs,
        scratch_shapes=scratch_shapes,
        compiler_params=_params("parallel", "arbitrary" if sequential_cols else "parallel"),
        name=name,
    )(*operands)


def _rms_scale(ssq_ref, n):
    return lax.rsqrt(ssq_ref[:, :1] / n + RMS_EPS)


def _gelu_exact(x):
    return 0.5 * x * (1.0 + lax.erf(x * math.sqrt(0.5)))


def _ep_gelu(accs, extras):
    return _gelu_exact(accs[0])


def _ep_colscale(accs, extras):
    return accs[0] * extras[0][...]


def _ep_sigmoid_bias(accs, extras):
    return jax.nn.sigmoid(accs[0] + extras[0][...])


def _ep_gated(accs, extras):
    return extras[0][...].astype(F32) * accs[0]


def _ep_gated_add(accs, extras):
    return extras[1][...] + extras[0][...].astype(F32) * accs[0]


def _ep_residual(accs, extras):
    return extras[0][...] + accs[0]


def _ep_ple_normed(accs, extras, *, n):
    r = _rms_scale(extras[1], n)
    return extras[0][...] + jax.nn.sigmoid(r * accs[0]) * accs[1]


def _swiglu_up_kernel(x_ref, wg_ref, wu_ref, ssq_ref, o_ref, w_ref, *, bn, n):
    @pl.when(pl.program_id(1) == 0)
    def _():
        w_ref[:, :bn] = wg_ref[...].astype(BF16)
        w_ref[:, bn:] = wu_ref[...].astype(BF16)

    acc = jnp.dot(x_ref[...], w_ref[...], preferred_element_type=F32)
    r = _rms_scale(ssq_ref, n)
    o_ref[...] = (jax.nn.silu(r * acc[:, :bn]) * (r * acc[:, bn:])).astype(o_ref.dtype)


def _swiglu_up(xg, ssq, w_gu, *, bm, bn):
    t, d = xg.shape
    d_ff = w_gu.shape[1] // 2
    return pl.pallas_call(
        functools.partial(_swiglu_up_kernel, bn=bn, n=d),
        out_shape=jax.ShapeDtypeStruct((t, d_ff), BF16),
        grid=(d_ff // bn, t // bm),
        in_specs=[pl.BlockSpec((bm, d), lambda j, i: (i, 0)),
                  pl.BlockSpec((d, bn), lambda j, i: (0, j)),
                  pl.BlockSpec((d, bn), lambda j, i: (0, d_ff // bn + j)),
                  pl.BlockSpec((bm, LANES), lambda j, i: (i, 0))],
        out_specs=pl.BlockSpec((bm, bn), lambda j, i: (i, j)),
        scratch_shapes=[pltpu.VMEM((d, 2 * bn), BF16)],
        compiler_params=_params("arbitrary", "arbitrary"),
        name="ffn_up",
    )(xg, w_gu, w_gu, ssq)


def _gmlp_kernel(gu_ref, gv_ref, lng_ref, lnb_ref, ws_ref, bs_ref, o_ref, wm_ref,
                 *, rows, n_groups, group_dim):
    @pl.when(pl.program_id(0) == 0)
    def _():
        r = lax.broadcasted_iota(jnp.int32, (CHUNK, CHUNK), 0)
        c = lax.broadcasted_iota(jnp.int32, (CHUNK, CHUNK), 1)
        for g in range(n_groups):
            wm_ref[g] = jnp.where(c <= r, ws_ref[g], 0.0).astype(BF16)

    for ch in range(rows // CHUNK):
        rs = slice(ch * CHUNK, (ch + 1) * CHUNK)
        v = gv_ref[rs, :].astype(F32)
        mu = jnp.mean(v, axis=-1, keepdims=True)
        vc = v - mu
        var = jnp.mean(vc * vc, axis=-1, keepdims=True)
        vn = (vc * lax.rsqrt(var + LN_EPS) * lng_ref[...] + lnb_ref[...]).astype(BF16)
        for g in range(n_groups):
            cs = slice(g * group_dim, (g + 1) * group_dim)
            mixed = jnp.dot(wm_ref[g], vn[:, cs], preferred_element_type=F32) + bs_ref[g]
            o_ref[rs, cs] = (gu_ref[rs, cs].astype(F32) * mixed).astype(o_ref.dtype)


def _gmlp(guv, ln_g, ln_b, w_s, b_s, *, rows=512):
    t = guv.shape[0]
    d = guv.shape[1] // 2
    n_groups = w_s.shape[0]
    return pl.pallas_call(
        functools.partial(_gmlp_kernel, rows=rows, n_groups=n_groups, group_dim=d // n_groups),
        out_shape=jax.ShapeDtypeStruct((t, d), BF16),
        grid=(t // rows,),
        in_specs=[pl.BlockSpec((rows, d), lambda i: (i, 0)),
                  pl.BlockSpec((rows, d), lambda i: (i, 1)),
                  pl.BlockSpec((1, d), lambda i: (0, 0)),
                  pl.BlockSpec((1, d), lambda i: (0, 0)),
                  pl.BlockSpec((n_groups, CHUNK, CHUNK), lambda i: (0, 0, 0)),
                  pl.BlockSpec((n_groups, CHUNK, 1), lambda i: (0, 0, 0))],
        out_specs=pl.BlockSpec((rows, d), lambda i: (i, 0)),
        scratch_shapes=[pltpu.VMEM((n_groups, CHUNK, CHUNK), BF16)],
        compiler_params=_params("arbitrary"),
        name="gmlp_gating",
    )(guv, guv, ln_g.reshape(1, d), ln_b.reshape(1, d), w_s, b_s.reshape(n_groups, CHUNK, 1))


def _lanes(x, n):
    return jnp.concatenate([x] * (n // LANES), axis=1)


def _attn_kernel(q_ref, k_ref, v_ref, lq1_ref, lk1_ref, lq2_ref, lk2_ref, sg_ref, o_ref,
                 s_ref, p_ref, m_ref, l_ref, a_ref, acc_ref,
                 *, blk, strip, n_q_blocks, n_heads, lam_init):
    h = pl.program_id(1)
    slope = jnp.exp2(-8.0 * jnp.full((1, blk), h + 1, F32) / n_heads) * LOG2_E
    k_idx = lax.broadcasted_iota(jnp.int32, (1, blk), 1).astype(F32)
    lam = (jnp.exp(jnp.sum(lq1_ref[...] * lk1_ref[...], axis=-1, keepdims=True))
           - jnp.exp(jnp.sum(lq2_ref[...] * lk2_ref[...], axis=-1, keepdims=True))
           + lam_init)

    def query_block(i, carry):
        _attn_query_block(i, slope, k_idx, lam, q_ref, k_ref, v_ref, sg_ref, o_ref,
                          s_ref, p_ref, m_ref, l_ref, a_ref, acc_ref,
                          blk=blk, strip=strip, n_q_blocks=n_q_blocks, lam_init=lam_init)
        return carry

    lax.fori_loop(0, n_q_blocks, query_block, 0)


def _attn_query_block(i, slope, k_idx, lam, q_ref, k_ref, v_ref, sg_ref, o_ref,
                      s_ref, p_ref, m_ref, l_ref, a_ref, acc_ref,
                      *, blk, strip, n_q_blocks, lam_init):
    q_rows = pl.ds(pl.multiple_of(i * blk, blk), blk)

    m_ref[...] = jnp.full(m_ref.shape, MASK_VALUE, F32)
    l_ref[...] = jnp.zeros(l_ref.shape, F32)
    acc_ref[...] = jnp.zeros(acc_ref.shape, F32)
    p_ref[1] = jnp.zeros(p_ref.shape[1:], BF16)
    a_ref[1] = jnp.ones(a_ref.shape[1:], F32)

    def qk(mp, j):
        off = pl.multiple_of(j * blk, blk)
        cs = slice(mp * HEAD_DIM, (mp + 1) * HEAD_DIM)
        s_ref[mp] = lax.dot_general(q_ref[q_rows, cs], k_ref[pl.ds(off, blk), cs],
                                    (((1,), (1,)), ((), ())),
                                    preferred_element_type=F32)

    def pv(mp, j):
        off = pl.multiple_of(j * blk, blk)
        acc_ref[mp] = (acc_ref[mp] * _lanes(a_ref[mp], V_HEAD_DIM)
                       + jnp.dot(p_ref[mp], v_ref[pl.ds(off, blk), :],
                                 preferred_element_type=F32))

    def softmax_strips(mp, bias, masked):
        for r in range(blk // strip):
            rows = slice(r * strip, (r + 1) * strip)
            s = s_ref[mp, rows, :] + bias
            if masked:
                row = r * strip + lax.broadcasted_iota(jnp.int32, (strip, blk), 0)
                col = lax.broadcasted_iota(jnp.int32, (strip, blk), 1)
                s = jnp.where(col <= row, s, MASK_VALUE)
            m_old = m_ref[mp, rows, :]
            m_new = jnp.maximum(m_old, jnp.max(s, axis=-1, keepdims=True))
            alpha = jnp.exp2(m_old - m_new)
            p = jnp.exp2(s - _lanes(m_new, blk))
            p_sum = p[:, :LANES]
            for c in range(1, blk // LANES):
                p_sum = p_sum + p[:, c * LANES:(c + 1) * LANES]
            l_ref[mp, rows, :] = alpha * l_ref[mp, rows, :] + p_sum
            m_ref[mp, rows, :] = m_new
            a_ref[mp, rows, :] = alpha
            p_ref[mp, rows, :] = p.astype(BF16)

    def step(j, masked):
        bias = slope * (k_idx + jnp.asarray((j - i) * blk, F32))
        pv(1, jnp.maximum(j - 1, 0))
        qk(1, j)
        softmax_strips(0, bias, masked)
        pv(0, j)
        if not masked:
            qk(0, j + 1)
        softmax_strips(1, bias, masked)

    qk(0, 0)
    run = n_q_blocks // 2
    while run >= TAIL_RUNS:
        @pl.when(lax.bitwise_and(i, run) != 0)
        def _(run=run):
            start = lax.bitwise_and(i, -2 * run)
            for u in range(run):
                step(start + u, masked=False)
        run //= 2

    def tail(n_rest):
        for u in range(n_rest):
            step(i - n_rest + u, masked=False)
        step(i, masked=True)
        pv(1, i)
        l1 = jnp.sum(l_ref[0], axis=-1, keepdims=True)
        l2 = jnp.sum(l_ref[1], axis=-1, keepdims=True)
        o = acc_ref[0] / l1 - lam * (acc_ref[1] / l2)
        ms = jnp.mean(o * o, axis=-1, keepdims=True)
        y = (o * lax.rsqrt(ms + LN_EPS) * sg_ref[...]) * (1.0 - lam_init)
        o_ref[q_rows, :] = y.astype(o_ref.dtype)

    for n_rest in range(TAIL_RUNS):
        @pl.when(lax.bitwise_and(i, TAIL_RUNS - 1) == n_rest)
        def _(n_rest=n_rest):
            tail(n_rest)


def _diff_attention(qkv, lq1, lk1, lq2, lk2, subln_g, *, batch, seq, n_heads, lam_init,
                    blk=512, strip=32):
    t = qkv.shape[0]
    nq = seq // blk
    assert nq & (nq - 1) == 0, "the key-block walk assumes a power-of-two block count"
    vec = lambda a: a.reshape(1, -1)
    vec_spec = lambda n: pl.BlockSpec((1, n), lambda b, h: (0, 0))
    return pl.pallas_call(
        functools.partial(_attn_kernel, blk=blk, strip=strip, n_q_blocks=nq, n_heads=n_heads,
                          lam_init=lam_init),
        out_shape=jax.ShapeDtypeStruct((t, n_heads * V_HEAD_DIM), BF16),
        grid=(batch, n_heads),
        in_specs=[pl.BlockSpec((seq, 2 * HEAD_DIM), lambda b, h: (b, h)),
                  pl.BlockSpec((seq, 2 * HEAD_DIM), lambda b, h: (b, n_heads + h)),
                  pl.BlockSpec((seq, V_HEAD_DIM), lambda b, h: (b, 2 * n_heads + h)),
                  vec_spec(HEAD_DIM), vec_spec(HEAD_DIM), vec_spec(HEAD_DIM), vec_spec(HEAD_DIM),
                  vec_spec(V_HEAD_DIM)],
        out_specs=pl.BlockSpec((seq, V_HEAD_DIM), lambda b, h: (b, h)),
        scratch_shapes=[pltpu.VMEM((2, blk, blk), F32),
                        pltpu.VMEM((2, blk, blk), BF16),
                        pltpu.VMEM((2, blk, LANES), F32),
                        pltpu.VMEM((2, blk, LANES), F32),
                        pltpu.VMEM((2, blk, LANES), F32),
                        pltpu.VMEM((2, blk, V_HEAD_DIM), F32)],
        compiler_params=_params("parallel", "parallel"),
        name="diff_attention",
    )(qkv, qkv, qkv, vec(lq1), vec(lk1), vec(lq2), vec(lk2), vec(subln_g))


def kernel(x, p, g_mix, w_in, b_gate, ln_v_g, ln_v_b, w_s, b_s, lambda_q1, lambda_k1, lambda_q2, lambda_k2, subln_g, w_br_a, w_br_b, w_o, g_ffn, w_gu, w_down, g_ple, w_ple_gate, w_ple_proj, g_final):
    batch, seq, d_model = x.shape
    depth = w_in.shape[0]
    t = batch * seq
    d_gmlp = ln_v_g.shape[1]
    d_attn = w_br_b.shape[1]
    n_heads = d_attn // V_HEAD_DIM
    d_qk = n_heads * 2 * HEAD_DIM
    d_ff = w_down.shape[1]
    ple_dim = p.shape[-1]

    xf = x.reshape(t, d_model)
    for layer in range(depth):
        lam_init = 0.8 - 0.6 * math.exp(-0.3 * layer)
        w_in_b = w_in[layer].astype(BF16)

        h = _rmsnorm(xf, g_mix[layer], BF16)
        hx = [(h, d_model, 0)]
        bn = 1024
        guv = _matmul("in_proj_gelu", hx, [(w_in_b, 0)], [], _ep_gelu, dot_pairs=((0, 0),),
                      n_out=2 * d_gmlp, out_dtype=BF16, bm=1024, bn=bn)
        qkv_scale = jnp.concatenate([jnp.full((d_qk,), HEAD_DIM ** -0.5 * LOG2_E, F32),
                                     jnp.ones((d_qk + d_attn,), F32)]).reshape(1, -1)
        qkv = _matmul("in_proj_qkv", hx, [(w_in_b, 2 * d_gmlp // bn)], [("row", qkv_scale, 0)], _ep_colscale,
                      dot_pairs=((0, 0),), n_out=2 * d_qk + d_attn, out_dtype=BF16, bm=1024, bn=bn)
        gates = _matmul("in_proj_gates", hx, [(w_in_b, (2 * d_gmlp + 2 * d_qk + d_attn) // bn)],
                        [("row", b_gate[layer].reshape(1, -1), 0)], _ep_sigmoid_bias,
                        dot_pairs=((0, 0),), n_out=2 * d_model, out_dtype=BF16, bm=1024, bn=bn)

        y_a = _gmlp(guv, ln_v_g[layer], ln_v_b[layer], w_s[layer], b_s[layer])
        y_b = _diff_attention(qkv, lambda_q1[layer], lambda_k1[layer], lambda_q2[layer],
                              lambda_k2[layer], subln_g[layer], batch=batch, seq=seq,
                              n_heads=n_heads, lam_init=lam_init)

        bn = 1024
        merged_a = _matmul("merge_a", [(y_a, d_gmlp, 0)], [(w_br_a[layer].astype(BF16), 0)],
                           [("tile", gates, 0)], _ep_gated, dot_pairs=((0, 0),),
                           n_out=d_model, out_dtype=F32, bm=1024, bn=bn)
        merged = _matmul("merge_b", [(y_b, d_attn, 0)], [(w_br_b[layer].astype(BF16), 0)],
                         [("tile", gates, d_model // bn), ("tile", merged_a, 0)], _ep_gated_add,
                         dot_pairs=((0, 0),), n_out=d_model, out_dtype=BF16, bm=1024, bn=bn)
        xf, xg, ssq = _matmul("out_proj", [(merged, d_model, 0)], [(w_o[layer].astype(BF16), 0)],
                              [("tile", xf, 0)], _ep_residual, dot_pairs=((0, 0),),
                              n_out=d_model, out_dtype=F32, bm=1024, bn=512,
                              norm_gain=g_ffn[layer].reshape(1, -1))

        act = _swiglu_up(xg, ssq, w_gu[layer], bm=1024, bn=256)
        xf, xg, ssq = _matmul("ffn_down", [(act, d_ff, 0)], [(w_down[layer].astype(BF16), 0)],
                              [("tile", xf, 0)], _ep_residual, dot_pairs=((0, 0),),
                              n_out=d_model, out_dtype=F32, bm=512, bn=512,
                              norm_gain=g_ple[layer].reshape(1, -1))

        ple_bm, ple_bn = (512, 1024) if layer == depth - 1 else (1024, 512)
        xf = _matmul("ple", [(xg, d_model, 0), (p[layer].reshape(t, ple_dim), ple_dim, 0)],
                     [(w_ple_gate[layer].astype(BF16), 0), (w_ple_proj[layer].astype(BF16), 0)],
                     [("tile", xf, 0), ("rowstat", ssq, 0)],
                     functools.partial(_ep_ple_normed, n=d_model), dot_pairs=((0, 0), (1, 1)),
                     n_out=d_model, out_dtype=F32, bm=ple_bm, bn=ple_bn,
                     row_norm_gain=g_final.reshape(1, -1) if layer == depth - 1 else None)

    return xf.reshape(batch, seq, d_model)
```

```python
import functools
import math

import jax
import jax.numpy as jnp
from jax import lax
from jax.experimental import pallas as pl
from jax.experimental.pallas import tpu as pltpu

F32 = jnp.float32
BF16 = jnp.bfloat16

V7X_VMEM_LIMIT_BYTES = 56 * 1024 * 1024

LANES = 128
LOG2_E = math.log2(math.e)

HEAD_DIM = 128
V_HEAD_DIM = 2 * HEAD_DIM
CHUNK = 128
RMS_EPS = 1e-6
LN_EPS = 1e-5
MASK_VALUE = -1e30
TAIL_RUNS = 4


def _params(*semantics):
    return pltpu.CompilerParams(dimension_semantics=semantics,
                                vmem_limit_bytes=V7X_VMEM_LIMIT_BYTES)


def _rmsnorm_kernel(x_ref, g_ref, o_ref, *, eps):
    x = x_ref[...]
    ms = jnp.mean(x * x, axis=-1, keepdims=True)
    o_ref[...] = (x * lax.rsqrt(ms + eps) * g_ref[...]).astype(o_ref.dtype)


def _rmsnorm(x, g, out_dtype, *, rows=512):
    t, d = x.shape
    return pl.pallas_call(
        functools.partial(_rmsnorm_kernel, eps=RMS_EPS),
        out_shape=jax.ShapeDtypeStruct((t, d), out_dtype),
        grid=(t // rows,),
        in_specs=[pl.BlockSpec((rows, d), lambda i: (i, 0)),
                  pl.BlockSpec((1, d), lambda i: (0, 0))],
        out_specs=pl.BlockSpec((rows, d), lambda i: (i, 0)),
        compiler_params=_params("parallel"),
        name="rmsnorm",
    )(x, g.reshape(1, d))


def _mm_kernel(*refs, n_x, n_w, n_extra, dot_pairs, epilogue, emit_norm, row_norm, bn):
    x_refs = refs[:n_x]
    w_refs = refs[n_x:n_x + n_w]
    extra_refs = refs[n_x + n_w:n_x + n_w + n_extra]
    out_refs = refs[n_x + n_w + n_extra:]
    accs = [jnp.dot(x_refs[a][...].astype(BF16), w_refs[b][...],
                    preferred_element_type=F32) for a, b in dot_pairs]
    y = epilogue(accs, extra_refs)
    if row_norm:
        o_ref, ssq_ref = out_refs
        gain_ref = extra_refs[-1]
        j = pl.program_id(1)
        o_ref[:, pl.ds(pl.multiple_of(j * bn, bn), bn)] = y
        part = jnp.broadcast_to(jnp.sum(y * y, axis=-1, keepdims=True), ssq_ref.shape)

        @pl.when(j == 0)
        def _():
            ssq_ref[...] = part

        @pl.when(j != 0)
        def _():
            ssq_ref[...] += part

        @pl.when(j == pl.num_programs(1) - 1)
        def _():
            r = _rms_scale(ssq_ref, o_ref.shape[1])
            for c in range(o_ref.shape[1] // LANES):
                cs = slice(c * LANES, (c + 1) * LANES)
                o_ref[:, cs] = o_ref[:, cs] * r * gain_ref[:, cs]
        return
    out_refs[0][...] = y.astype(out_refs[0].dtype)
    if emit_norm:
        gain_ref = extra_refs[-1]
        xg_ref, ssq_ref = out_refs[1:]
        xg_ref[...] = (y * gain_ref[...]).astype(xg_ref.dtype)
        part = jnp.broadcast_to(jnp.sum(y * y, axis=-1, keepdims=True), ssq_ref.shape)
        j = pl.program_id(1)

        @pl.when(j == 0)
        def _():
            ssq_ref[...] = part

        @pl.when(j != 0)
        def _():
            ssq_ref[...] += part


def _matmul(name, xs, ws, extras, epilogue, *, dot_pairs, n_out, out_dtype, bm, bn,
            norm_gain=None, row_norm_gain=None):
    t = xs[0][0].shape[0]
    emit_norm = norm_gain is not None
    row_norm = row_norm_gain is not None
    if emit_norm:
        extras = list(extras) + [("row", norm_gain, 0)]
    if row_norm:
        extras = list(extras) + [("fullrow", row_norm_gain, 0)]
    in_specs, operands = [], []
    for arr, k, cb in xs:
        in_specs.append(pl.BlockSpec((bm, k), lambda i, j, cb=cb: (i, cb)))
        operands.append(arr)
    for arr, cb0 in ws:
        k = arr.shape[0]
        in_specs.append(pl.BlockSpec((k, bn), lambda i, j, cb0=cb0: (0, cb0 + j)))
        operands.append(arr)
    for kind, arr, cb0 in extras:
        if kind == "tile":
            in_specs.append(pl.BlockSpec((bm, bn), lambda i, j, cb0=cb0: (i, cb0 + j)))
        elif kind == "row":
            in_specs.append(pl.BlockSpec((1, bn), lambda i, j, cb0=cb0: (0, cb0 + j)))
        elif kind == "rowstat":
            in_specs.append(pl.BlockSpec((bm, LANES), lambda i, j: (i, 0)))
        else:
            in_specs.append(pl.BlockSpec((1, n_out), lambda i, j: (0, 0)))
        operands.append(arr)
    tile_spec = pl.BlockSpec((bm, bn), lambda i, j: (i, j))
    stat_spec = pl.BlockSpec((bm, LANES), lambda i, j: (i, 0))
    out_shape = jax.ShapeDtypeStruct((t, n_out), out_dtype)
    out_specs = tile_spec
    scratch_shapes = []
    if emit_norm:
        out_shape = (out_shape, jax.ShapeDtypeStruct((t, n_out), BF16),
                     jax.ShapeDtypeStruct((t, LANES), F32))
        out_specs = (tile_spec, tile_spec, stat_spec)
    if row_norm:
        assert out_dtype == F32 and not emit_norm
        out_specs = pl.BlockSpec((bm, n_out), lambda i, j: (i, 0))
        scratch_shapes = [pltpu.VMEM((bm, LANES), F32)]
    sequential_cols = emit_norm or row_norm
    return pl.pallas_call(
        functools.partial(_mm_kernel, n_x=len(xs), n_w=len(ws), n_extra=len(extras),
                          dot_pairs=dot_pairs, epilogue=epilogue, emit_norm=emit_norm,
                          row_norm=row_norm, bn=bn),
        out_shape=out_shape,
        grid=(t // bm, n_out // bn),
        in_specs=in_specs,
        out_specs=out_specs,
        scratch_shapes=scratch_shapes,
        compiler_params=_params("parallel", "arbitrary" if sequential_cols else "parallel"),
        name=name,
    )(*operands)


def _rms_scale(ssq_ref, n):
    return lax.rsqrt(ssq_ref[:, :1] / n + RMS_EPS)


def _gelu_exact(x):
    return 0.5 * x * (1.0 + lax.erf(x * math.sqrt(0.5)))


def _ep_gelu(accs, extras):
    return _gelu_exact(accs[0])


def _ep_colscale(accs, extras):
    return accs[0] * extras[0][...]


def _ep_sigmoid_bias(accs, extras):
    return jax.nn.sigmoid(accs[0] + extras[0][...])


def _ep_gated(accs, extras):
    return extras[0][...].astype(F32) * accs[0]


def _ep_gated_add(accs, extras):
    return extras[1][...] + extras[0][...].astype(F32) * accs[0]


def _ep_residual(accs, extras):
    return extras[0][...] + accs[0]


def _ep_ple_normed(accs, extras, *, n):
    r = _rms_scale(extras[1], n)
    return extras[0][...] + jax.nn.sigmoid(r * accs[0]) * accs[1]


def _swiglu_up_kernel(x_ref, wg_ref, wu_ref, ssq_ref, o_ref, w_ref, *, bn, n):
    @pl.when(pl.program_id(1) == 0)
    def _():
        w_ref[:, :bn] = wg_ref[...].astype(BF16)
        w_ref[:, bn:] = wu_ref[...].astype(BF16)

    acc = jnp.dot(x_ref[...], w_ref[...], preferred_element_type=F32)
    r = _rms_scale(ssq_ref, n)
    o_ref[...] = (jax.nn.silu(r * acc[:, :bn]) * (r * acc[:, bn:])).astype(o_ref.dtype)


def _swiglu_up(xg, ssq, w_gu, *, bm, bn):
    t, d = xg.shape
    d_ff = w_gu.shape[1] // 2
    return pl.pallas_call(
        functools.partial(_swiglu_up_kernel, bn=bn, n=d),
        out_shape=jax.ShapeDtypeStruct((t, d_ff), BF16),
        grid=(d_ff // bn, t // bm),
        in_specs=[pl.BlockSpec((bm, d), lambda j, i: (i, 0)),
                  pl.BlockSpec((d, bn), lambda j, i: (0, j)),
                  pl.BlockSpec((d, bn), lambda j, i: (0, d_ff // bn + j)),
                  pl.BlockSpec((bm, LANES), lambda j, i: (i, 0))],
        out_specs=pl.BlockSpec((bm, bn), lambda j, i: (i, j)),
        scratch_shapes=[pltpu.VMEM((d, 2 * bn), BF16)],
        compiler_params=_params("arbitrary", "arbitrary"),
        name="ffn_up",
    )(xg, w_gu, w_gu, ssq)


def _gmlp_kernel(gu_ref, gv_ref, lng_ref, lnb_ref, ws_ref, bs_ref, o_ref, wm_ref,
                 *, rows, n_groups, group_dim):
    @pl.when(pl.program_id(0) == 0)
    def _():
        r = lax.broadcasted_iota(jnp.int32, (CHUNK, CHUNK), 0)
        c = lax.broadcasted_iota(jnp.int32, (CHUNK, CHUNK), 1)
        for g in range(n_groups):
            wm_ref[g] = jnp.where(c <= r, ws_ref[g], 0.0).astype(BF16)

    for ch in range(rows // CHUNK):
        rs = slice(ch * CHUNK, (ch + 1) * CHUNK)
        v = gv_ref[rs, :].astype(F32)
        mu = jnp.mean(v, axis=-1, keepdims=True)
        vc = v - mu
        var = jnp.mean(vc * vc, axis=-1, keepdims=True)
        vn = (vc * lax.rsqrt(var + LN_EPS) * lng_ref[...] + lnb_ref[...]).astype(BF16)
        for g in range(n_groups):
            cs = slice(g * group_dim, (g + 1) * group_dim)
            mixed = jnp.dot(wm_ref[g], vn[:, cs], preferred_element_type=F32) + bs_ref[g]
            o_ref[rs, cs] = (gu_ref[rs, cs].astype(F32) * mixed).astype(o_ref.dtype)


def _gmlp(guv, ln_g, ln_b, w_s, b_s, *, rows=512):
    t = guv.shape[0]
    d = guv.shape[1] // 2
    n_groups = w_s.shape[0]
    return pl.pallas_call(
        functools.partial(_gmlp_kernel, rows=rows, n_groups=n_groups, group_dim=d // n_groups),
        out_shape=jax.ShapeDtypeStruct((t, d), BF16),
        grid=(t // rows,),
        in_specs=[pl.BlockSpec((rows, d), lambda i: (i, 0)),
                  pl.BlockSpec((rows, d), lambda i: (i, 1)),
                  pl.BlockSpec((1, d), lambda i: (0, 0)),
                  pl.BlockSpec((1, d), lambda i: (0, 0)),
                  pl.BlockSpec((n_groups, CHUNK, CHUNK), lambda i: (0, 0, 0)),
                  pl.BlockSpec((n_groups, CHUNK, 1), lambda i: (0, 0, 0))],
        out_specs=pl.BlockSpec((rows, d), lambda i: (i, 0)),
        scratch_shapes=[pltpu.VMEM((n_groups, CHUNK, CHUNK), BF16)],
        compiler_params=_params("arbitrary"),
        name="gmlp_gating",
    )(guv, guv, ln_g.reshape(1, d), ln_b.reshape(1, d), w_s, b_s.reshape(n_groups, CHUNK, 1))


def _lanes(x, n):
    return jnp.concatenate([x] * (n // LANES), axis=1)


def _attn_kernel(q_ref, k_ref, v_ref, lq1_ref, lk1_ref, lq2_ref, lk2_ref, sg_ref, o_ref,
                 s_ref, p_ref, m_ref, l_ref, a_ref, acc_ref,
                 *, blk, strip, n_q_blocks, n_heads, lam_init):
    h = pl.program_id(1)
    slope = jnp.exp2(-8.0 * jnp.full((1, blk), h + 1, F32) / n_heads) * LOG2_E
    k_idx = lax.broadcasted_iota(jnp.int32, (1, blk), 1).astype(F32)
    lam = (jnp.exp(jnp.sum(lq1_ref[...] * lk1_ref[...], axis=-1, keepdims=True))
           - jnp.exp(jnp.sum(lq2_ref[...] * lk2_ref[...], axis=-1, keepdims=True))
           + lam_init)

    def query_block(i, carry):
        _attn_query_block(i, slope, k_idx, lam, q_ref, k_ref, v_ref, sg_ref, o_ref,
                          s_ref, p_ref, m_ref, l_ref, a_ref, acc_ref,
                          blk=blk, strip=strip, n_q_blocks=n_q_blocks, lam_init=lam_init)
        return carry

    lax.fori_loop(0, n_q_blocks, query_block, 0)


def _attn_query_block(i, slope, k_idx, lam, q_ref, k_ref, v_ref, sg_ref, o_ref,
                      s_ref, p_ref, m_ref, l_ref, a_ref, acc_ref,
                      *, blk, strip, n_q_blocks, lam_init):
    q_rows = pl.ds(pl.multiple_of(i * blk, blk), blk)

    def begin():
        m_ref[...] = jnp.full(m_ref.shape, MASK_VALUE, F32)
        l_ref[...] = jnp.zeros(l_ref.shape, F32)
        acc_ref[...] = jnp.zeros(acc_ref.shape, F32)
        p_ref[1] = jnp.zeros(p_ref.shape[1:], BF16)
        a_ref[1] = jnp.ones(a_ref.shape[1:], F32)
        qk(0, 0)

    def qk(mp, j):
        off = pl.multiple_of(j * blk, blk)
        cs = slice(mp * HEAD_DIM, (mp + 1) * HEAD_DIM)
        s_ref[mp] = lax.dot_general(q_ref[q_rows, cs], k_ref[pl.ds(off, blk), cs],
                                    (((1,), (1,)), ((), ())),
                                    preferred_element_type=F32)

    def pv(mp, j):
        off = pl.multiple_of(j * blk, blk)
        acc_ref[mp] = (acc_ref[mp] * _lanes(a_ref[mp], V_HEAD_DIM)
                       + jnp.dot(p_ref[mp], v_ref[pl.ds(off, blk), :],
                                 preferred_element_type=F32))

    def softmax_strips(mp, bias, masked):
        for r in range(blk // strip):
            rows = slice(r * strip, (r + 1) * strip)
            s = s_ref[mp, rows, :] + bias
            if masked:
                row = r * strip + lax.broadcasted_iota(jnp.int32, (strip, blk), 0)
                col = lax.broadcasted_iota(jnp.int32, (strip, blk), 1)
                s = jnp.where(col <= row, s, MASK_VALUE)
            m_old = m_ref[mp, rows, :]
            m_new = jnp.maximum(m_old, jnp.max(s, axis=-1, keepdims=True))
            alpha = jnp.exp2(m_old - m_new)
            p = jnp.exp2(s - _lanes(m_new, blk))
            p_sum = p[:, :LANES]
            for c in range(1, blk // LANES):
                p_sum = p_sum + p[:, c * LANES:(c + 1) * LANES]
            l_ref[mp, rows, :] = alpha * l_ref[mp, rows, :] + p_sum
            m_ref[mp, rows, :] = m_new
            a_ref[mp, rows, :] = alpha
            p_ref[mp, rows, :] = p.astype(BF16)

    def step(j, masked):
        bias = slope * (k_idx + jnp.asarray((j - i) * blk, F32))
        pv(1, jnp.maximum(j - 1, 0))
        qk(1, j)
        softmax_strips(0, bias, masked)
        pv(0, j)
        if not masked:
            qk(0, j + 1)
        softmax_strips(1, bias, masked)

    def regions(selected, is_first, can_be_later, body):
        for first in (True, False) if can_be_later else (True,):
            @pl.when(jnp.logical_and(selected, is_first if first else jnp.logical_not(is_first)))
            def _(first=first):
                if first:
                    begin()
                body()

    run = n_q_blocks // 2
    while run >= TAIL_RUNS:
        def run_body(run=run):
            start = lax.bitwise_and(i, -2 * run)
            for u in range(run):
                step(start + u, masked=False)
        regions(lax.bitwise_and(i, run) != 0, i < 2 * run, 2 * run < n_q_blocks, run_body)
        run //= 2

    def tail(n_rest):
        for u in range(n_rest):
            step(i - n_rest + u, masked=False)
        step(i, masked=True)
        pv(1, i)
        l1 = jnp.sum(l_ref[0], axis=-1, keepdims=True)
        l2 = jnp.sum(l_ref[1], axis=-1, keepdims=True)
        o = acc_ref[0] / l1 - lam * (acc_ref[1] / l2)
        ms = jnp.mean(o * o, axis=-1, keepdims=True)
        y = (o * lax.rsqrt(ms + LN_EPS) * sg_ref[...]) * (1.0 - lam_init)
        o_ref[q_rows, :] = y.astype(o_ref.dtype)

    for n_rest in range(TAIL_RUNS):
        regions(lax.bitwise_and(i, TAIL_RUNS - 1) == n_rest, i < TAIL_RUNS,
                TAIL_RUNS < n_q_blocks, functools.partial(tail, n_rest))


def _diff_attention(qkv, lq1, lk1, lq2, lk2, subln_g, *, batch, seq, n_heads, lam_init,
                    blk=512, strip=32):
    t = qkv.shape[0]
    nq = seq // blk
    assert nq & (nq - 1) == 0, "the key-block walk assumes a power-of-two block count"
    vec = lambda a: a.reshape(1, -1)
    vec_spec = lambda n: pl.BlockSpec((1, n), lambda b, h: (0, 0))
    return pl.pallas_call(
        functools.partial(_attn_kernel, blk=blk, strip=strip, n_q_blocks=nq, n_heads=n_heads,
                          lam_init=lam_init),
        out_shape=jax.ShapeDtypeStruct((t, n_heads * V_HEAD_DIM), BF16),
        grid=(batch, n_heads),
        in_specs=[pl.BlockSpec((seq, 2 * HEAD_DIM), lambda b, h: (b, h)),
                  pl.BlockSpec((seq, 2 * HEAD_DIM), lambda b, h: (b, n_heads + h)),
                  pl.BlockSpec((seq, V_HEAD_DIM), lambda b, h: (b, 2 * n_heads + h)),
                  vec_spec(HEAD_DIM), vec_spec(HEAD_DIM), vec_spec(HEAD_DIM), vec_spec(HEAD_DIM),
                  vec_spec(V_HEAD_DIM)],
        out_specs=pl.BlockSpec((seq, V_HEAD_DIM), lambda b, h: (b, h)),
        scratch_shapes=[pltpu.VMEM((2, blk, blk), F32),
                        pltpu.VMEM((2, blk, blk), BF16),
                        pltpu.VMEM((2, blk, LANES), F32),
                        pltpu.VMEM((2, blk, LANES), F32),
                        pltpu.VMEM((2, blk, LANES), F32),
                        pltpu.VMEM((2, blk, V_HEAD_DIM), F32)],
        compiler_params=_params("parallel", "parallel"),
        name="diff_attention",
    )(qkv, qkv, qkv, vec(lq1), vec(lk1), vec(lq2), vec(lk2), vec(subln_g))


def kernel(x, p, g_mix, w_in, b_gate, ln_v_g, ln_v_b, w_s, b_s, lambda_q1, lambda_k1, lambda_q2, lambda_k2, subln_g, w_br_a, w_br_b, w_o, g_ffn, w_gu, w_down, g_ple, w_ple_gate, w_ple_proj, g_final):
    batch, seq, d_model = x.shape
    depth = w_in.shape[0]
    t = batch * seq
    d_gmlp = ln_v_g.shape[1]
    d_attn = w_br_b.shape[1]
    n_heads = d_attn // V_HEAD_DIM
    d_qk = n_heads * 2 * HEAD_DIM
    d_ff = w_down.shape[1]
    ple_dim = p.shape[-1]

    xf = x.reshape(t, d_model)
    for layer in range(depth):
        lam_init = 0.8 - 0.6 * math.exp(-0.3 * layer)
        w_in_b = w_in[layer].astype(BF16)

        h = _rmsnorm(xf, g_mix[layer], BF16)
        hx = [(h, d_model, 0)]
        bn = 1024
        guv = _matmul("in_proj_gelu", hx, [(w_in_b, 0)], [], _ep_gelu, dot_pairs=((0, 0),),
                      n_out=2 * d_gmlp, out_dtype=BF16, bm=1024, bn=bn)
        qkv_scale = jnp.concatenate([jnp.full((d_qk,), HEAD_DIM ** -0.5 * LOG2_E, F32),
                                     jnp.ones((d_qk + d_attn,), F32)]).reshape(1, -1)
        qkv = _matmul("in_proj_qkv", hx, [(w_in_b, 2 * d_gmlp // bn)], [("row", qkv_scale, 0)], _ep_colscale,
                      dot_pairs=((0, 0),), n_out=2 * d_qk + d_attn, out_dtype=BF16, bm=1024, bn=bn)
        gates = _matmul("in_proj_gates", hx, [(w_in_b, (2 * d_gmlp + 2 * d_qk + d_attn) // bn)],
                        [("row", b_gate[layer].reshape(1, -1), 0)], _ep_sigmoid_bias,
                        dot_pairs=((0, 0),), n_out=2 * d_model, out_dtype=BF16, bm=1024, bn=bn)

        y_a = _gmlp(guv, ln_v_g[layer], ln_v_b[layer], w_s[layer], b_s[layer])
        y_b = _diff_attention(qkv, lambda_q1[layer], lambda_k1[layer], lambda_q2[layer],
                              lambda_k2[layer], subln_g[layer], batch=batch, seq=seq,
                              n_heads=n_heads, lam_init=lam_init)

        bn = 1024
        merged_a = _matmul("merge_a", [(y_a, d_gmlp, 0)], [(w_br_a[layer].astype(BF16), 0)],
                           [("tile", gates, 0)], _ep_gated, dot_pairs=((0, 0),),
                           n_out=d_model, out_dtype=F32, bm=1024, bn=bn)
        merged = _matmul("merge_b", [(y_b, d_attn, 0)], [(w_br_b[layer].astype(BF16), 0)],
                         [("tile", gates, d_model // bn), ("tile", merged_a, 0)], _ep_gated_add,
                         dot_pairs=((0, 0),), n_out=d_model, out_dtype=BF16, bm=1024, bn=bn)
        xf, xg, ssq = _matmul("out_proj", [(merged, d_model, 0)], [(w_o[layer].astype(BF16), 0)],
                              [("tile", xf, 0)], _ep_residual, dot_pairs=((0, 0),),
                              n_out=d_model, out_dtype=F32, bm=1024, bn=512,
                              norm_gain=g_ffn[layer].reshape(1, -1))

        act = _swiglu_up(xg, ssq, w_gu[layer], bm=1024, bn=256)
        xf, xg, ssq = _matmul("ffn_down", [(act, d_ff, 0)], [(w_down[layer].astype(BF16), 0)],
                              [("tile", xf, 0)], _ep_residual, dot_pairs=((0, 0),),
                              n_out=d_model, out_dtype=F32, bm=512, bn=512,
                              norm_gain=g_ple[layer].reshape(1, -1))

        ple_bm, ple_bn = (512, 1024) if layer == depth - 1 else (1024, 512)
        xf = _matmul("ple", [(xg, d_model, 0), (p[layer].reshape(t, ple_dim), ple_dim, 0)],
                     [(w_ple_gate[layer].astype(BF16), 0), (w_ple_proj[layer].astype(BF16), 0)],
                     [("tile", xf, 0), ("rowstat", ssq, 0)],
                     functools.partial(_ep_ple_normed, n=d_model), dot_pairs=((0, 0), (1, 1)),
                     n_out=d_model, out_dtype=F32, bm=ple_bm, bn=ple_bn,
                     row_norm_gain=g_final.reshape(1, -1) if layer == depth - 1 else None)

    return xf.reshape(batch, seq, d_model)
```

```python
import functools
import math

import jax
import jax.numpy as jnp
from jax import lax
from jax.experimental import pallas as pl
from jax.experimental.pallas import tpu as pltpu

F32 = jnp.float32
BF16 = jnp.bfloat16

V7X_VMEM_LIMIT_BYTES = 56 * 1024 * 1024

LANES = 128
LOG2_E = math.log2(math.e)

HEAD_DIM = 128
V_HEAD_DIM = 2 * HEAD_DIM
CHUNK = 128
RMS_EPS = 1e-6
LN_EPS = 1e-5
MASK_VALUE = -1e30
TAIL_RUNS = 4


def _params(*semantics):
    return pltpu.CompilerParams(dimension_semantics=semantics,
                                vmem_limit_bytes=V7X_VMEM_LIMIT_BYTES)


def _rmsnorm_kernel(x_ref, g_ref, o_ref, *, eps):
    x = x_ref[...]
    ms = jnp.mean(x * x, axis=-1, keepdims=True)
    o_ref[...] = (x * lax.rsqrt(ms + eps) * g_ref[...]).astype(o_ref.dtype)


def _rmsnorm(x, g, out_dtype, *, rows=512):
    t, d = x.shape
    return pl.pallas_call(
        functools.partial(_rmsnorm_kernel, eps=RMS_EPS),
        out_shape=jax.ShapeDtypeStruct((t, d), out_dtype),
        grid=(t // rows,),
        in_specs=[pl.BlockSpec((rows, d), lambda i: (i, 0)),
                  pl.BlockSpec((1, d), lambda i: (0, 0))],
        out_specs=pl.BlockSpec((rows, d), lambda i: (i, 0)),
        compiler_params=_params("parallel"),
        name="rmsnorm",
    )(x, g.reshape(1, d))


def _mm_kernel(*refs, n_x, n_w, n_extra, dot_pairs, epilogue, emit_norm, row_norm, bn):
    x_refs = refs[:n_x]
    w_refs = refs[n_x:n_x + n_w]
    extra_refs = refs[n_x + n_w:n_x + n_w + n_extra]
    out_refs = refs[n_x + n_w + n_extra:]
    accs = [jnp.dot(x_refs[a][...].astype(BF16), w_refs[b][...],
                    preferred_element_type=F32) for a, b in dot_pairs]
    y = epilogue(accs, extra_refs)
    if row_norm:
        o_ref, ssq_ref = out_refs
        gain_ref = extra_refs[-1]
        j = pl.program_id(1)
        o_ref[:, pl.ds(pl.multiple_of(j * bn, bn), bn)] = y
        part = jnp.broadcast_to(jnp.sum(y * y, axis=-1, keepdims=True), ssq_ref.shape)

        @pl.when(j == 0)
        def _():
            ssq_ref[...] = part

        @pl.when(j != 0)
        def _():
            ssq_ref[...] += part

        @pl.when(j == pl.num_programs(1) - 1)
        def _():
            r = _rms_scale(ssq_ref, o_ref.shape[1])
            for c in range(o_ref.shape[1] // LANES):
                cs = slice(c * LANES, (c + 1) * LANES)
                o_ref[:, cs] = o_ref[:, cs] * r * gain_ref[:, cs]
        return
    out_refs[0][...] = y.astype(out_refs[0].dtype)
    if emit_norm:
        gain_ref = extra_refs[-1]
        xg_ref, ssq_ref = out_refs[1:]
        xg_ref[...] = (y * gain_ref[...]).astype(xg_ref.dtype)
        part = jnp.broadcast_to(jnp.sum(y * y, axis=-1, keepdims=True), ssq_ref.shape)
        j = pl.program_id(1)

        @pl.when(j == 0)
        def _():
            ssq_ref[...] = part

        @pl.when(j != 0)
        def _():
            ssq_ref[...] += part


def _matmul(name, xs, ws, extras, epilogue, *, dot_pairs, n_out, out_dtype, bm, bn,
            norm_gain=None, row_norm_gain=None):
    t = xs[0][0].shape[0]
    emit_norm = norm_gain is not None
    row_norm = row_norm_gain is not None
    if emit_norm:
        extras = list(extras) + [("row", norm_gain, 0)]
    if row_norm:
        extras = list(extras) + [("fullrow", row_norm_gain, 0)]
    in_specs, operands = [], []
    for arr, k, cb in xs:
        in_specs.append(pl.BlockSpec((bm, k), lambda i, j, cb=cb: (i, cb)))
        operands.append(arr)
    for arr, cb0 in ws:
        k = arr.shape[0]
        in_specs.append(pl.BlockSpec((k, bn), lambda i, j, cb0=cb0: (0, cb0 + j)))
        operands.append(arr)
    for kind, arr, cb0 in extras:
        if kind == "tile":
            in_specs.append(pl.BlockSpec((bm, bn), lambda i, j, cb0=cb0: (i, cb0 + j)))
        elif kind == "row":
            in_specs.append(pl.BlockSpec((1, bn), lambda i, j, cb0=cb0: (0, cb0 + j)))
        elif kind == "rowstat":
            in_specs.append(pl.BlockSpec((bm, LANES), lambda i, j: (i, 0)))
        else:
            in_specs.append(pl.BlockSpec((1, n_out), lambda i, j: (0, 0)))
        operands.append(arr)
    tile_spec = pl.BlockSpec((bm, bn), lambda i, j: (i, j))
    stat_spec = pl.BlockSpec((bm, LANES), lambda i, j: (i, 0))
    out_shape = jax.ShapeDtypeStruct((t, n_out), out_dtype)
    out_specs = tile_spec
    scratch_shapes = []
    if emit_norm:
        out_shape = (out_shape, jax.ShapeDtypeStruct((t, n_out), BF16),
                     jax.ShapeDtypeStruct((t, LANES), F32))
        out_specs = (tile_spec, tile_spec, stat_spec)
    if row_norm:
        assert out_dtype == F32 and not emit_norm
        out_specs = pl.BlockSpec((bm, n_out), lambda i, j: (i, 0))
        scratch_shapes = [pltpu.VMEM((bm, LANES), F32)]
    sequential_cols = emit_norm or row_norm
    return pl.pallas_call(
        functools.partial(_mm_kernel, n_x=len(xs), n_w=len(ws), n_extra=len(extras),
                          dot_pairs=dot_pairs, epilogue=epilogue, emit_norm=emit_norm,
                          row_norm=row_norm, bn=bn),
        out_shape=out_shape,
        grid=(t // bm, n_out // bn),
        in_specs=in_specs,
        out_specs=out_specs,
        scratch_shapes=scratch_shapes,
        compiler_params=_params("parallel", "arbitrary" if sequential_cols else "parallel"),
        name=name,
    )(*operands)


def _rms_scale(ssq_ref, n):
    return lax.rsqrt(ssq_ref[:, :1] / n + RMS_EPS)


def _gelu_exact(x):
    return 0.5 * x * (1.0 + lax.erf(x * math.sqrt(0.5)))


def _ep_gelu(accs, extras):
    return _gelu_exact(accs[0])


def _ep_colscale(accs, extras):
    return accs[0] * extras[0][...]


def _ep_sigmoid_bias(accs, extras):
    return jax.nn.sigmoid(accs[0] + extras[0][...])


def _ep_gated(accs, extras):
    return extras[0][...].astype(F32) * accs[0]


def _ep_gated_add(accs, extras):
    return extras[1][...] + extras[0][...].astype(F32) * accs[0]


def _ep_residual(accs, extras):
    return extras[0][...] + accs[0]


def _ep_ple_normed(accs, extras, *, n):
    r = _rms_scale(extras[1], n)
    return extras[0][...] + jax.nn.sigmoid(r * accs[0]) * accs[1]


def _swiglu_up_kernel(x_ref, wg_ref, wu_ref, ssq_ref, o_ref, w_ref, *, bn, n):
    @pl.when(pl.program_id(1) == 0)
    def _():
        w_ref[:, :bn] = wg_ref[...].astype(BF16)
        w_ref[:, bn:] = wu_ref[...].astype(BF16)

    acc = jnp.dot(x_ref[...], w_ref[...], preferred_element_type=F32)
    r = _rms_scale(ssq_ref, n)
    o_ref[...] = (jax.nn.silu(r * acc[:, :bn]) * (r * acc[:, bn:])).astype(o_ref.dtype)


def _swiglu_up(xg, ssq, w_gu, *, bm, bn):
    t, d = xg.shape
    d_ff = w_gu.shape[1] // 2
    return pl.pallas_call(
        functools.partial(_swiglu_up_kernel, bn=bn, n=d),
        out_shape=jax.ShapeDtypeStruct((t, d_ff), BF16),
        grid=(d_ff // bn, t // bm),
        in_specs=[pl.BlockSpec((bm, d), lambda j, i: (i, 0)),
                  pl.BlockSpec((d, bn), lambda j, i: (0, j)),
                  pl.BlockSpec((d, bn), lambda j, i: (0, d_ff // bn + j)),
                  pl.BlockSpec((bm, LANES), lambda j, i: (i, 0))],
        out_specs=pl.BlockSpec((bm, bn), lambda j, i: (i, j)),
        scratch_shapes=[pltpu.VMEM((d, 2 * bn), BF16)],
        compiler_params=_params("arbitrary", "arbitrary"),
        name="ffn_up",
    )(xg, w_gu, w_gu, ssq)


def _gmlp_kernel(gu_ref, gv_ref, lng_ref, lnb_ref, ws_ref, bs_ref, o_ref, wm_ref,
                 *, rows, n_groups, group_dim):
    @pl.when(pl.program_id(0) == 0)
    def _():
        r = lax.broadcasted_iota(jnp.int32, (CHUNK, CHUNK), 0)
        c = lax.broadcasted_iota(jnp.int32, (CHUNK, CHUNK), 1)
        for g in range(n_groups):
            wm_ref[g] = jnp.where(c <= r, ws_ref[g], 0.0).astype(BF16)

    for ch in range(rows // CHUNK):
        rs = slice(ch * CHUNK, (ch + 1) * CHUNK)
        v = gv_ref[rs, :].astype(F32)
        mu = jnp.mean(v, axis=-1, keepdims=True)
        vc = v - mu
        var = jnp.mean(vc * vc, axis=-1, keepdims=True)
        vn = (vc * lax.rsqrt(var + LN_EPS) * lng_ref[...] + lnb_ref[...]).astype(BF16)
        for g in range(n_groups):
            cs = slice(g * group_dim, (g + 1) * group_dim)
            mixed = jnp.dot(wm_ref[g], vn[:, cs], preferred_element_type=F32) + bs_ref[g]
            o_ref[rs, cs] = (gu_ref[rs, cs].astype(F32) * mixed).astype(o_ref.dtype)


def _gmlp(guv, ln_g, ln_b, w_s, b_s, *, rows=512):
    t = guv.shape[0]
    d = guv.shape[1] // 2
    n_groups = w_s.shape[0]
    return pl.pallas_call(
        functools.partial(_gmlp_kernel, rows=rows, n_groups=n_groups, group_dim=d // n_groups),
        out_shape=jax.ShapeDtypeStruct((t, d), BF16),
        grid=(t // rows,),
        in_specs=[pl.BlockSpec((rows, d), lambda i: (i, 0)),
                  pl.BlockSpec((rows, d), lambda i: (i, 1)),
                  pl.BlockSpec((1, d), lambda i: (0, 0)),
                  pl.BlockSpec((1, d), lambda i: (0, 0)),
                  pl.BlockSpec((n_groups, CHUNK, CHUNK), lambda i: (0, 0, 0)),
                  pl.BlockSpec((n_groups, CHUNK, 1), lambda i: (0, 0, 0))],
        out_specs=pl.BlockSpec((rows, d), lambda i: (i, 0)),
        scratch_shapes=[pltpu.VMEM((n_groups, CHUNK, CHUNK), BF16)],
        compiler_params=_params("arbitrary"),
        name="gmlp_gating",
    )(guv, guv, ln_g.reshape(1, d), ln_b.reshape(1, d), w_s, b_s.reshape(n_groups, CHUNK, 1))


def _lanes(x, n):
    return jnp.concatenate([x] * (n // LANES), axis=1)


def _attn_kernel(q_ref, k_ref, v_ref, lq1_ref, lk1_ref, lq2_ref, lk2_ref, sg_ref, o_ref,
                 s_ref, p_ref, m_ref, l_ref, a_ref, acc_ref,
                 *, blk, strip, n_q_blocks, n_heads, lam_init):
    h = pl.program_id(1)
    slope = jnp.exp2(-8.0 * jnp.full((1, blk), h + 1, F32) / n_heads) * LOG2_E
    k_idx = lax.broadcasted_iota(jnp.int32, (1, blk), 1).astype(F32)
    lam = (jnp.exp(jnp.sum(lq1_ref[...] * lk1_ref[...], axis=-1, keepdims=True))
           - jnp.exp(jnp.sum(lq2_ref[...] * lk2_ref[...], axis=-1, keepdims=True))
           + lam_init)

    def query_block(i, carry):
        _attn_query_block(i, slope, k_idx, lam, q_ref, k_ref, v_ref, sg_ref, o_ref,
                          s_ref, p_ref, m_ref, l_ref, a_ref, acc_ref,
                          blk=blk, strip=strip, n_q_blocks=n_q_blocks, lam_init=lam_init)
        return carry

    lax.fori_loop(0, n_q_blocks, query_block, 0)


def _attn_query_block(i, slope, k_idx, lam, q_ref, k_ref, v_ref, sg_ref, o_ref,
                      s_ref, p_ref, m_ref, l_ref, a_ref, acc_ref,
                      *, blk, strip, n_q_blocks, lam_init):
    q_rows = pl.ds(pl.multiple_of(i * blk, blk), blk)

    def begin():
        m_ref[...] = jnp.full(m_ref.shape, MASK_VALUE, F32)
        l_ref[...] = jnp.zeros(l_ref.shape, F32)
        acc_ref[...] = jnp.zeros(acc_ref.shape, F32)
        qk(0, 0)

    def qk(mp, j):
        off = pl.multiple_of(j * blk, blk)
        cs = slice(mp * HEAD_DIM, (mp + 1) * HEAD_DIM)
        s_ref[mp] = lax.dot_general(q_ref[q_rows, cs], k_ref[pl.ds(off, blk), cs],
                                    (((1,), (1,)), ((), ())),
                                    preferred_element_type=F32)

    def pv(mp, j):
        off = pl.multiple_of(j * blk, blk)
        acc_ref[mp] = (acc_ref[mp] * _lanes(a_ref[mp], V_HEAD_DIM)
                       + jnp.dot(p_ref[mp], v_ref[pl.ds(off, blk), :],
                                 preferred_element_type=F32))

    def softmax_strips(mp, bias, masked):
        for r in range(blk // strip):
            rows = slice(r * strip, (r + 1) * strip)
            s = s_ref[mp, rows, :] + bias
            if masked:
                row = r * strip + lax.broadcasted_iota(jnp.int32, (strip, blk), 0)
                col = lax.broadcasted_iota(jnp.int32, (strip, blk), 1)
                s = jnp.where(col <= row, s, MASK_VALUE)
            m_old = m_ref[mp, rows, :]
            m_new = jnp.maximum(m_old, jnp.max(s, axis=-1, keepdims=True))
            alpha = jnp.exp2(m_old - m_new)
            p = jnp.exp2(s - _lanes(m_new, blk))
            p_sum = p[:, :LANES]
            for c in range(1, blk // LANES):
                p_sum = p_sum + p[:, c * LANES:(c + 1) * LANES]
            l_ref[mp, rows, :] = alpha * l_ref[mp, rows, :] + p_sum
            m_ref[mp, rows, :] = m_new
            a_ref[mp, rows, :] = alpha
            p_ref[mp, rows, :] = p.astype(BF16)

    def step(j, masked, first=False):
        bias = slope * (k_idx + jnp.asarray((j - i) * blk, F32))
        if not first:
            pv(1, j - 1)
        qk(1, j)
        softmax_strips(0, bias, masked)
        pv(0, j)
        if not masked:
            qk(0, j + 1)
        softmax_strips(1, bias, masked)

    def regions(selected, is_first, can_be_later, body):
        for first in (True, False) if can_be_later else (True,):
            @pl.when(jnp.logical_and(selected, is_first if first else jnp.logical_not(is_first)))
            def _(first=first):
                if first:
                    begin()
                body(first)

    run = n_q_blocks // 2
    while run >= TAIL_RUNS:
        def run_body(first, run=run):
            start = lax.bitwise_and(i, -2 * run)
            for u in range(run):
                step(start + u, masked=False, first=first and u == 0)
        regions(lax.bitwise_and(i, run) != 0, i < 2 * run, 2 * run < n_q_blocks, run_body)
        run //= 2

    def tail(n_rest, first):
        for u in range(n_rest):
            step(i - n_rest + u, masked=False, first=first and u == 0)
        step(i, masked=True, first=first and n_rest == 0)
        pv(1, i)
        l1 = jnp.sum(l_ref[0], axis=-1, keepdims=True)
        l2 = jnp.sum(l_ref[1], axis=-1, keepdims=True)
        o = acc_ref[0] / l1 - lam * (acc_ref[1] / l2)
        ms = jnp.mean(o * o, axis=-1, keepdims=True)
        y = (o * lax.rsqrt(ms + LN_EPS) * sg_ref[...]) * (1.0 - lam_init)
        o_ref[q_rows, :] = y.astype(o_ref.dtype)

    for n_rest in range(TAIL_RUNS):
        regions(lax.bitwise_and(i, TAIL_RUNS - 1) == n_rest, i < TAIL_RUNS,
                TAIL_RUNS < n_q_blocks, functools.partial(tail, n_rest))


def _diff_attention(qkv, lq1, lk1, lq2, lk2, subln_g, *, batch, seq, n_heads, lam_init,
                    blk=512, strip=32):
    t = qkv.shape[0]
    nq = seq // blk
    assert nq & (nq - 1) == 0, "the key-block walk assumes a power-of-two block count"
    vec = lambda a: a.reshape(1, -1)
    vec_spec = lambda n: pl.BlockSpec((1, n), lambda b, h: (0, 0))
    return pl.pallas_call(
        functools.partial(_attn_kernel, blk=blk, strip=strip, n_q_blocks=nq, n_heads=n_heads,
                          lam_init=lam_init),
        out_shape=jax.ShapeDtypeStruct((t, n_heads * V_HEAD_DIM), BF16),
        grid=(batch, n_heads),
        in_specs=[pl.BlockSpec((seq, 2 * HEAD_DIM), lambda b, h: (b, h)),
                  pl.BlockSpec((seq, 2 * HEAD_DIM), lambda b, h: (b, n_heads + h)),
                  pl.BlockSpec((seq, V_HEAD_DIM), lambda b, h: (b, 2 * n_heads + h)),
                  vec_spec(HEAD_DIM), vec_spec(HEAD_DIM), vec_spec(HEAD_DIM), vec_spec(HEAD_DIM),
                  vec_spec(V_HEAD_DIM)],
        out_specs=pl.BlockSpec((seq, V_HEAD_DIM), lambda b, h: (b, h)),
        scratch_shapes=[pltpu.VMEM((2, blk, blk), F32),
                        pltpu.VMEM((2, blk, blk), BF16),
                        pltpu.VMEM((2, blk, LANES), F32),
                        pltpu.VMEM((2, blk, LANES), F32),
                        pltpu.VMEM((2, blk, LANES), F32),
                        pltpu.VMEM((2, blk, V_HEAD_DIM), F32)],
        compiler_params=_params("parallel", "parallel"),
        name="diff_attention",
    )(qkv, qkv, qkv, vec(lq1), vec(lk1), vec(lq2), vec(lk2), vec(subln_g))


def kernel(x, p, g_mix, w_in, b_gate, ln_v_g, ln_v_b, w_s, b_s, lambda_q1, lambda_k1, lambda_q2, lambda_k2, subln_g, w_br_a, w_br_b, w_o, g_ffn, w_gu, w_down, g_ple, w_ple_gate, w_ple_proj, g_final):
    batch, seq, d_model = x.shape
    depth = w_in.shape[0]
    t = batch * seq
    d_gmlp = ln_v_g.shape[1]
    d_attn = w_br_b.shape[1]
    n_heads = d_attn // V_HEAD_DIM
    d_qk = n_heads * 2 * HEAD_DIM
    d_ff = w_down.shape[1]
    ple_dim = p.shape[-1]

    xf = x.reshape(t, d_model)
    for layer in range(depth):
        lam_init = 0.8 - 0.6 * math.exp(-0.3 * layer)
        w_in_b = w_in[layer].astype(BF16)

        h = _rmsnorm(xf, g_mix[layer], BF16)
        hx = [(h, d_model, 0)]
        bn = 1024
        guv = _matmul("in_proj_gelu", hx, [(w_in_b, 0)], [], _ep_gelu, dot_pairs=((0, 0),),
                      n_out=2 * d_gmlp, out_dtype=BF16, bm=1024, bn=bn)
        qkv_scale = jnp.concatenate([jnp.full((d_qk,), HEAD_DIM ** -0.5 * LOG2_E, F32),
                                     jnp.ones((d_qk + d_attn,), F32)]).reshape(1, -1)
        qkv = _matmul("in_proj_qkv", hx, [(w_in_b, 2 * d_gmlp // bn)], [("row", qkv_scale, 0)], _ep_colscale,
                      dot_pairs=((0, 0),), n_out=2 * d_qk + d_attn, out_dtype=BF16, bm=1024, bn=bn)
        gates = _matmul("in_proj_gates", hx, [(w_in_b, (2 * d_gmlp + 2 * d_qk + d_attn) // bn)],
                        [("row", b_gate[layer].reshape(1, -1), 0)], _ep_sigmoid_bias,
                        dot_pairs=((0, 0),), n_out=2 * d_model, out_dtype=BF16, bm=1024, bn=bn)

        y_a = _gmlp(guv, ln_v_g[layer], ln_v_b[layer], w_s[layer], b_s[layer])
        y_b = _diff_attention(qkv, lambda_q1[layer], lambda_k1[layer], lambda_q2[layer],
                              lambda_k2[layer], subln_g[layer], batch=batch, seq=seq,
                              n_heads=n_heads, lam_init=lam_init)

        bn = 1024
        merged_a = _matmul("merge_a", [(y_a, d_gmlp, 0)], [(w_br_a[layer].astype(BF16), 0)],
                           [("tile", gates, 0)], _ep_gated, dot_pairs=((0, 0),),
                           n_out=d_model, out_dtype=F32, bm=1024, bn=bn)
        merged = _matmul("merge_b", [(y_b, d_attn, 0)], [(w_br_b[layer].astype(BF16), 0)],
                         [("tile", gates, d_model // bn), ("tile", merged_a, 0)], _ep_gated_add,
                         dot_pairs=((0, 0),), n_out=d_model, out_dtype=BF16, bm=1024, bn=bn)
        xf, xg, ssq = _matmul("out_proj", [(merged, d_model, 0)], [(w_o[layer].astype(BF16), 0)],
                              [("tile", xf, 0)], _ep_residual, dot_pairs=((0, 0),),
                              n_out=d_model, out_dtype=F32, bm=1024, bn=512,
                              norm_gain=g_ffn[layer].reshape(1, -1))

        act = _swiglu_up(xg, ssq, w_gu[layer], bm=1024, bn=256)
        xf, xg, ssq = _matmul("ffn_down", [(act, d_ff, 0)], [(w_down[layer].astype(BF16), 0)],
                              [("tile", xf, 0)], _ep_residual, dot_pairs=((0, 0),),
                              n_out=d_model, out_dtype=F32, bm=512, bn=512,
                              norm_gain=g_ple[layer].reshape(1, -1))

        ple_bm, ple_bn = (512, 1024) if layer == depth - 1 else (1024, 512)
        xf = _matmul("ple", [(xg, d_model, 0), (p[layer].reshape(t, ple_dim), ple_dim, 0)],
                     [(w_ple_gate[layer].astype(BF16), 0), (w_ple_proj[layer].astype(BF16), 0)],
                     [("tile", xf, 0), ("rowstat", ssq, 0)],
                     functools.partial(_ep_ple_normed, n=d_model), dot_pairs=((0, 0), (1, 1)),
                     n_out=d_model, out_dtype=F32, bm=ple_bm, bn=ple_bn,
                     row_norm_gain=g_final.reshape(1, -1) if layer == depth - 1 else None)

    return xf.reshape(batch, seq, d_model)
```

```python
import functools
import math

import jax
import jax.numpy as jnp
from jax import lax
from jax.experimental import pallas as pl
from jax.experimental.pallas import tpu as pltpu

F32 = jnp.float32
BF16 = jnp.bfloat16

V7X_VMEM_LIMIT_BYTES = 56 * 1024 * 1024

LANES = 128
LOG2_E = math.log2(math.e)

HEAD_DIM = 128
V_HEAD_DIM = 2 * HEAD_DIM
CHUNK = 128
RMS_EPS = 1e-6
LN_EPS = 1e-5
MASK_VALUE = -1e30
TAIL_RUNS = 4


def _params(*semantics):
    return pltpu.CompilerParams(dimension_semantics=semantics,
                                vmem_limit_bytes=V7X_VMEM_LIMIT_BYTES)


def _rmsnorm_kernel(x_ref, g_ref, o_ref, *, eps):
    x = x_ref[...]
    ms = jnp.mean(x * x, axis=-1, keepdims=True)
    o_ref[...] = (x * lax.rsqrt(ms + eps) * g_ref[...]).astype(o_ref.dtype)


def _rmsnorm(x, g, out_dtype, *, rows=512):
    t, d = x.shape
    return pl.pallas_call(
        functools.partial(_rmsnorm_kernel, eps=RMS_EPS),
        out_shape=jax.ShapeDtypeStruct((t, d), out_dtype),
        grid=(t // rows,),
        in_specs=[pl.BlockSpec((rows, d), lambda i: (i, 0)),
                  pl.BlockSpec((1, d), lambda i: (0, 0))],
        out_specs=pl.BlockSpec((rows, d), lambda i: (i, 0)),
        compiler_params=_params("parallel"),
        name="rmsnorm",
    )(x, g.reshape(1, d))


def _mm_kernel(*refs, n_x, n_w, n_extra, dot_pairs, epilogue, emit_norm, row_norm, bn):
    x_refs = refs[:n_x]
    w_refs = refs[n_x:n_x + n_w]
    extra_refs = refs[n_x + n_w:n_x + n_w + n_extra]
    out_refs = refs[n_x + n_w + n_extra:]
    accs = [jnp.dot(x_refs[a][...].astype(BF16), w_refs[b][...],
                    preferred_element_type=F32) for a, b in dot_pairs]
    y = epilogue(accs, extra_refs)
    if row_norm:
        o_ref, ssq_ref = out_refs
        gain_ref = extra_refs[-1]
        j = pl.program_id(1)
        o_ref[:, pl.ds(pl.multiple_of(j * bn, bn), bn)] = y
        part = jnp.broadcast_to(jnp.sum(y * y, axis=-1, keepdims=True), ssq_ref.shape)

        @pl.when(j == 0)
        def _():
            ssq_ref[...] = part

        @pl.when(j != 0)
        def _():
            ssq_ref[...] += part

        @pl.when(j == pl.num_programs(1) - 1)
        def _():
            r = _rms_scale(ssq_ref, o_ref.shape[1])
            for c in range(o_ref.shape[1] // LANES):
                cs = slice(c * LANES, (c + 1) * LANES)
                o_ref[:, cs] = o_ref[:, cs] * r * gain_ref[:, cs]
        return
    out_refs[0][...] = y.astype(out_refs[0].dtype)
    if emit_norm:
        gain_ref = extra_refs[-1]
        xg_ref, ssq_ref = out_refs[1:]
        xg_ref[...] = (y * gain_ref[...]).astype(xg_ref.dtype)
        part = jnp.broadcast_to(jnp.sum(y * y, axis=-1, keepdims=True), ssq_ref.shape)
        j = pl.program_id(1)

        @pl.when(j == 0)
        def _():
            ssq_ref[...] = part

        @pl.when(j != 0)
        def _():
            ssq_ref[...] += part


def _matmul(name, xs, ws, extras, epilogue, *, dot_pairs, n_out, out_dtype, bm, bn,
            norm_gain=None, row_norm_gain=None):
    t = xs[0][0].shape[0]
    emit_norm = norm_gain is not None
    row_norm = row_norm_gain is not None
    if emit_norm:
        extras = list(extras) + [("row", norm_gain, 0)]
    if row_norm:
        extras = list(extras) + [("fullrow", row_norm_gain, 0)]
    in_specs, operands = [], []
    for arr, k, cb in xs:
        in_specs.append(pl.BlockSpec((bm, k), lambda i, j, cb=cb: (i, cb)))
        operands.append(arr)
    for arr, cb0 in ws:
        k = arr.shape[0]
        in_specs.append(pl.BlockSpec((k, bn), lambda i, j, cb0=cb0: (0, cb0 + j)))
        operands.append(arr)
    for kind, arr, cb0 in extras:
        if kind == "tile":
            in_specs.append(pl.BlockSpec((bm, bn), lambda i, j, cb0=cb0: (i, cb0 + j)))
        elif kind == "row":
            in_specs.append(pl.BlockSpec((1, bn), lambda i, j, cb0=cb0: (0, cb0 + j)))
        elif kind == "rowstat":
            in_specs.append(pl.BlockSpec((bm, LANES), lambda i, j: (i, 0)))
        else:
            in_specs.append(pl.BlockSpec((1, n_out), lambda i, j: (0, 0)))
        operands.append(arr)
    tile_spec = pl.BlockSpec((bm, bn), lambda i, j: (i, j))
    stat_spec = pl.BlockSpec((bm, LANES), lambda i, j: (i, 0))
    out_shape = jax.ShapeDtypeStruct((t, n_out), out_dtype)
    out_specs = tile_spec
    scratch_shapes = []
    if emit_norm:
        out_shape = (out_shape, jax.ShapeDtypeStruct((t, n_out), BF16),
                     jax.ShapeDtypeStruct((t, LANES), F32))
        out_specs = (tile_spec, tile_spec, stat_spec)
    if row_norm:
        assert out_dtype == F32 and not emit_norm
        out_specs = pl.BlockSpec((bm, n_out), lambda i, j: (i, 0))
        scratch_shapes = [pltpu.VMEM((bm, LANES), F32)]
    sequential_cols = emit_norm or row_norm
    return pl.pallas_call(
        functools.partial(_mm_kernel, n_x=len(xs), n_w=len(ws), n_extra=len(extras),
                          dot_pairs=dot_pairs, epilogue=epilogue, emit_norm=emit_norm,
                          row_norm=row_norm, bn=bn),
        out_shape=out_shape,
        grid=(t // bm, n_out // bn),
        in_specs=in_specs,
        out_specs=out_specs,
        scratch_shapes=scratch_shapes,
        compiler_params=pltpu.CompilerParams(
            dimension_semantics=("parallel", "arbitrary" if sequential_cols else "parallel"),
            vmem_limit_bytes=V7X_VMEM_LIMIT_BYTES,
            allow_input_fusion=[False] * len(xs) + [True] * len(ws) + [False] * len(extras)),
        name=name,
    )(*operands)


def _rms_scale(ssq_ref, n):
    return lax.rsqrt(ssq_ref[:, :1] / n + RMS_EPS)


def _gelu_exact(x):
    return 0.5 * x * (1.0 + lax.erf(x * math.sqrt(0.5)))


def _ep_gelu(accs, extras):
    return _gelu_exact(accs[0])


def _ep_colscale(accs, extras):
    return accs[0] * extras[0][...]


def _ep_sigmoid_bias(accs, extras):
    return jax.nn.sigmoid(accs[0] + extras[0][...])


def _ep_gated(accs, extras):
    return extras[0][...].astype(F32) * accs[0]


def _ep_gated_add(accs, extras):
    return extras[1][...] + extras[0][...].astype(F32) * accs[0]


def _ep_residual(accs, extras):
    return extras[0][...] + accs[0]


def _ep_ple_normed(accs, extras, *, n):
    r = _rms_scale(extras[1], n)
    return extras[0][...] + jax.nn.sigmoid(r * accs[0]) * accs[1]


def _swiglu_up_kernel(x_ref, wg_ref, wu_ref, ssq_ref, o_ref, w_ref, *, bn, n):
    @pl.when(pl.program_id(1) == 0)
    def _():
        w_ref[:, :bn] = wg_ref[...].astype(BF16)
        w_ref[:, bn:] = wu_ref[...].astype(BF16)

    acc = jnp.dot(x_ref[...], w_ref[...], preferred_element_type=F32)
    r = _rms_scale(ssq_ref, n)
    o_ref[...] = (jax.nn.silu(r * acc[:, :bn]) * (r * acc[:, bn:])).astype(o_ref.dtype)


def _swiglu_up(xg, ssq, w_gu, *, bm, bn):
    t, d = xg.shape
    d_ff = w_gu.shape[1] // 2
    return pl.pallas_call(
        functools.partial(_swiglu_up_kernel, bn=bn, n=d),
        out_shape=jax.ShapeDtypeStruct((t, d_ff), BF16),
        grid=(d_ff // bn, t // bm),
        in_specs=[pl.BlockSpec((bm, d), lambda j, i: (i, 0)),
                  pl.BlockSpec((d, bn), lambda j, i: (0, j)),
                  pl.BlockSpec((d, bn), lambda j, i: (0, d_ff // bn + j)),
                  pl.BlockSpec((bm, LANES), lambda j, i: (i, 0))],
        out_specs=pl.BlockSpec((bm, bn), lambda j, i: (i, j)),
        scratch_shapes=[pltpu.VMEM((d, 2 * bn), BF16)],
        compiler_params=_params("arbitrary", "arbitrary"),
        name="ffn_up",
    )(xg, w_gu, w_gu, ssq)


def _gmlp_kernel(gu_ref, gv_ref, lng_ref, lnb_ref, ws_ref, bs_ref, o_ref, wm_ref,
                 *, rows, n_groups, group_dim):
    @pl.when(pl.program_id(0) == 0)
    def _():
        r = lax.broadcasted_iota(jnp.int32, (CHUNK, CHUNK), 0)
        c = lax.broadcasted_iota(jnp.int32, (CHUNK, CHUNK), 1)
        for g in range(n_groups):
            wm_ref[g] = jnp.where(c <= r, ws_ref[g], 0.0).astype(BF16)

    for ch in range(rows // CHUNK):
        rs = slice(ch * CHUNK, (ch + 1) * CHUNK)
        v = gv_ref[rs, :].astype(F32)
        mu = jnp.mean(v, axis=-1, keepdims=True)
        vc = v - mu
        var = jnp.mean(vc * vc, axis=-1, keepdims=True)
        vn = (vc * lax.rsqrt(var + LN_EPS) * lng_ref[...] + lnb_ref[...]).astype(BF16)
        for g in range(n_groups):
            cs = slice(g * group_dim, (g + 1) * group_dim)
            mixed = jnp.dot(wm_ref[g], vn[:, cs], preferred_element_type=F32) + bs_ref[g]
            o_ref[rs, cs] = (gu_ref[rs, cs].astype(F32) * mixed).astype(o_ref.dtype)


def _gmlp(guv, ln_g, ln_b, w_s, b_s, *, rows=512):
    t = guv.shape[0]
    d = guv.shape[1] // 2
    n_groups = w_s.shape[0]
    return pl.pallas_call(
        functools.partial(_gmlp_kernel, rows=rows, n_groups=n_groups, group_dim=d // n_groups),
        out_shape=jax.ShapeDtypeStruct((t, d), BF16),
        grid=(t // rows,),
        in_specs=[pl.BlockSpec((rows, d), lambda i: (i, 0)),
                  pl.BlockSpec((rows, d), lambda i: (i, 1)),
                  pl.BlockSpec((1, d), lambda i: (0, 0)),
                  pl.BlockSpec((1, d), lambda i: (0, 0)),
                  pl.BlockSpec((n_groups, CHUNK, CHUNK), lambda i: (0, 0, 0)),
                  pl.BlockSpec((n_groups, CHUNK, 1), lambda i: (0, 0, 0))],
        out_specs=pl.BlockSpec((rows, d), lambda i: (i, 0)),
        scratch_shapes=[pltpu.VMEM((n_groups, CHUNK, CHUNK), BF16)],
        compiler_params=_params("arbitrary"),
        name="gmlp_gating",
    )(guv, guv, ln_g.reshape(1, d), ln_b.reshape(1, d), w_s, b_s.reshape(n_groups, CHUNK, 1))


def _lanes(x, n):
    return jnp.concatenate([x] * (n // LANES), axis=1)


def _attn_kernel(q_ref, k_ref, v_ref, lq1_ref, lk1_ref, lq2_ref, lk2_ref, sg_ref, o_ref,
                 s_ref, p_ref, m_ref, l_ref, a_ref, acc_ref,
                 *, blk, strip, n_q_blocks, n_heads, lam_init):
    h = pl.program_id(1)
    slope = jnp.exp2(-8.0 * jnp.full((1, blk), h + 1, F32) / n_heads) * LOG2_E
    k_idx = lax.broadcasted_iota(jnp.int32, (1, blk), 1).astype(F32)
    lam = (jnp.exp(jnp.sum(lq1_ref[...] * lk1_ref[...], axis=-1, keepdims=True))
           - jnp.exp(jnp.sum(lq2_ref[...] * lk2_ref[...], axis=-1, keepdims=True))
           + lam_init)

    def query_block(i, carry):
        _attn_query_block(i, slope, k_idx, lam, q_ref, k_ref, v_ref, sg_ref, o_ref,
                          s_ref, p_ref, m_ref, l_ref, a_ref, acc_ref,
                          blk=blk, strip=strip, n_q_blocks=n_q_blocks, lam_init=lam_init)
        return carry

    lax.fori_loop(0, n_q_blocks, query_block, 0)


def _attn_query_block(i, slope, k_idx, lam, q_ref, k_ref, v_ref, sg_ref, o_ref,
                      s_ref, p_ref, m_ref, l_ref, a_ref, acc_ref,
                      *, blk, strip, n_q_blocks, lam_init):
    q_rows = pl.ds(pl.multiple_of(i * blk, blk), blk)

    def begin():
        m_ref[...] = jnp.full(m_ref.shape, MASK_VALUE, F32)
        l_ref[...] = jnp.zeros(l_ref.shape, F32)
        acc_ref[...] = jnp.zeros(acc_ref.shape, F32)
        qk(0, 0)

    def qk(mp, j):
        off = pl.multiple_of(j * blk, blk)
        cs = slice(mp * HEAD_DIM, (mp + 1) * HEAD_DIM)
        s_ref[mp] = lax.dot_general(q_ref[q_rows, cs], k_ref[pl.ds(off, blk), cs],
                                    (((1,), (1,)), ((), ())),
                                    preferred_element_type=F32)

    def pv(mp, j):
        off = pl.multiple_of(j * blk, blk)
        acc_ref[mp] = (acc_ref[mp] * _lanes(a_ref[mp], V_HEAD_DIM)
                       + jnp.dot(p_ref[mp], v_ref[pl.ds(off, blk), :],
                                 preferred_element_type=F32))

    def softmax_strips(mp, bias, masked):
        for r in range(blk // strip):
            rows = slice(r * strip, (r + 1) * strip)
            s = s_ref[mp, rows, :] + bias
            if masked:
                row = r * strip + lax.broadcasted_iota(jnp.int32, (strip, blk), 0)
                col = lax.broadcasted_iota(jnp.int32, (strip, blk), 1)
                s = jnp.where(col <= row, s, MASK_VALUE)
            m_old = m_ref[mp, rows, :]
            m_new = jnp.maximum(m_old, jnp.max(s, axis=-1, keepdims=True))
            alpha = jnp.exp2(m_old - m_new)
            p = jnp.exp2(s - _lanes(m_new, blk))
            p_sum = p[:, :LANES]
            for c in range(1, blk // LANES):
                p_sum = p_sum + p[:, c * LANES:(c + 1) * LANES]
            l_ref[mp, rows, :] = alpha * l_ref[mp, rows, :] + p_sum
            m_ref[mp, rows, :] = m_new
            a_ref[mp, rows, :] = alpha
            p_ref[mp, rows, :] = p.astype(BF16)

    def step(j, masked, first=False):
        bias = slope * (k_idx + jnp.asarray((j - i) * blk, F32))
        if not first:
            pv(1, j - 1)
        qk(1, j)
        softmax_strips(0, bias, masked)
        pv(0, j)
        if not masked:
            qk(0, j + 1)
        softmax_strips(1, bias, masked)

    def regions(selected, is_first, can_be_later, body):
        for first in (True, False) if can_be_later else (True,):
            @pl.when(jnp.logical_and(selected, is_first if first else jnp.logical_not(is_first)))
            def _(first=first):
                if first:
                    begin()
                body(first)

    run = n_q_blocks // 2
    while run >= TAIL_RUNS:
        def run_body(first, run=run):
            start = lax.bitwise_and(i, -2 * run)
            for u in range(run):
                step(start + u, masked=False, first=first and u == 0)
        regions(lax.bitwise_and(i, run) != 0, i < 2 * run, 2 * run < n_q_blocks, run_body)
        run //= 2

    def tail(n_rest, first):
        for u in range(n_rest):
            step(i - n_rest + u, masked=False, first=first and u == 0)
        step(i, masked=True, first=first and n_rest == 0)
        pv(1, i)
        l1 = jnp.sum(l_ref[0], axis=-1, keepdims=True)
        l2 = jnp.sum(l_ref[1], axis=-1, keepdims=True)
        o = acc_ref[0] / l1 - lam * (acc_ref[1] / l2)
        ms = jnp.mean(o * o, axis=-1, keepdims=True)
        y = (o * lax.rsqrt(ms + LN_EPS) * sg_ref[...]) * (1.0 - lam_init)
        o_ref[q_rows, :] = y.astype(o_ref.dtype)

    for n_rest in range(TAIL_RUNS):
        regions(lax.bitwise_and(i, TAIL_RUNS - 1) == n_rest, i < TAIL_RUNS,
                TAIL_RUNS < n_q_blocks, functools.partial(tail, n_rest))


def _diff_attention(qkv, lq1, lk1, lq2, lk2, subln_g, *, batch, seq, n_heads, lam_init,
                    blk=512, strip=32):
    t = qkv.shape[0]
    nq = seq // blk
    assert nq & (nq - 1) == 0, "the key-block walk assumes a power-of-two block count"
    vec = lambda a: a.reshape(1, -1)
    vec_spec = lambda n: pl.BlockSpec((1, n), lambda b, h: (0, 0))
    return pl.pallas_call(
        functools.partial(_attn_kernel, blk=blk, strip=strip, n_q_blocks=nq, n_heads=n_heads,
                          lam_init=lam_init),
        out_shape=jax.ShapeDtypeStruct((t, n_heads * V_HEAD_DIM), BF16),
        grid=(batch, n_heads),
        in_specs=[pl.BlockSpec((seq, 2 * HEAD_DIM), lambda b, h: (b, h)),
                  pl.BlockSpec((seq, 2 * HEAD_DIM), lambda b, h: (b, n_heads + h)),
                  pl.BlockSpec((seq, V_HEAD_DIM), lambda b, h: (b, 2 * n_heads + h)),
                  vec_spec(HEAD_DIM), vec_spec(HEAD_DIM), vec_spec(HEAD_DIM), vec_spec(HEAD_DIM),
                  vec_spec(V_HEAD_DIM)],
        out_specs=pl.BlockSpec((seq, V_HEAD_DIM), lambda b, h: (b, h)),
        scratch_shapes=[pltpu.VMEM((2, blk, blk), F32),
                        pltpu.VMEM((2, blk, blk), BF16),
                        pltpu.VMEM((2, blk, LANES), F32),
                        pltpu.VMEM((2, blk, LANES), F32),
                        pltpu.VMEM((2, blk, LANES), F32),
                        pltpu.VMEM((2, blk, V_HEAD_DIM), F32)],
        compiler_params=_params("parallel", "parallel"),
        name="diff_attention",
    )(qkv, qkv, qkv, vec(lq1), vec(lk1), vec(lq2), vec(lk2), vec(subln_g))


def kernel(x, p, g_mix, w_in, b_gate, ln_v_g, ln_v_b, w_s, b_s, lambda_q1, lambda_k1, lambda_q2, lambda_k2, subln_g, w_br_a, w_br_b, w_o, g_ffn, w_gu, w_down, g_ple, w_ple_gate, w_ple_proj, g_final):
    batch, seq, d_model = x.shape
    depth = w_in.shape[0]
    t = batch * seq
    d_gmlp = ln_v_g.shape[1]
    d_attn = w_br_b.shape[1]
    n_heads = d_attn // V_HEAD_DIM
    d_qk = n_heads * 2 * HEAD_DIM
    d_ff = w_down.shape[1]
    ple_dim = p.shape[-1]

    xf = x.reshape(t, d_model)
    for layer in range(depth):
        lam_init = 0.8 - 0.6 * math.exp(-0.3 * layer)
        w_in_b = w_in[layer].astype(BF16)

        h = _rmsnorm(xf, g_mix[layer], BF16)
        hx = [(h, d_model, 0)]
        bn = 1024
        guv = _matmul("in_proj_gelu", hx, [(w_in_b, 0)], [], _ep_gelu, dot_pairs=((0, 0),),
                      n_out=2 * d_gmlp, out_dtype=BF16, bm=1024, bn=bn)
        qkv_scale = jnp.concatenate([jnp.full((d_qk,), HEAD_DIM ** -0.5 * LOG2_E, F32),
                                     jnp.ones((d_qk + d_attn,), F32)]).reshape(1, -1)
        qkv = _matmul("in_proj_qkv", hx, [(w_in_b, 2 * d_gmlp // bn)], [("row", qkv_scale, 0)], _ep_colscale,
                      dot_pairs=((0, 0),), n_out=2 * d_qk + d_attn, out_dtype=BF16, bm=1024, bn=bn)
        gates = _matmul("in_proj_gates", hx, [(w_in_b, (2 * d_gmlp + 2 * d_qk + d_attn) // bn)],
                        [("row", b_gate[layer].reshape(1, -1), 0)], _ep_sigmoid_bias,
                        dot_pairs=((0, 0),), n_out=2 * d_model, out_dtype=BF16, bm=1024, bn=bn)

        y_a = _gmlp(guv, ln_v_g[layer], ln_v_b[layer], w_s[layer], b_s[layer])
        y_b = _diff_attention(qkv, lambda_q1[layer], lambda_k1[layer], lambda_q2[layer],
                              lambda_k2[layer], subln_g[layer], batch=batch, seq=seq,
                              n_heads=n_heads, lam_init=lam_init)

        bn = 1024
        merged_a = _matmul("merge_a", [(y_a, d_gmlp, 0)], [(w_br_a[layer].astype(BF16), 0)],
                           [("tile", gates, 0)], _ep_gated, dot_pairs=((0, 0),),
                           n_out=d_model, out_dtype=F32, bm=1024, bn=bn)
        merged = _matmul("merge_b", [(y_b, d_attn, 0)], [(w_br_b[layer].astype(BF16), 0)],
                         [("tile", gates, d_model // bn), ("tile", merged_a, 0)], _ep_gated_add,
                         dot_pairs=((0, 0),), n_out=d_model, out_dtype=BF16, bm=1024, bn=bn)
        xf, xg, ssq = _matmul("out_proj", [(merged, d_model, 0)], [(w_o[layer].astype(BF16), 0)],
                              [("tile", xf, 0)], _ep_residual, dot_pairs=((0, 0),),
                              n_out=d_model, out_dtype=F32, bm=1024, bn=512,
                              norm_gain=g_ffn[layer].reshape(1, -1))

        act = _swiglu_up(xg, ssq, w_gu[layer], bm=1024, bn=256)
        xf, xg, ssq = _matmul("ffn_down", [(act, d_ff, 0)], [(w_down[layer].astype(BF16), 0)],
                              [("tile", xf, 0)], _ep_residual, dot_pairs=((0, 0),),
                              n_out=d_model, out_dtype=F32, bm=512, bn=512,
                              norm_gain=g_ple[layer].reshape(1, -1))

        ple_bm, ple_bn = (512, 1024) if layer == depth - 1 else (1024, 512)
        xf = _matmul("ple", [(xg, d_model, 0), (p[layer].reshape(t, ple_dim), ple_dim, 0)],
                     [(w_ple_gate[layer].astype(BF16), 0), (w_ple_proj[layer].astype(BF16), 0)],
                     [("tile", xf, 0), ("rowstat", ssq, 0)],
                     functools.partial(_ep_ple_normed, n=d_model), dot_pairs=((0, 0), (1, 1)),
                     n_out=d_model, out_dtype=F32, bm=ple_bm, bn=ple_bn,
                     row_norm_gain=g_final.reshape(1, -1) if layer == depth - 1 else None)

    return xf.reshape(batch, seq, d_model)
```
